```python
import jax, jax.numpy as jnp
from jax import lax
import numpy as np

D_MODEL = 1024
BATCH = 8
SEQ = 8192
DEPTH = 1

GDN_HEADS = 4
GDN_HEAD_DIM = 128
GDN_WIDTH = GDN_HEADS * GDN_HEAD_DIM
CONV_WIDTH = 4
CHUNK = 64
MLA_HEADS = 4
QK_NOPE_DIM = 128
QK_ROPE_DIM = 64
V_HEAD_DIM = 128
Q_LORA_RANK = 384
KV_LORA_RANK = 256
MLA_WIDTH = MLA_HEADS * V_HEAD_DIM
ROPE_THETA = 10000.0
Q_BLOCK = 128
MIX_WIDTH = GDN_WIDTH + MLA_WIDTH
IN_DIM = 4 * GDN_WIDTH + 2 * GDN_HEADS + Q_LORA_RANK + KV_LORA_RANK + QK_ROPE_DIM
PEER_HEADS = 8
N_KEYS = 128
N_EXPERTS = N_KEYS * N_KEYS
PEER_KEY_DIM = 256
PEER_HALF = PEER_KEY_DIM // 2
PEER_TOPK = 16
TOKEN_BLOCK = 128
N_MOD = 6
EPS = 1e-6

kernel_name = "hybrid_gdn_mla_peer_adaln_block"


def rms_norm(x, g):
    xf = x.astype(jnp.float32)
    y = xf * lax.rsqrt(jnp.mean(xf * xf, axis=-1, keepdims=True) + EPS)
    return (y * g.astype(jnp.float32)).astype(x.dtype)


def l2_norm(x):
    return x * lax.rsqrt(jnp.sum(x * x, axis=-1, keepdims=True) + EPS)


def modulate(h, shift, scale):
    return h * (1.0 + scale) + shift


def apply_rotary(x, cos, sin):
    xf = x.astype(jnp.float32)
    x1, x2 = jnp.split(xf, 2, axis=-1)
    out = jnp.concatenate([x1 * cos - x2 * sin, x2 * cos + x1 * sin], axis=-1)
    return out.astype(x.dtype)


def causal_short_conv(x, w):
    ch = x.shape[-1]
    return lax.conv_general_dilated(
        x, w.astype(x.dtype)[:, None, :], window_strides=(1,),
        padding=[(CONV_WIDTH - 1, 0)], dimension_numbers=("NWC", "WIO", "NWC"),
        feature_group_count=ch)


def gated_delta_rule(q, k, v, g, beta):
    b, h, s, dk = q.shape
    dv = v.shape[-1]
    n = s // CHUNK
    q = q * (dk ** -0.5)
    q = q.reshape(b, h, n, CHUNK, dk)
    k = k.reshape(b, h, n, CHUNK, dk)
    v = v.reshape(b, h, n, CHUNK, dv)
    g = g.reshape(b, h, n, CHUNK)
    beta = beta.reshape(b, h, n, CHUNK)
    gc = jnp.cumsum(g, axis=-1)
    incl = jnp.tril(jnp.ones((CHUNK, CHUNK), dtype=bool))
    strict = jnp.tril(jnp.ones((CHUNK, CHUNK), dtype=bool), -1)
    decay = jnp.exp(jnp.where(incl, gc[..., :, None] - gc[..., None, :], -jnp.inf))
    kb = k * beta[..., None]
    m = jnp.where(strict, jnp.einsum('bhnid,bhnjd->bhnij', kb, k) * decay, 0.0)
    a = m + jnp.eye(CHUNK, dtype=m.dtype)
    rhs = jnp.concatenate([v * beta[..., None], kb * jnp.exp(gc)[..., None]], axis=-1)
    sol = lax.linalg.triangular_solve(a, rhs, left_side=True, lower=True, unit_diagonal=True)
    u, w = sol[..., :dv], sol[..., dv:]
    attn = jnp.einsum('bhnid,bhnjd->bhnij', q, k) * decay
    q_dec = q * jnp.exp(gc)[..., None]
    k_dec = k * jnp.exp(gc[..., -1:] - gc)[..., None]
    g_last = jnp.exp(gc[..., -1])
    xs = (jnp.moveaxis(q_dec, 2, 0), jnp.moveaxis(k_dec, 2, 0), jnp.moveaxis(u, 2, 0),
          jnp.moveaxis(w, 2, 0), jnp.moveaxis(attn, 2, 0), jnp.moveaxis(g_last, 2, 0))

    def step(state, inp):
        qd, kd, ui, wi, ai, gl = inp
        v_new = ui - jnp.einsum('bhcd,bhde->bhce', wi, state)
        o = jnp.einsum('bhcd,bhde->bhce', qd, state) + jnp.einsum('bhij,bhje->bhie', ai, v_new)
        state = state * gl[..., None, None] + jnp.einsum('bhcd,bhce->bhde', kd, v_new)
        return state, o

    state0 = jnp.zeros((b, h, dk, dv), dtype=q.dtype)
    _, o = lax.scan(step, state0, xs)
    return jnp.moveaxis(o, 0, 2).reshape(b, h, s, dv)


def gdn_mixer(q, k, v, z, a, bgate, conv_w, a_log, dt_bias, norm_g):
    bsz, s, _ = q.shape
    qkv = jax.nn.silu(causal_short_conv(jnp.concatenate([q, k, v], axis=-1), conv_w))
    qc, kc, vc = jnp.split(qkv.astype(jnp.float32), 3, axis=-1)

    def heads(t):
        return t.reshape(bsz, s, GDN_HEADS, GDN_HEAD_DIM).transpose(0, 2, 1, 3)

    qh, kh, vh = l2_norm(heads(qc)), l2_norm(heads(kc)), heads(vc)
    g = -jnp.exp(a_log.astype(jnp.float32)) * jax.nn.softplus(
        a.astype(jnp.float32) + dt_bias.astype(jnp.float32))
    beta = jax.nn.sigmoid(bgate.astype(jnp.float32))
    o = gated_delta_rule(qh, kh, vh, g.transpose(0, 2, 1), beta.transpose(0, 2, 1))
    o = o.transpose(0, 2, 1, 3)
    zh = z.astype(jnp.float32).reshape(bsz, s, GDN_HEADS, GDN_HEAD_DIM)
    o = rms_norm(o, norm_g) * jax.nn.silu(zh)
    return o.reshape(bsz, s, GDN_WIDTH).astype(q.dtype)


def mla_mixer(c_q, c_kv, k_rope, cos, sin, q_norm_g, w_uq, kv_norm_g, w_ukv):
    bsz, s, _ = c_q.shape
    q = (rms_norm(c_q, q_norm_g) @ w_uq).reshape(bsz, s, MLA_HEADS, QK_NOPE_DIM + QK_ROPE_DIM)
    kv = (rms_norm(c_kv, kv_norm_g) @ w_ukv).reshape(bsz, s, MLA_HEADS, QK_NOPE_DIM + V_HEAD_DIM)
    q_nope, q_rope = q[..., :QK_NOPE_DIM], q[..., QK_NOPE_DIM:]
    k_nope, v = kv[..., :QK_NOPE_DIM], kv[..., QK_NOPE_DIM:]
    q_rope = apply_rotary(q_rope, cos[:, :, None, :], sin[:, :, None, :])
    k_rope = apply_rotary(k_rope, cos, sin)
    k_rope = jnp.broadcast_to(k_rope[:, :, None, :], (bsz, s, MLA_HEADS, QK_ROPE_DIM))
    scale = (QK_NOPE_DIM + QK_ROPE_DIM) ** -0.5
    qf = (jnp.concatenate([q_nope, q_rope], axis=-1) * scale).transpose(0, 2, 1, 3)
    kf = jnp.concatenate([k_nope, k_rope], axis=-1).transpose(0, 2, 1, 3)
    vf = v.transpose(0, 2, 1, 3)
    nb = s // Q_BLOCK
    qb = qf.reshape(bsz, MLA_HEADS, nb, Q_BLOCK, -1).transpose(2, 0, 1, 3, 4)
    key_pos = jnp.arange(s)

    def attend(args):
        q_blk, start = args
        sc = jnp.einsum('bhqd,bhkd->bhqk', q_blk, kf).astype(jnp.float32)
        q_pos = start + jnp.arange(Q_BLOCK)
        sc = jnp.where(key_pos[None, :] <= q_pos[:, None], sc, -jnp.inf)
        p = jax.nn.softmax(sc, axis=-1).astype(vf.dtype)
        return jnp.einsum('bhqk,bhkd->bhqd', p, vf)

    o = lax.map(attend, (qb, jnp.arange(nb, dtype=jnp.int32) * Q_BLOCK))
    return o.transpose(1, 0, 3, 2, 4).reshape(bsz, s, MLA_WIDTH)


def peer_ffn(h, w_pq, sub_keys, expert_u, expert_v):
    bsz, s, d = h.shape
    ht = h.reshape(bsz * s // TOKEN_BLOCK, TOKEN_BLOCK, d)

    def block(xb):
        q = (xb @ w_pq).reshape(TOKEN_BLOCK, PEER_HEADS, 2, PEER_HALF)
        sc = jnp.einsum('thpd,phkd->thpk', q, sub_keys).astype(jnp.float32)
        s_top, i_top = lax.top_k(sc, PEER_TOPK)
        cand = (s_top[:, :, 0, :, None] + s_top[:, :, 1, None, :]).reshape(
            TOKEN_BLOCK, PEER_HEADS, PEER_TOPK * PEER_TOPK)
        cand_idx = (i_top[:, :, 0, :, None] * N_KEYS + i_top[:, :, 1, None, :]).reshape(
            TOKEN_BLOCK, PEER_HEADS, PEER_TOPK * PEER_TOPK)
        best, pos = lax.top_k(cand, PEER_TOPK)
        idx = jnp.take_along_axis(cand_idx, pos, axis=-1)
        gate = jax.nn.softmax(best, axis=-1)
        u = expert_u[idx]
        act = jax.nn.gelu(jnp.einsum('td,thkd->thk', xb, u).astype(jnp.float32))
        v = expert_v[idx]
        return jnp.einsum('thk,thkd->td', (gate * act).astype(xb.dtype), v)

    return lax.map(block, ht).reshape(bsz, s, d)


def setup_inputs(seed: int = 0) -> dict:
    key = jax.random.key(seed)
    ks = jax.random.split(key, 24)
    f32 = jnp.float32
    D = D_MODEL
    nrm = lambda k, shp, sc: jax.random.normal(k, shp, f32) * sc
    x = nrm(ks[0], (BATCH, SEQ, D), 1.0)
    c = nrm(ks[1], (BATCH, D), 1.0)
    start = jax.random.randint(ks[2], (BATCH, 1), 0, 4096, dtype=jnp.int32)
    positions = (start + jnp.arange(SEQ, dtype=jnp.int32)[None, :]).astype(jnp.int32)
    ln_mix_g = 1.0 + nrm(ks[3], (DEPTH, D), 0.02)
    w_in = nrm(ks[4], (DEPTH, D, IN_DIM), D ** -0.5)
    conv_w = nrm(ks[5], (DEPTH, CONV_WIDTH, 3 * GDN_WIDTH), CONV_WIDTH ** -0.5)
    a_log = jnp.log(jax.random.uniform(ks[6], (DEPTH, GDN_HEADS), f32, 1.0, 16.0))
    dt = jnp.exp(jax.random.uniform(ks[7], (DEPTH, GDN_HEADS), f32, np.log(1e-3), np.log(1e-1)))
    dt_bias = dt + jnp.log(-jnp.expm1(-dt))
    gdn_norm_g = 1.0 + nrm(ks[8], (DEPTH, GDN_HEAD_DIM), 0.02)
    q_norm_g = 1.0 + nrm(ks[9], (DEPTH, Q_LORA_RANK), 0.02)
    w_uq = nrm(ks[10], (DEPTH, Q_LORA_RANK, MLA_HEADS * (QK_NOPE_DIM + QK_ROPE_DIM)), Q_LORA_RANK ** -0.5)
    kv_norm_g = 1.0 + nrm(ks[11], (DEPTH, KV_LORA_RANK), 0.02)
    w_ukv = nrm(ks[12], (DEPTH, KV_LORA_RANK, MLA_HEADS * (QK_NOPE_DIM + V_HEAD_DIM)), KV_LORA_RANK ** -0.5)
    w_out = nrm(ks[13], (DEPTH, MIX_WIDTH, D), MIX_WIDTH ** -0.5)
    ln_ffn_g = 1.0 + nrm(ks[14], (DEPTH, D), 0.02)
    w_pq = nrm(ks[15], (DEPTH, D, PEER_HEADS * PEER_KEY_DIM), D ** -0.5)
    sub_keys = nrm(ks[16], (DEPTH, 2, PEER_HEADS, N_KEYS, PEER_HALF), PEER_HALF ** -0.5)
    expert_u = nrm(ks[17], (DEPTH, N_EXPERTS, D), D ** -0.5)
    expert_v = nrm(ks[18], (DEPTH, N_EXPERTS, D), PEER_HEADS ** -0.5)
    w_ada = nrm(ks[19], (DEPTH, D, N_MOD * D), D ** -0.5)
    b_ada = nrm(ks[20], (DEPTH, N_MOD * D), 0.02)
    final_norm_g = 1.0 + nrm(ks[21], (D,), 0.02)
    return {"x": x, "c": c, "positions": positions, "ln_mix_g": ln_mix_g, "w_in": w_in,
            "conv_w": conv_w, "a_log": a_log, "dt_bias": dt_bias, "gdn_norm_g": gdn_norm_g,
            "q_norm_g": q_norm_g, "w_uq": w_uq, "kv_norm_g": kv_norm_g, "w_ukv": w_ukv,
            "w_out": w_out, "ln_ffn_g": ln_ffn_g, "w_pq": w_pq, "sub_keys": sub_keys,
            "expert_u": expert_u, "expert_v": expert_v, "w_ada": w_ada, "b_ada": b_ada,
            "final_norm_g": final_norm_g}


def reference(x, c, positions, ln_mix_g, w_in, conv_w, a_log, dt_bias, gdn_norm_g,
              q_norm_g, w_uq, kv_norm_g, w_ukv, w_out, ln_ffn_g, w_pq, sub_keys,
              expert_u, expert_v, w_ada, b_ada, final_norm_g):
    half = QK_ROPE_DIM // 2
    inv_freq = ROPE_THETA ** (-jnp.arange(half, dtype=jnp.float32) / half)
    ang = positions.astype(jnp.float32)[..., None] * inv_freq
    cos, sin = jnp.cos(ang), jnp.sin(ang)
    sizes = [GDN_WIDTH] * 4 + [GDN_HEADS] * 2 + [Q_LORA_RANK, KV_LORA_RANK, QK_ROPE_DIM]
    split_at = [int(v) for v in np.cumsum(sizes)[:-1]]
    c_act = jax.nn.silu(c)
    for layer in range(DEPTH):
        mod = (c_act @ w_ada[layer] + b_ada[layer])[:, None, :]
        sh1, sc1, gt1, sh2, sc2, gt2 = jnp.split(mod, N_MOD, axis=-1)
        h = modulate(rms_norm(x, ln_mix_g[layer]), sh1, sc1)
        proj = h @ w_in[layer]
        gq, gk, gv, gz, ga, gb, cq, ckv, kr = jnp.split(proj, split_at, axis=-1)
        o_gdn = gdn_mixer(gq, gk, gv, gz, ga, gb, conv_w[layer], a_log[layer],
                          dt_bias[layer], gdn_norm_g[layer])
        o_mla = mla_mixer(cq, ckv, kr, cos, sin, q_norm_g[layer], w_uq[layer],
                          kv_norm_g[layer], w_ukv[layer])
        mixed = jnp.concatenate([o_gdn, o_mla.astype(o_gdn.dtype)], axis=-1) @ w_out[layer]
        x = x + gt1 * mixed
        h = modulate(rms_norm(x, ln_ffn_g[layer]), sh2, sc2)
        x = x + gt2 * peer_ffn(h, w_pq[layer], sub_keys[layer], expert_u[layer], expert_v[layer])
    return rms_norm(x, final_norm_g)
```

```python
import functools

import jax
import jax.numpy as jnp
from jax import lax
from jax.experimental import pallas as pl
from jax.experimental.pallas import tpu as pltpu
from jax.experimental.pallas import tpu_sc as plsc

GDN_HEADS = 4
GDN_HEAD_DIM = 128
CONV_WIDTH = 4
CHUNK = 64
MLA_HEADS = 4
QK_NOPE_DIM = 128
QK_ROPE_DIM = 64
V_HEAD_DIM = 128
Q_LORA_RANK = 384
KV_LORA_RANK = 256
ROPE_THETA = 10000.0
PEER_HEADS = 8
N_KEYS = 128
PEER_TOPK = 16
N_MOD = 6
EPS = 1e-6

LANES = 128
SUBLANES = 8
SC_CORES = 2
SC_SUBCORES = 16
SC_LANES = 16
SC_WORKERS = SC_CORES * SC_SUBCORES
VMEM_LIMIT = 48 * 1024 * 1024

F32 = jnp.float32
BF16 = jnp.bfloat16
HI = lax.Precision.HIGHEST


def _dot(a, b, precision=None):
    return jnp.dot(a, b, preferred_element_type=F32, precision=precision)


def _dot_nt(a, b, precision=None):
    return lax.dot_general(a, b, (((1,), (1,)), ((), ())),
                           preferred_element_type=F32, precision=precision)


def _dot_tn(a, b, precision=None):
    return lax.dot_general(a, b, (((0,), (0,)), ((), ())),
                           preferred_element_type=F32, precision=precision)


def _rms(x):
    return x * lax.rsqrt(jnp.mean(x * x, axis=-1, keepdims=True) + EPS)


def _silu(x):
    return x * jax.nn.sigmoid(x)


def _params(*sem):
    return pltpu.CompilerParams(dimension_semantics=sem, vmem_limit_bytes=VMEM_LIMIT)


def _mod_kernel(c_ref, w_ref, b_ref, o_ref):
    o_ref[...] = _dot(_silu(c_ref[...]), w_ref[...], HI) + b_ref[...]


def _mod(c, w_ada, b_ada):
    bsz, d = c.shape
    n = w_ada.shape[1]
    return pl.pallas_call(
        _mod_kernel,
        grid=(n // d,),
        in_specs=[pl.BlockSpec((bsz, d), lambda j: (0, 0)),
                  pl.BlockSpec((d, d), lambda j: (0, j)),
                  pl.BlockSpec((1, d), lambda j: (0, j))],
        out_specs=pl.BlockSpec((bsz, d), lambda j: (0, j)),
        out_shape=jax.ShapeDtypeStruct((bsz, n), F32),
        compiler_params=_params("arbitrary"),
        name="mod",
    )(c, w_ada, b_ada.reshape(1, n))


def _inproj_kernel(x_ref, mod_ref, g_ref, w_ref, gdn_ref, mla_ref):
    m = mod_ref[0]
    h = _rms(x_ref[0]) * g_ref[...] * (1.0 + m[1:2]) + m[0:1]
    p = _dot(h.astype(BF16), w_ref[...])
    gw = gdn_ref.shape[-1]
    gdn_ref[0] = p[:, :gw]
    mla_ref[0] = p[:, gw:]


def _in_proj(x, mod, g, w_cat, gdn_cols, mla_cols, tm):
    bsz, s, d = x.shape
    return pl.pallas_call(
        _inproj_kernel,
        grid=(bsz, s // tm),
        in_specs=[pl.BlockSpec((1, tm, d), lambda b, i: (b, i, 0)),
                  pl.BlockSpec((1, N_MOD, d), lambda b, i: (b, 0, 0)),
                  pl.BlockSpec((1, d), lambda b, i: (0, 0)),
                  pl.BlockSpec((d, gdn_cols + mla_cols), lambda b, i: (0, 0))],
        out_specs=[pl.BlockSpec((1, tm, gdn_cols), lambda b, i: (b, i, 0)),
                   pl.BlockSpec((1, tm, mla_cols), lambda b, i: (b, i, 0))],
        out_shape=[jax.ShapeDtypeStruct((bsz, s, gdn_cols), F32),
                   jax.ShapeDtypeStruct((bsz, s, mla_cols), F32)],
        compiler_params=_params("parallel", "parallel"),
        name="in_proj",
    )(x, mod, g, w_cat)


def _gdn_kernel(q_ref, k_ref, v_ref, z_ref, ab_ref, cwq_ref, cwk_ref, cwv_ref,
                alog_ref, dtb_ref, ng_ref, o_ref, state_ref, tail_ref, buf_ref, vnew_ref):
    hd = pl.program_id(1)
    sb = q_ref.shape[1]
    nchunk = sb // CHUNK

    @pl.when(pl.program_id(2) == 0)
    def _():
        state_ref[...] = jnp.zeros_like(state_ref)
        tail_ref[...] = jnp.zeros_like(tail_ref)

    def conv_silu(x_ref, cw_ref, slot):
        x = x_ref[0]
        buf_ref[0:SUBLANES, :] = tail_ref[slot]
        buf_ref[SUBLANES:SUBLANES + sb, :] = x
        tail_ref[slot] = x[sb - SUBLANES:sb, :]
        cw = cw_ref[0]
        y = cw[CONV_WIDTH - 1:CONV_WIDTH] * x
        for j in range(CONV_WIDTH - 1):
            off = SUBLANES - (CONV_WIDTH - 1) + j
            y = y + cw[j:j + 1] * buf_ref[off:off + sb, :]
        return _silu(y)

    def l2n(x):
        return x * lax.rsqrt(jnp.sum(x * x, axis=-1, keepdims=True) + EPS)

    q = l2n(conv_silu(q_ref, cwq_ref, 0)) * (GDN_HEAD_DIM ** -0.5)
    k = l2n(conv_silu(k_ref, cwk_ref, 1))
    v = conv_silu(v_ref, cwv_ref, 2)

    ab = ab_ref[0]
    lane = lax.broadcasted_iota(jnp.int32, ab.shape, 1)
    a = jnp.sum(jnp.where(lane == hd, ab, 0.0), axis=-1, keepdims=True)
    bg = jnp.sum(jnp.where(lane == hd + GDN_HEADS, ab, 0.0), axis=-1, keepdims=True)
    g = -jnp.exp(alog_ref[0]) * jax.nn.softplus(a + dtb_ref[0])
    beta = jax.nn.sigmoid(bg)

    ri = lax.broadcasted_iota(jnp.int32, (sb, sb), 0)
    ci = lax.broadcasted_iota(jnp.int32, (sb, sb), 1)
    same = (ri // CHUNK) == (ci // CHUNK)
    incl = same & (ci <= ri)
    strict = same & (ci < ri)
    incl_f = incl.astype(F32)
    same_f = same.astype(F32)

    gc = _dot(incl_f, g, HI)
    gcl = _dot(same_f, g, HI)
    g_wide = jnp.broadcast_to(g[:, 0:1], (sb, sb))
    gc_row = _dot_tn(g_wide, (same & (ri <= ci)).astype(F32), HI)
    decay = jnp.where(incl, jnp.exp(jnp.where(incl, gc[:, 0:1] - gc_row, 0.0)), 0.0)

    kb = k * beta
    mmat = jnp.where(strict, _dot_nt(kb, k, HI) * decay, 0.0)
    tinv = (ri == ci).astype(F32)
    bs = 1
    while bs < CHUNK:
        off = ((ri // bs) % 2 == 1) & ((ci // bs) % 2 == 0) & ((ri // (2 * bs)) == (ci // (2 * bs)))
        a_off = jnp.where(off, mmat, 0.0)
        if bs == 1:
            tinv = tinv - a_off
        else:
            tinv = tinv - _dot(_dot(tinv, a_off, HI), tinv, HI)
        bs *= 2

    egc = jnp.exp(gc)
    rhs = jnp.concatenate([v * beta, kb * egc], axis=1)
    sol = _dot(tinv, rhs, HI)
    u = sol[:, :GDN_HEAD_DIM]
    w = sol[:, GDN_HEAD_DIM:]
    attn = _dot_nt(q, k, HI) * decay
    q_dec = q * egc
    k_dec = k * jnp.exp(gcl - gc)
    g_last = jnp.exp(gcl)

    st = state_ref[...]
    o_parts = []
    for n in range(nchunk):
        lo = n * CHUNK
        v_new = u[lo:lo + CHUNK] - _dot(w[lo:lo + CHUNK], st, HI)
        vnew_ref[lo:lo + CHUNK, :] = v_new
        o_parts.append(_dot(q_dec[lo:lo + CHUNK], st, HI))
        st = st * g_last[lo:lo + 1, :] + _dot_tn(k_dec[lo:lo + CHUNK], v_new, HI)
    state_ref[...] = st
    o = jnp.concatenate(o_parts, axis=0) + _dot(attn, vnew_ref[...], HI)

    o = _rms(o) * ng_ref[...] * _silu(z_ref[0])
    o_ref[0] = o.astype(o_ref.dtype)


def _gdn(gdn_in, conv_w, a_log, dt_bias, norm_g, sb):
    bsz, s, _ = gdn_in.shape
    hdim = GDN_HEAD_DIM
    nh = GDN_HEADS
    cw = conv_w.reshape(CONV_WIDTH, 3 * nh, hdim).transpose(1, 0, 2)
    alog = jnp.broadcast_to(a_log.reshape(nh, 1, 1), (nh, 1, hdim)).astype(F32)
    dtb = jnp.broadcast_to(dt_bias.reshape(nh, 1, 1), (nh, 1, hdim)).astype(F32)

    def col(off):
        return pl.BlockSpec((1, sb, hdim), lambda b, h, i: (b, i, h + off))

    def cws(off):
        return pl.BlockSpec((1, CONV_WIDTH, hdim), lambda b, h, i: (h + off, 0, 0))

    hspec = pl.BlockSpec((1, 1, hdim), lambda b, h, i: (h, 0, 0))
    return pl.pallas_call(
        _gdn_kernel,
        grid=(bsz, nh, s // sb),
        in_specs=[col(0), col(nh), col(2 * nh), col(3 * nh),
                  pl.BlockSpec((1, sb, hdim), lambda b, h, i: (b, i, 4 * nh)),
                  cws(0), cws(nh), cws(2 * nh), hspec, hspec,
                  pl.BlockSpec((1, hdim), lambda b, h, i: (0, 0))],
        out_specs=pl.BlockSpec((1, sb, hdim), lambda b, h, i: (b, i, h)),
        out_shape=jax.ShapeDtypeStruct((bsz, s, nh * hdim), BF16),
        scratch_shapes=[pltpu.VMEM((hdim, hdim), F32),
                        pltpu.VMEM((3, SUBLANES, hdim), F32),
                        pltpu.VMEM((sb + SUBLANES, hdim), F32),
                        pltpu.VMEM((sb, hdim), F32)],
        compiler_params=_params("parallel", "parallel", "arbitrary"),
        name="gdn",
    )(gdn_in, gdn_in, gdn_in, gdn_in, gdn_in, cw, cw, cw, alog, dtb, norm_g.reshape(1, hdim))


MLA_QK_PAD = 256


def _mla_prep_kernel(m_ref, cos_ref, sin_ref, gq_ref, gkv_ref, wq_ref, wkv_ref,
                     q_ref, k_ref, v_ref):
    m = m_ref[0]
    cosr = cos_ref[0]
    sinr = sin_ref[0]
    cq = m[:, :Q_LORA_RANK]
    ckv = m[:, Q_LORA_RANK:Q_LORA_RANK + KV_LORA_RANK]
    o = Q_LORA_RANK + KV_LORA_RANK
    kr = m[:, o:o + LANES]
    krs = m[:, o + LANES:o + 2 * LANES]
    scale = (QK_NOPE_DIM + QK_ROPE_DIM) ** -0.5
    qa = _dot((_rms(cq) * gq_ref[...]).astype(BF16), wq_ref[...])
    kva = _dot((_rms(ckv) * gkv_ref[...]).astype(BF16), wkv_ref[...])
    k_rope = (kr * cosr + krs * sinr).astype(k_ref.dtype)
    sw0 = MLA_HEADS * MLA_QK_PAD
    for h in range(MLA_HEADS):
        b0 = h * MLA_QK_PAD
        rope = qa[:, b0 + LANES:b0 + 2 * LANES] * cosr + qa[:, sw0 + h * LANES:sw0 + (h + 1) * LANES] * sinr
        q_ref[0, h, :, 0:LANES] = (qa[:, b0:b0 + LANES] * scale).astype(q_ref.dtype)
        q_ref[0, h, :, LANES:2 * LANES] = (rope * scale).astype(q_ref.dtype)
        c0 = h * (QK_NOPE_DIM + V_HEAD_DIM)
        k_ref[0, h, :, 0:LANES] = kva[:, c0:c0 + QK_NOPE_DIM].astype(k_ref.dtype)
        k_ref[0, h, :, LANES:2 * LANES] = k_rope
        v_ref[0, h] = kva[:, c0 + QK_NOPE_DIM:c0 + QK_NOPE_DIM + V_HEAD_DIM].astype(v_ref.dtype)


def _mla_prep(mla_in, cosr, sinr, gq, gkv, wq, wkv, tm):
    bsz, s, mc = mla_in.shape
    nh = MLA_HEADS
    return pl.pallas_call(
        _mla_prep_kernel,
        grid=(bsz, s // tm),
        in_specs=[pl.BlockSpec((1, tm, mc), lambda b, i: (b, i, 0)),
                  pl.BlockSpec((1, tm, LANES), lambda b, i: (b, i, 0)),
                  pl.BlockSpec((1, tm, LANES), lambda b, i: (b, i, 0)),
                  pl.BlockSpec((1, Q_LORA_RANK), lambda b, i: (0, 0)),
                  pl.BlockSpec((1, KV_LORA_RANK), lambda b, i: (0, 0)),
                  pl.BlockSpec(wq.shape, lambda b, i: (0, 0)),
                  pl.BlockSpec(wkv.shape, lambda b, i: (0, 0))],
        out_specs=[pl.BlockSpec((1, nh, tm, MLA_QK_PAD), lambda b, i: (b, 0, i, 0)),
                   pl.BlockSpec((1, nh, tm, MLA_QK_PAD), lambda b, i: (b, 0, i, 0)),
                   pl.BlockSpec((1, nh, tm, V_HEAD_DIM), lambda b, i: (b, 0, i, 0))],
        out_shape=[jax.ShapeDtypeStruct((bsz, nh, s, MLA_QK_PAD), BF16),
                   jax.ShapeDtypeStruct((bsz, nh, s, MLA_QK_PAD), BF16),
                   jax.ShapeDtypeStruct((bsz, nh, s, V_HEAD_DIM), BF16)],
        compiler_params=_params("parallel", "parallel"),
        name="mla_prep",
    )(mla_in, cosr, sinr, gq, gkv, wq, wkv)


def _attn_kernel(q_ref, k_ref, v_ref, o_ref, m_ref, l_ref, acc_ref):
    i = pl.program_id(2)
    j = pl.program_id(3)

    @pl.when(j == 0)
    def _():
        m_ref[...] = jnp.full_like(m_ref, -jnp.inf)
        l_ref[...] = jnp.zeros_like(l_ref)
        acc_ref[...] = jnp.zeros_like(acc_ref)

    @pl.when(j <= i)
    def _():
        s = _dot_nt(q_ref[0, 0], k_ref[0, 0])
        row = lax.broadcasted_iota(jnp.int32, s.shape, 0)
        col = lax.broadcasted_iota(jnp.int32, s.shape, 1)
        s = jnp.where((j < i) | (col <= row), s, -jnp.inf)
        m_prev = m_ref[...]
        m_new = jnp.maximum(m_prev, jnp.max(s, axis=-1, keepdims=True))
        alpha = jnp.exp(m_prev - m_new)
        p = jnp.exp(s - m_new)
        l_ref[...] = alpha * l_ref[...] + jnp.sum(p, axis=-1, keepdims=True)
        acc_ref[...] = alpha * acc_ref[...] + _dot(p.astype(v_ref.dtype), v_ref[0, 0])
        m_ref[...] = m_new

    @pl.when(j == i)
    def _():
        o_ref[0] = (acc_ref[...] / l_ref[...]).astype(o_ref.dtype)


def _attn(q, k, v, tq):
    bsz, nh, s, dq = q.shape
    dv = v.shape[-1]
    nb = s // tq
    return pl.pallas_call(
        _attn_kernel,
        grid=(bsz, nh, nb, nb),
        in_specs=[pl.BlockSpec((1, 1, tq, dq), lambda b, h, i, j: (b, h, i, 0)),
                  pl.BlockSpec((1, 1, tq, dq), lambda b, h, i, j: (b, h, jnp.minimum(i, j), 0)),
                  pl.BlockSpec((1, 1, tq, dv), lambda b, h, i, j: (b, h, jnp.minimum(i, j), 0))],
        out_specs=pl.BlockSpec((1, tq, dv), lambda b, h, i, j: (b, i, h)),
        out_shape=jax.ShapeDtypeStruct((bsz, s, nh * dv), BF16),
        scratch_shapes=[pltpu.VMEM((tq, 1), F32), pltpu.VMEM((tq, 1), F32),
                        pltpu.VMEM((tq, dv), F32)],
        compiler_params=_params("parallel", "parallel", "parallel", "arbitrary"),
        name="attn",
    )(q, k, v)


def _outproj_kernel(og_ref, om_ref, x_ref, mod_ref, wo_ref, g_ref, wpq_ref,
                    x1_ref, h2_ref, qp_ref):
    m = mod_ref[0]
    gw = og_ref.shape[-1]
    mixed = _dot(og_ref[0], wo_ref[0:gw, :]) + _dot(om_ref[0], wo_ref[gw:, :])
    x1 = x_ref[0] + m[2:3] * mixed
    h2 = _rms(x1) * g_ref[...] * (1.0 + m[4:5]) + m[3:4]
    x1_ref[0] = x1
    h2_ref[0] = h2
    qp_ref[0] = _dot(h2.astype(BF16), wpq_ref[...]).astype(qp_ref.dtype)


def _out_proj(o_gdn, o_mla, x, mod, w_out, g, w_pq, tm):
    bsz, s, d = x.shape
    gw = o_gdn.shape[-1]
    mw = o_mla.shape[-1]
    nq = w_pq.shape[1]
    return pl.pallas_call(
        _outproj_kernel,
        grid=(bsz, s // tm),
        in_specs=[pl.BlockSpec((1, tm, gw), lambda b, i: (b, i, 0)),
                  pl.BlockSpec((1, tm, mw), lambda b, i: (b, i, 0)),
                  pl.BlockSpec((1, tm, d), lambda b, i: (b, i, 0)),
                  pl.BlockSpec((1, N_MOD, d), lambda b, i: (b, 0, 0)),
                  pl.BlockSpec((gw + mw, d), lambda b, i: (0, 0)),
                  pl.BlockSpec((1, d), lambda b, i: (0, 0)),
                  pl.BlockSpec((d, nq), lambda b, i: (0, 0))],
        out_specs=[pl.BlockSpec((1, tm, d), lambda b, i: (b, i, 0)),
                   pl.BlockSpec((1, tm, d), lambda b, i: (b, i, 0)),
                   pl.BlockSpec((1, tm, nq), lambda b, i: (b, i, 0))],
        out_shape=[jax.ShapeDtypeStruct((bsz, s, d), F32),
                   jax.ShapeDtypeStruct((bsz, s, d), F32),
                   jax.ShapeDtypeStruct((bsz, s, nq), BF16)],
        compiler_params=_params("parallel", "parallel"),
        name="out_proj",
    )(o_gdn, o_mla, x, mod, w_out, g, w_pq)


def _peer_topk_kernel(qp_ref, keys_ref, idx_ref, gate_ref,
                      stop_ref, itop_ref, cand_ref, cidx_ref, best_ref, idxt_ref, gatet_ref):
    tb = qp_ref.shape[0]
    kk = PEER_TOPK

    def extract(vals, row, n):
        m = jnp.max(vals, axis=0, keepdims=True)
        pos = jnp.min(jnp.where(vals == m, row, n), axis=0, keepdims=True)
        return m, pos, jnp.where(row == pos, -jnp.inf, vals)

    row_k = lax.broadcasted_iota(jnp.int32, (N_KEYS, tb), 0)
    row_c = lax.broadcasted_iota(jnp.int32, (kk * kk, tb), 0)
    for h in range(PEER_HEADS):
        for p in range(2):
            c0 = (2 * h + p) * N_KEYS
            vals = _dot_nt(keys_ref[p, h], qp_ref[:, c0:c0 + N_KEYS])
            for r in range(kk):
                m, pos, vals = extract(vals, row_k, N_KEYS)
                stop_ref[p, r:r + 1, :] = m
                itop_ref[p, r:r + 1, :] = pos
        s2 = stop_ref[1]
        i2 = itop_ref[1]
        for a in range(kk):
            cand_ref[a * kk:(a + 1) * kk, :] = stop_ref[0, a:a + 1, :] + s2
            cidx_ref[a * kk:(a + 1) * kk, :] = itop_ref[0, a:a + 1, :] * N_KEYS + i2
        vals = cand_ref[...]
        cidx = cidx_ref[...]
        for r in range(kk):
            m, pos, vals = extract(vals, row_c, kk * kk)
            best_ref[r:r + 1, :] = m
            idxt_ref[h * kk + r:h * kk + r + 1, :] = jnp.max(
                jnp.where(row_c == pos, cidx, -1), axis=0, keepdims=True)
        best = best_ref[...]
        e = jnp.exp(best - best[0:1, :])
        gatet_ref[h * kk:(h + 1) * kk, :] = e / jnp.sum(e, axis=0, keepdims=True)
    idx_ref[...] = idxt_ref[...].T
    gate_ref[...] = gatet_ref[...].T


def _peer_topk(qp, keys, tb):
    t, nq = qp.shape
    hk = PEER_HEADS * PEER_TOPK
    kk = PEER_TOPK
    return pl.pallas_call(
        _peer_topk_kernel,
        grid=(t // tb,),
        in_specs=[pl.BlockSpec((tb, nq), lambda i: (i, 0)),
                  pl.BlockSpec(keys.shape, lambda i: (0, 0, 0, 0))],
        out_specs=[pl.BlockSpec((tb, hk), lambda i: (i, 0)),
                   pl.BlockSpec((tb, hk), lambda i: (i, 0))],
        out_shape=[jax.ShapeDtypeStruct((t, hk), jnp.int32),
                   jax.ShapeDtypeStruct((t, hk), F32)],
        scratch_shapes=[pltpu.VMEM((2, kk, tb), F32), pltpu.VMEM((2, kk, tb), jnp.int32),
                        pltpu.VMEM((kk * kk, tb), F32), pltpu.VMEM((kk * kk, tb), jnp.int32),
                        pltpu.VMEM((kk, tb), F32),
                        pltpu.VMEM((hk, tb), jnp.int32), pltpu.VMEM((hk, tb), F32)],
        compiler_params=_params("parallel"),
        name="peer_topk",
    )(qp, keys)


PEER_TOKEN_GROUP = 8


def _gelu_tanh(a):
    z = 0.7978845608028654 * (a + 0.044715 * a * a * a)
    t = 1.0 - 2.0 / (1.0 + jnp.exp(2.0 * z))
    return 0.5 * a * (1.0 + t)


def _peer_rows(h, idx, gate, expert_u, expert_v):
    t_total, d = h.shape
    hk = idx.shape[1]
    ln = SC_LANES
    nu = hk // ln
    nch = d // ln
    grp = PEER_TOKEN_GROUP
    tok_per_w = t_total // SC_WORKERS
    n_groups = tok_per_w // grp
    assert tok_per_w * SC_WORKERS == t_total and n_groups * grp == tok_per_w
    assert (grp * nu) % 2 == 0
    idx3 = idx.reshape(t_total, nu, ln)
    mesh = plsc.VectorSubcoreMesh(core_axis_name="c", subcore_axis_name="s",
                                  num_cores=SC_CORES, num_subcores=SC_SUBCORES)

    @functools.partial(
        pl.kernel, mesh=mesh,
        compiler_params=pltpu.CompilerParams(needs_layout_passes=False),
        out_type=jax.ShapeDtypeStruct((t_total, d), F32),
        scratch_types=[
            pltpu.VMEM((grp, nu, ln), jnp.int32),
            pltpu.VMEM((grp, hk), F32),
            pltpu.VMEM((grp, d), F32),
            pltpu.VMEM((grp, d), F32),
            pltpu.VMEM((2, ln, d), F32),
            pltpu.VMEM((2, ln, d), F32),
            pltpu.SemaphoreType.DMA((2,)),
            pltpu.SemaphoreType.DMA((2,)),
        ],
    )
    def rows_kernel(h_hbm, idx_hbm, gate_hbm, u_hbm, v_hbm, y_hbm,
                    idx_v, gate_v, x_v, out_v, ubuf, vbuf, usem, vsem):
        wid = lax.axis_index("s") * SC_CORES + lax.axis_index("c")
        base = wid * tok_per_w
        lane = lax.iota(jnp.int32, ln)

        def u_copy(t, u, slot):
            return pltpu.make_async_copy(u_hbm.at[idx_v[t, u]], ubuf.at[slot], usem.at[slot])

        def v_copy(t, u, slot):
            return pltpu.make_async_copy(v_hbm.at[idx_v[t, u]], vbuf.at[slot], vsem.at[slot])

        def compute_unit(t, u, slot):
            def dot_body(c, accs):
                xv = x_v[t, pl.ds(c * ln, ln)]
                return tuple(accs[r] + xv * ubuf[slot, r, pl.ds(c * ln, ln)] for r in range(ln))
            accs = lax.fori_loop(0, nch, dot_body,
                                 tuple(jnp.zeros((ln,), F32) for _ in range(ln)))
            act = jnp.zeros((ln,), F32)
            for r in range(ln):
                act = jnp.where(lane == r, jnp.sum(accs[r]), act)
            w = gate_v[t, pl.ds(u * ln, ln)] * _gelu_tanh(act)
            ws = [jnp.sum(jnp.where(lane == r, w, 0.0)) for r in range(ln)]
            first = u == 0

            def acc_body(c, carry):
                o = out_v[t, pl.ds(c * ln, ln)]
                o = jnp.where(first, jnp.zeros_like(o), o)
                for r in range(ln):
                    o = o + ws[r] * vbuf[slot, r, pl.ds(c * ln, ln)]
                out_v[t, pl.ds(c * ln, ln)] = o
                return carry
            lax.fori_loop(0, nch, acc_body, 0)

        def group_body(g, carry):
            tok0 = base + g * grp
            pltpu.sync_copy(idx_hbm.at[pl.ds(tok0, grp)], idx_v)
            pltpu.sync_copy(gate_hbm.at[pl.ds(tok0, grp)], gate_v)
            pltpu.sync_copy(h_hbm.at[pl.ds(tok0, grp)], x_v)
            u_copy(0, 0, 0).start()
            v_copy(0, 0, 0).start()

            def unit_pair(i, carry2):
                for slot in range(2):
                    n = i * 2 + slot
                    t = n // nu
                    u = n % nu
                    nn = n + 1

                    @pl.when(nn < grp * nu)
                    def _():
                        u_copy(nn // nu, nn % nu, 1 - slot).start()
                        v_copy(nn // nu, nn % nu, 1 - slot).start()
                    u_copy(t, u, slot).wait()
                    v_copy(t, u, slot).wait()
                    compute_unit(t, u, slot)
                return carry2
            lax.fori_loop(0, grp * nu // 2, unit_pair, 0)
            pltpu.sync_copy(out_v, y_hbm.at[pl.ds(tok0, grp)])
            return carry
        lax.fori_loop(0, n_groups, group_body, 0)

    return rows_kernel(h, idx3, gate, expert_u, expert_v)


def _final_kernel(x1_ref, y_ref, mod_ref, g_ref, o_ref, *, normalize):
    m = mod_ref[0]
    x2 = x1_ref[0] + m[5:6] * y_ref[0]
    o_ref[0] = _rms(x2) * g_ref[...] if normalize else x2


def _final(x1, y, mod, g, tm, normalize):
    bsz, s, d = x1.shape
    blk = pl.BlockSpec((1, tm, d), lambda b, i: (b, i, 0))
    return pl.pallas_call(
        functools.partial(_final_kernel, normalize=normalize),
        grid=(bsz, s // tm),
        in_specs=[blk, blk, pl.BlockSpec((1, N_MOD, d), lambda b, i: (b, 0, 0)),
                  pl.BlockSpec((1, d), lambda b, i: (0, 0))],
        out_specs=blk,
        out_shape=jax.ShapeDtypeStruct((bsz, s, d), F32),
        compiler_params=_params("parallel", "parallel"),
        name="final",
    )(x1, y, mod, g)


def _split_w_in(w):
    gw = GDN_HEADS * GDN_HEAD_DIM
    sizes = [gw] * 4 + [GDN_HEADS] * 2 + [Q_LORA_RANK, KV_LORA_RANK, QK_ROPE_DIM]
    offs = [0]
    for sz in sizes:
        offs.append(offs[-1] + sz)
    parts = [w[:, offs[i]:offs[i + 1]] for i in range(len(sizes))]
    gq, gk, gv, gz, ga, gb, cq, ckv, kr = parts
    d = w.shape[0]
    half = QK_ROPE_DIM // 2
    zeros = lambda n: jnp.zeros((d, n), w.dtype)
    gdn = jnp.concatenate([gq, gk, gv, gz, ga, gb, zeros(LANES - 2 * GDN_HEADS)], axis=1)
    kr_sw = jnp.concatenate([kr[:, half:], kr[:, :half]], axis=1)
    mla = jnp.concatenate([cq, ckv, kr, zeros(LANES - QK_ROPE_DIM),
                           kr_sw, zeros(LANES - QK_ROPE_DIM)], axis=1)
    return jnp.concatenate([gdn, mla], axis=1).astype(BF16), gdn.shape[1], mla.shape[1]


def _split_w_uq(w):
    r = w.shape[0]
    half = QK_ROPE_DIM // 2
    hd = QK_NOPE_DIM + QK_ROPE_DIM
    main, swapped = [], []
    for h in range(MLA_HEADS):
        nope = w[:, h * hd:h * hd + QK_NOPE_DIM]
        rope = w[:, h * hd + QK_NOPE_DIM:(h + 1) * hd]
        main += [nope, rope, jnp.zeros((r, MLA_QK_PAD - hd), w.dtype)]
        swapped += [rope[:, half:], rope[:, :half], jnp.zeros((r, LANES - QK_ROPE_DIM), w.dtype)]
    return jnp.concatenate(main + swapped, axis=1).astype(BF16)


def _rotary_tables(positions):
    half = QK_ROPE_DIM // 2
    inv_freq = ROPE_THETA ** (-jnp.arange(half, dtype=F32) / half)
    ang = positions.astype(F32)[..., None] * inv_freq
    cos, sin = jnp.cos(ang), jnp.sin(ang)
    z = jnp.zeros(cos.shape[:-1] + (LANES - QK_ROPE_DIM,), F32)
    return (jnp.concatenate([cos, cos, z], axis=-1),
            jnp.concatenate([-sin, sin, z], axis=-1))


def _block(n, pref):
    return pref if n % pref == 0 else n


def kernel(x, c, positions, ln_mix_g, w_in, conv_w, a_log, dt_bias, gdn_norm_g, q_norm_g, w_uq, kv_norm_g, w_ukv, w_out, ln_ffn_g, w_pq, sub_keys, expert_u, expert_v, w_ada, b_ada, final_norm_g):
    bsz, s, d = x.shape
    depth = w_in.shape[0]
    cosr, sinr = _rotary_tables(positions)
    tm = _block(s, 256)
    for layer in range(depth):
        mod = _mod(c, w_ada[layer], b_ada[layer]).reshape(bsz, N_MOD, d)
        w_cat, gdn_cols, mla_cols = _split_w_in(w_in[layer])
        gdn_in, mla_in = _in_proj(x, mod, ln_mix_g[layer].reshape(1, d), w_cat, gdn_cols, mla_cols, tm)
        o_gdn = _gdn(gdn_in, conv_w[layer], a_log[layer], dt_bias[layer], gdn_norm_g[layer],
                     _block(s, 256))
        q, k, v = _mla_prep(mla_in, cosr, sinr, q_norm_g[layer].reshape(1, -1),
                            kv_norm_g[layer].reshape(1, -1), _split_w_uq(w_uq[layer]),
                            w_ukv[layer].astype(BF16), _block(s, 512))
        o_mla = _attn(q, k, v, _block(s, 512))
        x1, h2, qp = _out_proj(o_gdn, o_mla, x, mod, w_out[layer].astype(BF16),
                               ln_ffn_g[layer].reshape(1, d), w_pq[layer].astype(BF16), tm)
        idx, gate = _peer_topk(qp.reshape(bsz * s, -1), sub_keys[layer].astype(BF16),
                               _block(bsz * s, 256))
        y = _peer_rows(h2.reshape(bsz * s, d), idx, gate, expert_u[layer], expert_v[layer])
        last = layer + 1 == depth
        x = _final(x1, y.reshape(bsz, s, d), mod, final_norm_g.reshape(1, d), tm, normalize=last)
    return x
```

```python
import functools

import jax
import jax.numpy as jnp
from jax import lax
from jax.experimental import pallas as pl
from jax.experimental.pallas import tpu as pltpu
from jax.experimental.pallas import tpu_sc as plsc

GDN_HEADS = 4
GDN_HEAD_DIM = 128
CONV_WIDTH = 4
CHUNK = 64
MLA_HEADS = 4
QK_NOPE_DIM = 128
QK_ROPE_DIM = 64
V_HEAD_DIM = 128
Q_LORA_RANK = 384
KV_LORA_RANK = 256
ROPE_THETA = 10000.0
PEER_HEADS = 8
N_KEYS = 128
PEER_TOPK = 16
N_MOD = 6
EPS = 1e-6

LANES = 128
SUBLANES = 8
SC_CORES = 2
SC_SUBCORES = 16
SC_LANES = 16
SC_WORKERS = SC_CORES * SC_SUBCORES
VMEM_LIMIT = 48 * 1024 * 1024

F32 = jnp.float32
BF16 = jnp.bfloat16
HI = lax.Precision.HIGHEST


def _dot(a, b, precision=None):
    return jnp.dot(a, b, preferred_element_type=F32, precision=precision)


def _dot_nt(a, b, precision=None):
    return lax.dot_general(a, b, (((1,), (1,)), ((), ())),
                           preferred_element_type=F32, precision=precision)


def _dot_tn(a, b, precision=None):
    return lax.dot_general(a, b, (((0,), (0,)), ((), ())),
                           preferred_element_type=F32, precision=precision)


def _rms(x):
    return x * lax.rsqrt(jnp.mean(x * x, axis=-1, keepdims=True) + EPS)


def _silu(x):
    return x * jax.nn.sigmoid(x)


def _params(*sem):
    return pltpu.CompilerParams(dimension_semantics=sem, vmem_limit_bytes=VMEM_LIMIT)


def _mod_kernel(c_ref, w_ref, b_ref, o_ref):
    o_ref[...] = _dot(_silu(c_ref[...]), w_ref[...], HI) + b_ref[...]


def _mod(c, w_ada, b_ada):
    bsz, d = c.shape
    n = w_ada.shape[1]
    return pl.pallas_call(
        _mod_kernel,
        grid=(n // d,),
        in_specs=[pl.BlockSpec((bsz, d), lambda j: (0, 0)),
                  pl.BlockSpec((d, d), lambda j: (0, j)),
                  pl.BlockSpec((1, d), lambda j: (0, j))],
        out_specs=pl.BlockSpec((bsz, d), lambda j: (0, j)),
        out_shape=jax.ShapeDtypeStruct((bsz, n), F32),
        compiler_params=_params("arbitrary"),
        name="mod",
    )(c, w_ada, b_ada.reshape(1, n))


def _inproj_kernel(x_ref, mod_ref, g_ref, w_ref, gdn_ref, mla_ref):
    m = mod_ref[0]
    h = _rms(x_ref[0]) * g_ref[...] * (1.0 + m[1:2]) + m[0:1]
    p = _dot(h.astype(BF16), w_ref[...])
    gw = gdn_ref.shape[-1]
    gdn_ref[0] = p[:, :gw]
    mla_ref[0] = p[:, gw:]


def _in_proj(x, mod, g, w_cat, gdn_cols, mla_cols, tm):
    bsz, s, d = x.shape
    return pl.pallas_call(
        _inproj_kernel,
        grid=(bsz, s // tm),
        in_specs=[pl.BlockSpec((1, tm, d), lambda b, i: (b, i, 0)),
                  pl.BlockSpec((1, N_MOD, d), lambda b, i: (b, 0, 0)),
                  pl.BlockSpec((1, d), lambda b, i: (0, 0)),
                  pl.BlockSpec((d, gdn_cols + mla_cols), lambda b, i: (0, 0))],
        out_specs=[pl.BlockSpec((1, tm, gdn_cols), lambda b, i: (b, i, 0)),
                   pl.BlockSpec((1, tm, mla_cols), lambda b, i: (b, i, 0))],
        out_shape=[jax.ShapeDtypeStruct((bsz, s, gdn_cols), F32),
                   jax.ShapeDtypeStruct((bsz, s, mla_cols), F32)],
        compiler_params=_params("parallel", "parallel"),
        name="in_proj",
    )(x, mod, g, w_cat)


def _gdn_kernel(q_ref, k_ref, v_ref, z_ref, ab_ref, cwq_ref, cwk_ref, cwv_ref,
                alog_ref, dtb_ref, ng_ref, o_ref, state_ref, tail_ref, buf_ref, vnew_ref):
    hd = pl.program_id(1)
    sb = q_ref.shape[1]
    nchunk = sb // CHUNK

    @pl.when(pl.program_id(2) == 0)
    def _():
        state_ref[...] = jnp.zeros_like(state_ref)
        tail_ref[...] = jnp.zeros_like(tail_ref)

    def conv_silu(x_ref, cw_ref, slot):
        x = x_ref[0]
        buf_ref[0:SUBLANES, :] = tail_ref[slot]
        buf_ref[SUBLANES:SUBLANES + sb, :] = x
        tail_ref[slot] = x[sb - SUBLANES:sb, :]
        cw = cw_ref[0]
        y = cw[CONV_WIDTH - 1:CONV_WIDTH] * x
        for j in range(CONV_WIDTH - 1):
            off = SUBLANES - (CONV_WIDTH - 1) + j
            y = y + cw[j:j + 1] * buf_ref[off:off + sb, :]
        return _silu(y)

    def l2n(x):
        return x * lax.rsqrt(jnp.sum(x * x, axis=-1, keepdims=True) + EPS)

    q = l2n(conv_silu(q_ref, cwq_ref, 0)) * (GDN_HEAD_DIM ** -0.5)
    k = l2n(conv_silu(k_ref, cwk_ref, 1))
    v = conv_silu(v_ref, cwv_ref, 2)

    ab = ab_ref[0]
    lane = lax.broadcasted_iota(jnp.int32, ab.shape, 1)
    a = jnp.sum(jnp.where(lane == hd, ab, 0.0), axis=-1, keepdims=True)
    bg = jnp.sum(jnp.where(lane == hd + GDN_HEADS, ab, 0.0), axis=-1, keepdims=True)
    g = -jnp.exp(alog_ref[0]) * jax.nn.softplus(a + dtb_ref[0])
    beta = jax.nn.sigmoid(bg)

    ri = lax.broadcasted_iota(jnp.int32, (sb, sb), 0)
    ci = lax.broadcasted_iota(jnp.int32, (sb, sb), 1)
    same = (ri // CHUNK) == (ci // CHUNK)
    incl = same & (ci <= ri)
    strict = same & (ci < ri)
    incl_f = incl.astype(F32)
    same_f = same.astype(F32)

    gc = _dot(incl_f, g, HI)
    gcl = _dot(same_f, g, HI)
    g_wide = jnp.broadcast_to(g[:, 0:1], (sb, sb))
    gc_row = _dot_tn(g_wide, (same & (ri <= ci)).astype(F32), HI)
    decay = jnp.where(incl, jnp.exp(jnp.where(incl, gc[:, 0:1] - gc_row, 0.0)), 0.0)

    kb = k * beta
    mmat = jnp.where(strict, _dot_nt(kb, k, HI) * decay, 0.0)
    tinv = (ri == ci).astype(F32)
    bs = 1
    while bs < CHUNK:
        off = ((ri // bs) % 2 == 1) & ((ci // bs) % 2 == 0) & ((ri // (2 * bs)) == (ci // (2 * bs)))
        a_off = jnp.where(off, mmat, 0.0)
        if bs == 1:
            tinv = tinv - a_off
        else:
            tinv = tinv - _dot(_dot(tinv, a_off, HI), tinv, HI)
        bs *= 2

    egc = jnp.exp(gc)
    rhs = jnp.concatenate([v * beta, kb * egc], axis=1)
    sol = _dot(tinv, rhs, HI)
    u = sol[:, :GDN_HEAD_DIM]
    w = sol[:, GDN_HEAD_DIM:]
    attn = _dot_nt(q, k, HI) * decay
    q_dec = q * egc
    k_dec = k * jnp.exp(gcl - gc)
    g_last = jnp.exp(gcl)

    st = state_ref[...]
    o_parts = []
    for n in range(nchunk):
        lo = n * CHUNK
        v_new = u[lo:lo + CHUNK] - _dot(w[lo:lo + CHUNK], st, HI)
        vnew_ref[lo:lo + CHUNK, :] = v_new
        o_parts.append(_dot(q_dec[lo:lo + CHUNK], st, HI))
        st = st * g_last[lo:lo + 1, :] + _dot_tn(k_dec[lo:lo + CHUNK], v_new, HI)
    state_ref[...] = st
    o = jnp.concatenate(o_parts, axis=0) + _dot(attn, vnew_ref[...], HI)

    o = _rms(o) * ng_ref[...] * _silu(z_ref[0])
    o_ref[0] = o.astype(o_ref.dtype)


def _gdn(gdn_in, conv_w, a_log, dt_bias, norm_g, sb):
    bsz, s, _ = gdn_in.shape
    hdim = GDN_HEAD_DIM
    nh = GDN_HEADS
    cw = conv_w.reshape(CONV_WIDTH, 3 * nh, hdim).transpose(1, 0, 2)
    alog = jnp.broadcast_to(a_log.reshape(nh, 1, 1), (nh, 1, hdim)).astype(F32)
    dtb = jnp.broadcast_to(dt_bias.reshape(nh, 1, 1), (nh, 1, hdim)).astype(F32)

    def col(off):
        return pl.BlockSpec((1, sb, hdim), lambda b, h, i: (b, i, h + off))

    def cws(off):
        return pl.BlockSpec((1, CONV_WIDTH, hdim), lambda b, h, i: (h + off, 0, 0))

    hspec = pl.BlockSpec((1, 1, hdim), lambda b, h, i: (h, 0, 0))
    return pl.pallas_call(
        _gdn_kernel,
        grid=(bsz, nh, s // sb),
        in_specs=[col(0), col(nh), col(2 * nh), col(3 * nh),
                  pl.BlockSpec((1, sb, hdim), lambda b, h, i: (b, i, 4 * nh)),
                  cws(0), cws(nh), cws(2 * nh), hspec, hspec,
                  pl.BlockSpec((1, hdim), lambda b, h, i: (0, 0))],
        out_specs=pl.BlockSpec((1, sb, hdim), lambda b, h, i: (b, i, h)),
        out_shape=jax.ShapeDtypeStruct((bsz, s, nh * hdim), BF16),
        scratch_shapes=[pltpu.VMEM((hdim, hdim), F32),
                        pltpu.VMEM((3, SUBLANES, hdim), F32),
                        pltpu.VMEM((sb + SUBLANES, hdim), F32),
                        pltpu.VMEM((sb, hdim), F32)],
        compiler_params=_params("parallel", "parallel", "arbitrary"),
        name="gdn",
    )(gdn_in, gdn_in, gdn_in, gdn_in, gdn_in, cw, cw, cw, alog, dtb, norm_g.reshape(1, hdim))


MLA_QK_PAD = 256


def _mla_prep_kernel(m_ref, cos_ref, sin_ref, gq_ref, gkv_ref, wq_ref, wkv_ref,
                     q_ref, k_ref, v_ref):
    m = m_ref[0]
    cosr = cos_ref[0]
    sinr = sin_ref[0]
    cq = m[:, :Q_LORA_RANK]
    ckv = m[:, Q_LORA_RANK:Q_LORA_RANK + KV_LORA_RANK]
    o = Q_LORA_RANK + KV_LORA_RANK
    kr = m[:, o:o + LANES]
    krs = m[:, o + LANES:o + 2 * LANES]
    scale = (QK_NOPE_DIM + QK_ROPE_DIM) ** -0.5
    qa = _dot((_rms(cq) * gq_ref[...]).astype(BF16), wq_ref[...])
    kva = _dot((_rms(ckv) * gkv_ref[...]).astype(BF16), wkv_ref[...])
    k_rope = (kr * cosr + krs * sinr).astype(k_ref.dtype)
    sw0 = MLA_HEADS * MLA_QK_PAD
    for h in range(MLA_HEADS):
        b0 = h * MLA_QK_PAD
        rope = qa[:, b0 + LANES:b0 + 2 * LANES] * cosr + qa[:, sw0 + h * LANES:sw0 + (h + 1) * LANES] * sinr
        q_ref[0, h, :, 0:LANES] = (qa[:, b0:b0 + LANES] * scale).astype(q_ref.dtype)
        q_ref[0, h, :, LANES:2 * LANES] = (rope * scale).astype(q_ref.dtype)
        c0 = h * (QK_NOPE_DIM + V_HEAD_DIM)
        k_ref[0, h, :, 0:LANES] = kva[:, c0:c0 + QK_NOPE_DIM].astype(k_ref.dtype)
        k_ref[0, h, :, LANES:2 * LANES] = k_rope
        v_ref[0, h] = kva[:, c0 + QK_NOPE_DIM:c0 + QK_NOPE_DIM + V_HEAD_DIM].astype(v_ref.dtype)


def _mla_prep(mla_in, cosr, sinr, gq, gkv, wq, wkv, tm):
    bsz, s, mc = mla_in.shape
    nh = MLA_HEADS
    return pl.pallas_call(
        _mla_prep_kernel,
        grid=(bsz, s // tm),
        in_specs=[pl.BlockSpec((1, tm, mc), lambda b, i: (b, i, 0)),
                  pl.BlockSpec((1, tm, LANES), lambda b, i: (b, i, 0)),
                  pl.BlockSpec((1, tm, LANES), lambda b, i: (b, i, 0)),
                  pl.BlockSpec((1, Q_LORA_RANK), lambda b, i: (0, 0)),
                  pl.BlockSpec((1, KV_LORA_RANK), lambda b, i: (0, 0)),
                  pl.BlockSpec(wq.shape, lambda b, i: (0, 0)),
                  pl.BlockSpec(wkv.shape, lambda b, i: (0, 0))],
        out_specs=[pl.BlockSpec((1, nh, tm, MLA_QK_PAD), lambda b, i: (b, 0, i, 0)),
                   pl.BlockSpec((1, nh, tm, MLA_QK_PAD), lambda b, i: (b, 0, i, 0)),
                   pl.BlockSpec((1, nh, tm, V_HEAD_DIM), lambda b, i: (b, 0, i, 0))],
        out_shape=[jax.ShapeDtypeStruct((bsz, nh, s, MLA_QK_PAD), BF16),
                   jax.ShapeDtypeStruct((bsz, nh, s, MLA_QK_PAD), BF16),
                   jax.ShapeDtypeStruct((bsz, nh, s, V_HEAD_DIM), BF16)],
        compiler_params=_params("parallel", "parallel"),
        name="mla_prep",
    )(mla_in, cosr, sinr, gq, gkv, wq, wkv)


def _attn_kernel(q_ref, k_ref, v_ref, o_ref, m_ref, l_ref, acc_ref):
    i = pl.program_id(2)
    j = pl.program_id(3)

    @pl.when(j == 0)
    def _():
        m_ref[...] = jnp.full_like(m_ref, -jnp.inf)
        l_ref[...] = jnp.zeros_like(l_ref)
        acc_ref[...] = jnp.zeros_like(acc_ref)

    @pl.when(j <= i)
    def _():
        s = _dot_nt(q_ref[0, 0], k_ref[0, 0])
        row = lax.broadcasted_iota(jnp.int32, s.shape, 0)
        col = lax.broadcasted_iota(jnp.int32, s.shape, 1)
        s = jnp.where((j < i) | (col <= row), s, -jnp.inf)
        m_prev = m_ref[...]
        m_new = jnp.maximum(m_prev, jnp.max(s, axis=-1, keepdims=True))
        alpha = jnp.exp(m_prev - m_new)
        p = jnp.exp(s - m_new)
        l_ref[...] = alpha * l_ref[...] + jnp.sum(p, axis=-1, keepdims=True)
        acc_ref[...] = alpha * acc_ref[...] + _dot(p.astype(v_ref.dtype), v_ref[0, 0])
        m_ref[...] = m_new

    @pl.when(j == i)
    def _():
        o_ref[0] = (acc_ref[...] / l_ref[...]).astype(o_ref.dtype)


def _attn(q, k, v, tq):
    bsz, nh, s, dq = q.shape
    dv = v.shape[-1]
    nb = s // tq
    return pl.pallas_call(
        _attn_kernel,
        grid=(bsz, nh, nb, nb),
        in_specs=[pl.BlockSpec((1, 1, tq, dq), lambda b, h, i, j: (b, h, i, 0)),
                  pl.BlockSpec((1, 1, tq, dq), lambda b, h, i, j: (b, h, jnp.minimum(i, j), 0)),
                  pl.BlockSpec((1, 1, tq, dv), lambda b, h, i, j: (b, h, jnp.minimum(i, j), 0))],
        out_specs=pl.BlockSpec((1, tq, dv), lambda b, h, i, j: (b, i, h)),
        out_shape=jax.ShapeDtypeStruct((bsz, s, nh * dv), BF16),
        scratch_shapes=[pltpu.VMEM((tq, 1), F32), pltpu.VMEM((tq, 1), F32),
                        pltpu.VMEM((tq, dv), F32)],
        compiler_params=_params("parallel", "parallel", "parallel", "arbitrary"),
        name="attn",
    )(q, k, v)


def _outproj_kernel(og_ref, om_ref, x_ref, mod_ref, wo_ref, g_ref, wpq_ref,
                    x1_ref, h2_ref, qp_ref):
    m = mod_ref[0]
    gw = og_ref.shape[-1]
    mixed = _dot(og_ref[0], wo_ref[0:gw, :]) + _dot(om_ref[0], wo_ref[gw:, :])
    x1 = x_ref[0] + m[2:3] * mixed
    h2 = _rms(x1) * g_ref[...] * (1.0 + m[4:5]) + m[3:4]
    x1_ref[0] = x1
    h2_ref[0] = h2.astype(h2_ref.dtype)
    qp_ref[0] = _dot(h2.astype(BF16), wpq_ref[...]).astype(qp_ref.dtype)


def _out_proj(o_gdn, o_mla, x, mod, w_out, g, w_pq, tm):
    bsz, s, d = x.shape
    gw = o_gdn.shape[-1]
    mw = o_mla.shape[-1]
    nq = w_pq.shape[1]
    return pl.pallas_call(
        _outproj_kernel,
        grid=(bsz, s // tm),
        in_specs=[pl.BlockSpec((1, tm, gw), lambda b, i: (b, i, 0)),
                  pl.BlockSpec((1, tm, mw), lambda b, i: (b, i, 0)),
                  pl.BlockSpec((1, tm, d), lambda b, i: (b, i, 0)),
                  pl.BlockSpec((1, N_MOD, d), lambda b, i: (b, 0, 0)),
                  pl.BlockSpec((gw + mw, d), lambda b, i: (0, 0)),
                  pl.BlockSpec((1, d), lambda b, i: (0, 0)),
                  pl.BlockSpec((d, nq), lambda b, i: (0, 0))],
        out_specs=[pl.BlockSpec((1, tm, d), lambda b, i: (b, i, 0)),
                   pl.BlockSpec((1, tm, d), lambda b, i: (b, i, 0)),
                   pl.BlockSpec((1, tm, nq), lambda b, i: (b, i, 0))],
        out_shape=[jax.ShapeDtypeStruct((bsz, s, d), F32),
                   jax.ShapeDtypeStruct((bsz, s, d), BF16),
                   jax.ShapeDtypeStruct((bsz, s, nq), BF16)],
        compiler_params=_params("parallel", "parallel"),
        name="out_proj",
    )(o_gdn, o_mla, x, mod, w_out, g, w_pq)


def _peer_topk_kernel(qp_ref, keys_ref, idx_ref, gate_ref,
                      stop_ref, itop_ref, cand_ref, cidx_ref, best_ref, idxt_ref, gatet_ref):
    tb = qp_ref.shape[0]
    kk = PEER_TOPK

    def extract(vals, row, n):
        m = jnp.max(vals, axis=0, keepdims=True)
        pos = jnp.min(jnp.where(vals == m, row, n), axis=0, keepdims=True)
        return m, pos, jnp.where(row == pos, -jnp.inf, vals)

    row_k = lax.broadcasted_iota(jnp.int32, (N_KEYS, tb), 0)
    row_c = lax.broadcasted_iota(jnp.int32, (kk * kk, tb), 0)
    for h in range(PEER_HEADS):
        for p in range(2):
            c0 = (2 * h + p) * N_KEYS
            vals = _dot_nt(keys_ref[p, h], qp_ref[:, c0:c0 + N_KEYS])
            for r in range(kk):
                m, pos, vals = extract(vals, row_k, N_KEYS)
                stop_ref[p, r:r + 1, :] = m
                itop_ref[p, r:r + 1, :] = pos
        s2 = stop_ref[1]
        i2 = itop_ref[1]
        for a in range(kk):
            cand_ref[a * kk:(a + 1) * kk, :] = stop_ref[0, a:a + 1, :] + s2
            cidx_ref[a * kk:(a + 1) * kk, :] = itop_ref[0, a:a + 1, :] * N_KEYS + i2
        vals = cand_ref[...]
        cidx = cidx_ref[...]
        for r in range(kk):
            m, pos, vals = extract(vals, row_c, kk * kk)
            best_ref[r:r + 1, :] = m
            idxt_ref[h * kk + r:h * kk + r + 1, :] = jnp.max(
                jnp.where(row_c == pos, cidx, -1), axis=0, keepdims=True)
        best = best_ref[...]
        e = jnp.exp(best - best[0:1, :])
        gatet_ref[h * kk:(h + 1) * kk, :] = e / jnp.sum(e, axis=0, keepdims=True)
    idx_ref[...] = idxt_ref[...].T
    gate_ref[...] = gatet_ref[...].T


def _peer_topk(qp, keys, tb):
    t, nq = qp.shape
    hk = PEER_HEADS * PEER_TOPK
    kk = PEER_TOPK
    return pl.pallas_call(
        _peer_topk_kernel,
        grid=(t // tb,),
        in_specs=[pl.BlockSpec((tb, nq), lambda i: (i, 0)),
                  pl.BlockSpec(keys.shape, lambda i: (0, 0, 0, 0))],
        out_specs=[pl.BlockSpec((tb, hk), lambda i: (i, 0)),
                   pl.BlockSpec((tb, hk), lambda i: (i, 0))],
        out_shape=[jax.ShapeDtypeStruct((t, hk), jnp.int32),
                   jax.ShapeDtypeStruct((t, hk), F32)],
        scratch_shapes=[pltpu.VMEM((2, kk, tb), F32), pltpu.VMEM((2, kk, tb), jnp.int32),
                        pltpu.VMEM((kk * kk, tb), F32), pltpu.VMEM((kk * kk, tb), jnp.int32),
                        pltpu.VMEM((kk, tb), F32),
                        pltpu.VMEM((hk, tb), jnp.int32), pltpu.VMEM((hk, tb), F32)],
        compiler_params=_params("parallel"),
        name="peer_topk",
    )(qp, keys)


PEER_TOKEN_GROUP = 16


def _peer_gate_matrix(idx, gate, n_experts):
    t_total, hk = idx.shape
    ln = SC_LANES
    nu = hk // ln
    grp = PEER_TOKEN_GROUP
    tok_per_w = t_total // SC_WORKERS
    n_groups = tok_per_w // grp
    assert n_groups * grp * SC_WORKERS == t_total and grp % 2 == 0
    mesh = plsc.VectorSubcoreMesh(core_axis_name="c", subcore_axis_name="s",
                                  num_cores=SC_CORES, num_subcores=SC_SUBCORES)

    @functools.partial(
        pl.kernel, mesh=mesh,
        compiler_params=pltpu.CompilerParams(needs_layout_passes=False),
        out_type=jax.ShapeDtypeStruct((t_total, n_experts), F32),
        scratch_types=[
            pltpu.VMEM((grp * hk,), jnp.int32),
            pltpu.VMEM((grp * hk,), F32),
            pltpu.VMEM((n_experts,), F32),
            pltpu.VMEM((n_experts,), F32),
            pltpu.SemaphoreType.DMA((2,)),
        ],
    )
    def gate_kernel(idx_hbm, gate_hbm, g_hbm, idx_v, gate_v, row0_v, row1_v, sem):
        rows = (row0_v, row1_v)
        wid = lax.axis_index("s") * SC_CORES + lax.axis_index("c")
        base = wid * tok_per_w
        zero = jnp.zeros((ln,), F32)

        def zero_body(c, carry):
            row0_v[pl.ds(c * ln, ln)] = zero
            row1_v[pl.ds(c * ln, ln)] = zero
            return carry
        lax.fori_loop(0, n_experts // ln, zero_body, 0)

        def out_copy(tok, slot):
            return pltpu.make_async_copy(rows[slot], g_hbm.at[tok], sem.at[slot])

        def group_body(g, carry):
            tok0 = base + g * grp
            pltpu.sync_copy(idx_hbm.at[pl.ds(tok0 * hk, grp * hk)], idx_v)
            pltpu.sync_copy(gate_hbm.at[pl.ds(tok0 * hk, grp * hk)], gate_v)

            def pair_body(i, carry2):
                for slot in range(2):
                    t = i * 2 + slot
                    for u in range(nu):
                        sl = pl.ds(t * hk + u * ln, ln)
                        plsc.addupdate_scatter(rows[slot], [idx_v[sl]], gate_v[sl])
                    out_copy(tok0 + t, slot).start()
                for slot in range(2):
                    t = i * 2 + slot
                    out_copy(tok0 + t, slot).wait()
                    for u in range(nu):
                        plsc.store_scatter(rows[slot], [idx_v[pl.ds(t * hk + u * ln, ln)]], zero)
                return carry2
            lax.fori_loop(0, grp // 2, pair_body, 0)
            return carry
        lax.fori_loop(0, n_groups, group_body, 0)

    return gate_kernel(idx.reshape(-1), gate.reshape(-1))


def _peer_dense_kernel(h_ref, g_ref, u_ref, v_ref, o_ref):
    e = pl.program_id(1)
    s = _dot_nt(h_ref[...], u_ref[...])
    p = (jax.nn.gelu(s) * g_ref[...]).astype(v_ref.dtype)
    contrib = _dot(p, v_ref[...])

    @pl.when(e == 0)
    def _():
        o_ref[...] = contrib

    @pl.when(e > 0)
    def _():
        o_ref[...] += contrib


def _peer_dense(h, gmat, u, v, tb, eb):
    t, d = h.shape
    n_e = u.shape[0]
    return pl.pallas_call(
        _peer_dense_kernel,
        grid=(t // tb, n_e // eb),
        in_specs=[pl.BlockSpec((tb, d), lambda i, e: (i, 0)),
                  pl.BlockSpec((tb, eb), lambda i, e: (i, e)),
                  pl.BlockSpec((eb, d), lambda i, e: (e, 0)),
                  pl.BlockSpec((eb, d), lambda i, e: (e, 0))],
        out_specs=pl.BlockSpec((tb, d), lambda i, e: (i, 0)),
        out_shape=jax.ShapeDtypeStruct((t, d), F32),
        compiler_params=_params("parallel", "arbitrary"),
        name="peer_dense",
    )(h, gmat, u, v)


def _final_kernel(x1_ref, y_ref, mod_ref, g_ref, o_ref, *, normalize):
    m = mod_ref[0]
    x2 = x1_ref[0] + m[5:6] * y_ref[0]
    o_ref[0] = _rms(x2) * g_ref[...] if normalize else x2


def _final(x1, y, mod, g, tm, normalize):
    bsz, s, d = x1.shape
    blk = pl.BlockSpec((1, tm, d), lambda b, i: (b, i, 0))
    return pl.pallas_call(
        functools.partial(_final_kernel, normalize=normalize),
        grid=(bsz, s // tm),
        in_specs=[blk, blk, pl.BlockSpec((1, N_MOD, d), lambda b, i: (b, 0, 0)),
                  pl.BlockSpec((1, d), lambda b, i: (0, 0))],
        out_specs=blk,
        out_shape=jax.ShapeDtypeStruct((bsz, s, d), F32),
        compiler_params=_params("parallel", "parallel"),
        name="final",
    )(x1, y, mod, g)


def _split_w_in(w):
    gw = GDN_HEADS * GDN_HEAD_DIM
    sizes = [gw] * 4 + [GDN_HEADS] * 2 + [Q_LORA_RANK, KV_LORA_RANK, QK_ROPE_DIM]
    offs = [0]
    for sz in sizes:
        offs.append(offs[-1] + sz)
    parts = [w[:, offs[i]:offs[i + 1]] for i in range(len(sizes))]
    gq, gk, gv, gz, ga, gb, cq, ckv, kr = parts
    d = w.shape[0]
    half = QK_ROPE_DIM // 2
    zeros = lambda n: jnp.zeros((d, n), w.dtype)
    gdn = jnp.concatenate([gq, gk, gv, gz, ga, gb, zeros(LANES - 2 * GDN_HEADS)], axis=1)
    kr_sw = jnp.concatenate([kr[:, half:], kr[:, :half]], axis=1)
    mla = jnp.concatenate([cq, ckv, kr, zeros(LANES - QK_ROPE_DIM),
                           kr_sw, zeros(LANES - QK_ROPE_DIM)], axis=1)
    return jnp.concatenate([gdn, mla], axis=1).astype(BF16), gdn.shape[1], mla.shape[1]


def _split_w_uq(w):
    r = w.shape[0]
    half = QK_ROPE_DIM // 2
    hd = QK_NOPE_DIM + QK_ROPE_DIM
    main, swapped = [], []
    for h in range(MLA_HEADS):
        nope = w[:, h * hd:h * hd + QK_NOPE_DIM]
        rope = w[:, h * hd + QK_NOPE_DIM:(h + 1) * hd]
        main += [nope, rope, jnp.zeros((r, MLA_QK_PAD - hd), w.dtype)]
        swapped += [rope[:, half:], rope[:, :half], jnp.zeros((r, LANES - QK_ROPE_DIM), w.dtype)]
    return jnp.concatenate(main + swapped, axis=1).astype(BF16)


def _rotary_tables(positions):
    half = QK_ROPE_DIM // 2
    inv_freq = ROPE_THETA ** (-jnp.arange(half, dtype=F32) / half)
    ang = positions.astype(F32)[..., None] * inv_freq
    cos, sin = jnp.cos(ang), jnp.sin(ang)
    z = jnp.zeros(cos.shape[:-1] + (LANES - QK_ROPE_DIM,), F32)
    return (jnp.concatenate([cos, cos, z], axis=-1),
            jnp.concatenate([-sin, sin, z], axis=-1))


def _block(n, pref):
    return pref if n % pref == 0 else n


def kernel(x, c, positions, ln_mix_g, w_in, conv_w, a_log, dt_bias, gdn_norm_g, q_norm_g, w_uq, kv_norm_g, w_ukv, w_out, ln_ffn_g, w_pq, sub_keys, expert_u, expert_v, w_ada, b_ada, final_norm_g):
    bsz, s, d = x.shape
    depth = w_in.shape[0]
    cosr, sinr = _rotary_tables(positions)
    tm = _block(s, 256)
    for layer in range(depth):
        mod = _mod(c, w_ada[layer], b_ada[layer]).reshape(bsz, N_MOD, d)
        w_cat, gdn_cols, mla_cols = _split_w_in(w_in[layer])
        gdn_in, mla_in = _in_proj(x, mod, ln_mix_g[layer].reshape(1, d), w_cat, gdn_cols, mla_cols, tm)
        o_gdn = _gdn(gdn_in, conv_w[layer], a_log[layer], dt_bias[layer], gdn_norm_g[layer],
                     _block(s, 256))
        q, k, v = _mla_prep(mla_in, cosr, sinr, q_norm_g[layer].reshape(1, -1),
                            kv_norm_g[layer].reshape(1, -1), _split_w_uq(w_uq[layer]),
                            w_ukv[layer].astype(BF16), _block(s, 512))
        o_mla = _attn(q, k, v, _block(s, 512))
        x1, h2, qp = _out_proj(o_gdn, o_mla, x, mod, w_out[layer].astype(BF16),
                               ln_ffn_g[layer].reshape(1, d), w_pq[layer].astype(BF16), tm)
        idx, gate = _peer_topk(qp.reshape(bsz * s, -1), sub_keys[layer].astype(BF16),
                               _block(bsz * s, 256))
        n_experts = expert_u.shape[1]
        gmat = _peer_gate_matrix(idx, gate, n_experts)
        y = _peer_dense(h2.reshape(bsz * s, d), gmat, expert_u[layer].astype(BF16),
                        expert_v[layer].astype(BF16), _block(bsz * s, 1024), _block(n_experts, 512))
        last = layer + 1 == depth
        x = _final(x1, y.reshape(bsz, s, d), mod, final_norm_g.reshape(1, d), tm, normalize=last)
    return x
```

```python
import functools

import jax
import jax.numpy as jnp
from jax import lax
from jax.experimental import pallas as pl
from jax.experimental.pallas import tpu as pltpu
from jax.experimental.pallas import tpu_sc as plsc

GDN_HEADS = 4
GDN_HEAD_DIM = 128
CONV_WIDTH = 4
CHUNK = 64
MLA_HEADS = 4
QK_NOPE_DIM = 128
QK_ROPE_DIM = 64
V_HEAD_DIM = 128
Q_LORA_RANK = 384
KV_LORA_RANK = 256
ROPE_THETA = 10000.0
PEER_HEADS = 8
N_KEYS = 128
PEER_TOPK = 16
N_MOD = 6
EPS = 1e-6

LANES = 128
SUBLANES = 8
SC_CORES = 2
SC_SUBCORES = 16
SC_LANES = 16
SC_WORKERS = SC_CORES * SC_SUBCORES
VMEM_LIMIT = 48 * 1024 * 1024

F32 = jnp.float32
BF16 = jnp.bfloat16
HI = lax.Precision.HIGHEST


def _dot(a, b, precision=None):
    return jnp.dot(a, b, preferred_element_type=F32, precision=precision)


def _dot_nt(a, b, precision=None):
    return lax.dot_general(a, b, (((1,), (1,)), ((), ())),
                           preferred_element_type=F32, precision=precision)


def _dot_tn(a, b, precision=None):
    return lax.dot_general(a, b, (((0,), (0,)), ((), ())),
                           preferred_element_type=F32, precision=precision)


def _rms(x):
    return x * lax.rsqrt(jnp.mean(x * x, axis=-1, keepdims=True) + EPS)


def _silu(x):
    return x * jax.nn.sigmoid(x)


def _params(*sem):
    return pltpu.CompilerParams(dimension_semantics=sem, vmem_limit_bytes=VMEM_LIMIT)


def _mod_kernel(c_ref, w_ref, b_ref, o_ref):
    o_ref[...] = _dot(_silu(c_ref[...]), w_ref[...], HI) + b_ref[...]


def _mod(c, w_ada, b_ada):
    bsz, d = c.shape
    n = w_ada.shape[1]
    return pl.pallas_call(
        _mod_kernel,
        grid=(n // d,),
        in_specs=[pl.BlockSpec((bsz, d), lambda j: (0, 0)),
                  pl.BlockSpec((d, d), lambda j: (0, j)),
                  pl.BlockSpec((1, d), lambda j: (0, j))],
        out_specs=pl.BlockSpec((bsz, d), lambda j: (0, j)),
        out_shape=jax.ShapeDtypeStruct((bsz, n), F32),
        compiler_params=_params("arbitrary"),
        name="mod",
    )(c, w_ada, b_ada.reshape(1, n))


def _inproj_kernel(x_ref, mod_ref, g_ref, w_ref, gdn_ref, mla_ref):
    m = mod_ref[0]
    h = _rms(x_ref[0]) * g_ref[...] * (1.0 + m[1:2]) + m[0:1]
    p = _dot(h.astype(BF16), w_ref[...])
    gw = gdn_ref.shape[-1]
    gdn_ref[0] = p[:, :gw]
    mla_ref[0] = p[:, gw:]


def _in_proj(x, mod, g, w_cat, gdn_cols, mla_cols, tm):
    bsz, s, d = x.shape
    return pl.pallas_call(
        _inproj_kernel,
        grid=(bsz, s // tm),
        in_specs=[pl.BlockSpec((1, tm, d), lambda b, i: (b, i, 0)),
                  pl.BlockSpec((1, N_MOD, d), lambda b, i: (b, 0, 0)),
                  pl.BlockSpec((1, d), lambda b, i: (0, 0)),
                  pl.BlockSpec((d, gdn_cols + mla_cols), lambda b, i: (0, 0))],
        out_specs=[pl.BlockSpec((1, tm, gdn_cols), lambda b, i: (b, i, 0)),
                   pl.BlockSpec((1, tm, mla_cols), lambda b, i: (b, i, 0))],
        out_shape=[jax.ShapeDtypeStruct((bsz, s, gdn_cols), F32),
                   jax.ShapeDtypeStruct((bsz, s, mla_cols), F32)],
        compiler_params=_params("parallel", "parallel"),
        name="in_proj",
    )(x, mod, g, w_cat)


def _split_bf16(x):
    hi = x.astype(BF16)
    return hi, (x - hi.astype(F32)).astype(BF16)


def _dot_split(a_parts, b_parts):
    ah, al = a_parts
    bh, bl = b_parts
    return _dot(ah, bh) + (_dot(ah, bl) + _dot(al, bh))


def _gdn_kernel(x_ref, cw_ref, alog_ref, dtb_ref, ng_ref, o_ref,
                state_ref, tail_ref, buf_ref, vnew_ref):
    sb = x_ref.shape[1]
    nchunk = sb // CHUNK
    hdim = GDN_HEAD_DIM
    gw = GDN_HEADS * hdim

    @pl.when(pl.program_id(1) == 0)
    def _():
        state_ref[...] = jnp.zeros_like(state_ref)
        tail_ref[...] = jnp.zeros_like(tail_ref)

    def conv_silu(slot):
        x = x_ref[0, :, slot * hdim:(slot + 1) * hdim]
        buf_ref[slot, 0:SUBLANES, :] = tail_ref[slot]
        buf_ref[slot, SUBLANES:SUBLANES + sb, :] = x
        tail_ref[slot] = x[sb - SUBLANES:sb, :]
        cw = cw_ref[slot]
        y = cw[CONV_WIDTH - 1:CONV_WIDTH] * x
        for j in range(CONV_WIDTH - 1):
            off = SUBLANES - (CONV_WIDTH - 1) + j
            y = y + cw[j:j + 1] * buf_ref[slot, off:off + sb, :]
        return _silu(y)

    def l2n(x):
        return x * lax.rsqrt(jnp.sum(x * x, axis=-1, keepdims=True) + EPS)

    ri = lax.broadcasted_iota(jnp.int32, (sb, sb), 0)
    ci = lax.broadcasted_iota(jnp.int32, (sb, sb), 1)
    same = (ri // CHUNK) == (ci // CHUNK)
    incl = same & (ci <= ri)
    strict = same & (ci < ri)
    eye = (ri == ci).astype(F32)
    levels = []
    bs = 1
    while bs < CHUNK:
        levels.append(((ri // bs) % 2 == 1) & ((ci // bs) % 2 == 0)
                      & ((ri // (2 * bs)) == (ci // (2 * bs))))
        bs *= 2
    rowmod = lax.broadcasted_iota(jnp.int32, (sb, hdim), 0) % CHUNK
    ab = x_ref[0, :, 4 * gw:4 * gw + LANES]

    for h in range(GDN_HEADS):
        q = l2n(conv_silu(h)) * (hdim ** -0.5)
        k = l2n(conv_silu(GDN_HEADS + h))
        v = conv_silu(2 * GDN_HEADS + h)
        g = -jnp.exp(alog_ref[h]) * jax.nn.softplus(ab[:, h:h + 1] + dtb_ref[h])
        beta = jax.nn.sigmoid(ab[:, GDN_HEADS + h:GDN_HEADS + h + 1])

        gc = g
        sh = 1
        while sh < CHUNK:
            gc = gc + jnp.where(rowmod >= sh, pltpu.roll(gc, sh, axis=0), 0.0)
            sh *= 2
        gcl = jnp.concatenate(
            [jnp.broadcast_to(gc[(n + 1) * CHUNK - 1:(n + 1) * CHUNK, :], (CHUNK, hdim))
             for n in range(nchunk)], axis=0)
        gc_row = jnp.broadcast_to(gc.T[0:1, :], (sb, sb))
        decay = jnp.where(incl, jnp.exp(jnp.where(incl, gc[:, 0:1] - gc_row, 0.0)), 0.0)

        kb = k * beta
        k16 = k.astype(BF16)
        mmat = jnp.where(strict, _dot_nt(kb.astype(BF16), k16) * decay, 0.0)
        tinv = eye - jnp.where(levels[0], mmat, 0.0)
        for off in levels[1:]:
            tparts = _split_bf16(tinv)
            a_off = _split_bf16(jnp.where(off, mmat, 0.0))
            tinv = tinv - _dot_split(_split_bf16(_dot_split(tparts, a_off)), tparts)

        egc = jnp.exp(gc)
        rhs = jnp.concatenate([v * beta, kb * egc], axis=1)
        sol = _dot_split(_split_bf16(tinv), _split_bf16(rhs))
        u = sol[:, :hdim]
        w16 = sol[:, hdim:].astype(BF16)
        attn = _dot_nt(q.astype(BF16), k16) * decay
        qd16 = (q * egc).astype(BF16)
        kd16 = (k * jnp.exp(gcl - gc)).astype(BF16)
        g_last = jnp.exp(gcl)

        st = state_ref[h]
        o_parts = []
        for n in range(nchunk):
            lo = n * CHUNK
            st16 = st.astype(BF16)
            v_new = u[lo:lo + CHUNK] - _dot(w16[lo:lo + CHUNK], st16)
            vnew_ref[h, lo:lo + CHUNK, :] = v_new
            o_parts.append(_dot(qd16[lo:lo + CHUNK], st16))
            st = st * g_last[lo:lo + 1, :] + _dot_tn(kd16[lo:lo + CHUNK], v_new.astype(BF16))
        state_ref[h] = st
        o = jnp.concatenate(o_parts, axis=0) + _dot(attn.astype(BF16), vnew_ref[h].astype(BF16))

        z = x_ref[0, :, 3 * gw + h * hdim:3 * gw + (h + 1) * hdim]
        o = _rms(o) * ng_ref[...] * _silu(z)
        o_ref[0, :, h * hdim:(h + 1) * hdim] = o.astype(o_ref.dtype)


def _gdn(gdn_in, conv_w, a_log, dt_bias, norm_g, sb):
    bsz, s, cols = gdn_in.shape
    hdim = GDN_HEAD_DIM
    nh = GDN_HEADS
    cw = conv_w.reshape(CONV_WIDTH, 3 * nh, hdim).transpose(1, 0, 2)
    alog = jnp.broadcast_to(a_log.reshape(nh, 1, 1), (nh, 1, hdim)).astype(F32)
    dtb = jnp.broadcast_to(dt_bias.reshape(nh, 1, 1), (nh, 1, hdim)).astype(F32)
    full = lambda shape: pl.BlockSpec(shape, lambda b, i: (0,) * len(shape))
    return pl.pallas_call(
        _gdn_kernel,
        grid=(bsz, s // sb),
        in_specs=[pl.BlockSpec((1, sb, cols), lambda b, i: (b, i, 0)),
                  full(cw.shape), full(alog.shape), full(dtb.shape), full((1, hdim))],
        out_specs=pl.BlockSpec((1, sb, nh * hdim), lambda b, i: (b, i, 0)),
        out_shape=jax.ShapeDtypeStruct((bsz, s, nh * hdim), BF16),
        scratch_shapes=[pltpu.VMEM((nh, hdim, hdim), F32),
                        pltpu.VMEM((3 * nh, SUBLANES, hdim), F32),
                        pltpu.VMEM((3 * nh, sb + SUBLANES, hdim), F32),
                        pltpu.VMEM((nh, sb, hdim), F32)],
        compiler_params=_params("parallel", "arbitrary"),
        name="gdn",
    )(gdn_in, cw, alog, dtb, norm_g.reshape(1, hdim))


MLA_QK_PAD = 256


def _mla_prep_kernel(m_ref, cos_ref, sin_ref, gq_ref, gkv_ref, wq_ref, wkv_ref,
                     q_ref, k_ref, v_ref):
    m = m_ref[0]
    cosr = cos_ref[0]
    sinr = sin_ref[0]
    cq = m[:, :Q_LORA_RANK]
    ckv = m[:, Q_LORA_RANK:Q_LORA_RANK + KV_LORA_RANK]
    o = Q_LORA_RANK + KV_LORA_RANK
    kr = m[:, o:o + LANES]
    krs = m[:, o + LANES:o + 2 * LANES]
    scale = (QK_NOPE_DIM + QK_ROPE_DIM) ** -0.5
    qa = _dot((_rms(cq) * gq_ref[...]).astype(BF16), wq_ref[...])
    kva = _dot((_rms(ckv) * gkv_ref[...]).astype(BF16), wkv_ref[...])
    k_rope = (kr * cosr + krs * sinr).astype(k_ref.dtype)
    sw0 = MLA_HEADS * MLA_QK_PAD
    for h in range(MLA_HEADS):
        b0 = h * MLA_QK_PAD
        rope = qa[:, b0 + LANES:b0 + 2 * LANES] * cosr + qa[:, sw0 + h * LANES:sw0 + (h + 1) * LANES] * sinr
        q_ref[0, h, :, 0:LANES] = (qa[:, b0:b0 + LANES] * scale).astype(q_ref.dtype)
        q_ref[0, h, :, LANES:2 * LANES] = (rope * scale).astype(q_ref.dtype)
        c0 = h * (QK_NOPE_DIM + V_HEAD_DIM)
        k_ref[0, h, :, 0:LANES] = kva[:, c0:c0 + QK_NOPE_DIM].astype(k_ref.dtype)
        k_ref[0, h, :, LANES:2 * LANES] = k_rope
        v_ref[0, h] = kva[:, c0 + QK_NOPE_DIM:c0 + QK_NOPE_DIM + V_HEAD_DIM].astype(v_ref.dtype)


def _mla_prep(mla_in, cosr, sinr, gq, gkv, wq, wkv, tm):
    bsz, s, mc = mla_in.shape
    nh = MLA_HEADS
    return pl.pallas_call(
        _mla_prep_kernel,
        grid=(bsz, s // tm),
        in_specs=[pl.BlockSpec((1, tm, mc), lambda b, i: (b, i, 0)),
                  pl.BlockSpec((1, tm, LANES), lambda b, i: (b, i, 0)),
                  pl.BlockSpec((1, tm, LANES), lambda b, i: (b, i, 0)),
                  pl.BlockSpec((1, Q_LORA_RANK), lambda b, i: (0, 0)),
                  pl.BlockSpec((1, KV_LORA_RANK), lambda b, i: (0, 0)),
                  pl.BlockSpec(wq.shape, lambda b, i: (0, 0)),
                  pl.BlockSpec(wkv.shape, lambda b, i: (0, 0))],
        out_specs=[pl.BlockSpec((1, nh, tm, MLA_QK_PAD), lambda b, i: (b, 0, i, 0)),
                   pl.BlockSpec((1, nh, tm, MLA_QK_PAD), lambda b, i: (b, 0, i, 0)),
                   pl.BlockSpec((1, nh, tm, V_HEAD_DIM), lambda b, i: (b, 0, i, 0))],
        out_shape=[jax.ShapeDtypeStruct((bsz, nh, s, MLA_QK_PAD), BF16),
                   jax.ShapeDtypeStruct((bsz, nh, s, MLA_QK_PAD), BF16),
                   jax.ShapeDtypeStruct((bsz, nh, s, V_HEAD_DIM), BF16)],
        compiler_params=_params("parallel", "parallel"),
        name="mla_prep",
    )(mla_in, cosr, sinr, gq, gkv, wq, wkv)


def _fold_lanes(x, op):
    parts = [x[:, c * LANES:(c + 1) * LANES] for c in range(x.shape[1] // LANES)]
    while len(parts) > 1:
        parts = [op(parts[i], parts[i + 1]) for i in range(0, len(parts), 2)]
    return parts[0]


def _attn_kernel(q_ref, k_ref, v_ref, o_ref, m_ref, l_ref, acc_ref):
    i = pl.program_id(2)
    tq = q_ref.shape[2]
    q = q_ref[0, 0]
    m_ref[...] = jnp.full_like(m_ref, -jnp.inf)
    l_ref[...] = jnp.zeros_like(l_ref)
    acc_ref[...] = jnp.zeros_like(acc_ref)

    def step(j, diagonal):
        start = pl.multiple_of(j * tq, tq)
        s = _dot_nt(q, k_ref[0, 0, pl.ds(start, tq), :])
        if diagonal:
            row = lax.broadcasted_iota(jnp.int32, s.shape, 0)
            col = lax.broadcasted_iota(jnp.int32, s.shape, 1)
            s = jnp.where(col <= row, s, -jnp.inf)
        m_prev = m_ref[...]
        m_new = jnp.maximum(m_prev, jnp.max(_fold_lanes(s, jnp.maximum), axis=-1, keepdims=True))
        alpha = jnp.exp(m_prev - m_new)
        p = jnp.exp(s - m_new)
        l_ref[...] = alpha * l_ref[...] + jnp.sum(_fold_lanes(p, jnp.add), axis=-1, keepdims=True)
        acc_ref[...] = alpha * acc_ref[...] + _dot(p.astype(v_ref.dtype),
                                                   v_ref[0, 0, pl.ds(start, tq), :])
        m_ref[...] = m_new

    def body(j, carry):
        step(j, False)
        return carry
    lax.fori_loop(0, i, body, 0)
    step(i, True)
    o_ref[0] = (acc_ref[...] / l_ref[...]).astype(o_ref.dtype)


def _attn(q, k, v, tq):
    bsz, nh, s, dq = q.shape
    dv = v.shape[-1]
    return pl.pallas_call(
        _attn_kernel,
        grid=(bsz, nh, s // tq),
        in_specs=[pl.BlockSpec((1, 1, tq, dq), lambda b, h, i: (b, h, i, 0)),
                  pl.BlockSpec((1, 1, s, dq), lambda b, h, i: (b, h, 0, 0)),
                  pl.BlockSpec((1, 1, s, dv), lambda b, h, i: (b, h, 0, 0))],
        out_specs=pl.BlockSpec((1, tq, dv), lambda b, h, i: (b, i, h)),
        out_shape=jax.ShapeDtypeStruct((bsz, s, nh * dv), BF16),
        scratch_shapes=[pltpu.VMEM((tq, 1), F32), pltpu.VMEM((tq, 1), F32),
                        pltpu.VMEM((tq, dv), F32)],
        compiler_params=_params("parallel", "parallel", "arbitrary"),
        name="attn",
    )(q, k, v)


def _outproj_kernel(og_ref, om_ref, x_ref, mod_ref, wo_ref, g_ref, wpq_ref,
                    x1_ref, h2_ref, qp_ref):
    m = mod_ref[0]
    gw = og_ref.shape[-1]
    mixed = _dot(og_ref[0], wo_ref[0:gw, :]) + _dot(om_ref[0], wo_ref[gw:, :])
    x1 = x_ref[0] + m[2:3] * mixed
    h2 = _rms(x1) * g_ref[...] * (1.0 + m[4:5]) + m[3:4]
    x1_ref[0] = x1
    h2_ref[0] = h2.astype(h2_ref.dtype)
    qp_ref[0] = _dot(h2.astype(BF16), wpq_ref[...]).astype(qp_ref.dtype)


def _out_proj(o_gdn, o_mla, x, mod, w_out, g, w_pq, tm):
    bsz, s, d = x.shape
    gw = o_gdn.shape[-1]
    mw = o_mla.shape[-1]
    nq = w_pq.shape[1]
    return pl.pallas_call(
        _outproj_kernel,
        grid=(bsz, s // tm),
        in_specs=[pl.BlockSpec((1, tm, gw), lambda b, i: (b, i, 0)),
                  pl.BlockSpec((1, tm, mw), lambda b, i: (b, i, 0)),
                  pl.BlockSpec((1, tm, d), lambda b, i: (b, i, 0)),
                  pl.BlockSpec((1, N_MOD, d), lambda b, i: (b, 0, 0)),
                  pl.BlockSpec((gw + mw, d), lambda b, i: (0, 0)),
                  pl.BlockSpec((1, d), lambda b, i: (0, 0)),
                  pl.BlockSpec((d, nq), lambda b, i: (0, 0))],
        out_specs=[pl.BlockSpec((1, tm, d), lambda b, i: (b, i, 0)),
                   pl.BlockSpec((1, tm, d), lambda b, i: (b, i, 0)),
                   pl.BlockSpec((1, tm, nq), lambda b, i: (b, i, 0))],
        out_shape=[jax.ShapeDtypeStruct((bsz, s, d), F32),
                   jax.ShapeDtypeStruct((bsz, s, d), BF16),
                   jax.ShapeDtypeStruct((bsz, s, nq), BF16)],
        compiler_params=_params("parallel", "parallel"),
        name="out_proj",
    )(o_gdn, o_mla, x, mod, w_out, g, w_pq)


PEER_CAND_ROWS = 2 * SUBLANES + (SUBLANES - 1) * SUBLANES + SUBLANES


def _peer_topk_kernel(qp_ref, keys_ref, idx_ref, gate_ref,
                      stop_ref, itop_ref, cand_ref, cidx_ref, best_ref, idxt_ref, gatet_ref):
    tb = qp_ref.shape[0]
    kk = PEER_TOPK
    assert kk == 2 * SUBLANES

    def extract(vals, row, n):
        m = jnp.max(vals, axis=0, keepdims=True)
        pos = jnp.min(jnp.where(vals == m, row, n), axis=0, keepdims=True)
        return m, pos, jnp.where(row == pos, -jnp.inf, vals)

    row_k = lax.broadcasted_iota(jnp.int32, (N_KEYS, tb), 0)
    row_c = lax.broadcasted_iota(jnp.int32, (PEER_CAND_ROWS, tb), 0)
    row_8 = lax.broadcasted_iota(jnp.int32, (SUBLANES, tb), 0)
    for h in range(PEER_HEADS):
        for p in range(2):
            c0 = (2 * h + p) * N_KEYS
            vals = _dot_nt(keys_ref[p, h], qp_ref[:, c0:c0 + N_KEYS])
            for r in range(kk):
                m, pos, vals = extract(vals, row_k, N_KEYS)
                stop_ref[p, r:r + 1, :] = m
                itop_ref[p, r:r + 1, :] = pos
        cand_ref[0:kk, :] = stop_ref[0, 0:1, :] + stop_ref[1]
        cidx_ref[0:kk, :] = itop_ref[0, 0:1, :] * N_KEYS + itop_ref[1]
        s2 = stop_ref[1, 0:SUBLANES, :]
        i2 = itop_ref[1, 0:SUBLANES, :]
        for a in range(1, SUBLANES):
            r0 = kk + (a - 1) * SUBLANES
            cand_ref[r0:r0 + SUBLANES, :] = jnp.where(
                row_8 < kk // (a + 1), stop_ref[0, a:a + 1, :] + s2, -jnp.inf)
            cidx_ref[r0:r0 + SUBLANES, :] = itop_ref[0, a:a + 1, :] * N_KEYS + i2
        r0 = kk + (SUBLANES - 1) * SUBLANES
        cand_ref[r0:r0 + SUBLANES, :] = stop_ref[0, SUBLANES:kk, :] + stop_ref[1, 0:1, :]
        cidx_ref[r0:r0 + SUBLANES, :] = itop_ref[0, SUBLANES:kk, :] * N_KEYS + itop_ref[1, 0:1, :]
        vals = cand_ref[...]
        cidx = cidx_ref[...]
        for r in range(kk):
            m, pos, vals = extract(vals, row_c, PEER_CAND_ROWS)
            best_ref[r:r + 1, :] = m
            idxt_ref[h * kk + r:h * kk + r + 1, :] = jnp.max(
                jnp.where(row_c == pos, cidx, -1), axis=0, keepdims=True)
        best = best_ref[...]
        e = jnp.exp(best - best[0:1, :])
        gatet_ref[h * kk:(h + 1) * kk, :] = e / jnp.sum(e, axis=0, keepdims=True)
    idx_ref[...] = idxt_ref[...].T
    gate_ref[...] = gatet_ref[...].T


def _peer_topk(qp, keys, tb):
    t, nq = qp.shape
    hk = PEER_HEADS * PEER_TOPK
    kk = PEER_TOPK
    return pl.pallas_call(
        _peer_topk_kernel,
        grid=(t // tb,),
        in_specs=[pl.BlockSpec((tb, nq), lambda i: (i, 0)),
                  pl.BlockSpec(keys.shape, lambda i: (0, 0, 0, 0))],
        out_specs=[pl.BlockSpec((tb, hk), lambda i: (i, 0)),
                   pl.BlockSpec((tb, hk), lambda i: (i, 0))],
        out_shape=[jax.ShapeDtypeStruct((t, hk), jnp.int32),
                   jax.ShapeDtypeStruct((t, hk), F32)],
        scratch_shapes=[pltpu.VMEM((2, kk, tb), F32), pltpu.VMEM((2, kk, tb), jnp.int32),
                        pltpu.VMEM((PEER_CAND_ROWS, tb), F32),
                        pltpu.VMEM((PEER_CAND_ROWS, tb), jnp.int32),
                        pltpu.VMEM((kk, tb), F32),
                        pltpu.VMEM((hk, tb), jnp.int32), pltpu.VMEM((hk, tb), F32)],
        compiler_params=_params("parallel"),
        name="peer_topk",
    )(qp, keys)


PEER_TOKEN_GROUP = 16


def _peer_gate_matrix(idx, gate, n_experts):
    t_total, hk = idx.shape
    ln = SC_LANES
    nu = hk // ln
    grp = PEER_TOKEN_GROUP
    tok_per_w = t_total // SC_WORKERS
    n_groups = tok_per_w // grp
    assert n_groups * grp * SC_WORKERS == t_total and grp % 2 == 0
    mesh = plsc.VectorSubcoreMesh(core_axis_name="c", subcore_axis_name="s",
                                  num_cores=SC_CORES, num_subcores=SC_SUBCORES)

    @functools.partial(
        pl.kernel, mesh=mesh,
        compiler_params=pltpu.CompilerParams(needs_layout_passes=False),
        out_type=jax.ShapeDtypeStruct((t_total, n_experts), F32),
        scratch_types=[
            pltpu.VMEM((grp * hk,), jnp.int32),
            pltpu.VMEM((grp * hk,), F32),
            pltpu.VMEM((n_experts,), F32),
            pltpu.VMEM((n_experts,), F32),
            pltpu.SemaphoreType.DMA((2,)),
        ],
    )
    def gate_kernel(idx_hbm, gate_hbm, g_hbm, idx_v, gate_v, row0_v, row1_v, sem):
        rows = (row0_v, row1_v)
        wid = lax.axis_index("s") * SC_CORES + lax.axis_index("c")
        base = wid * tok_per_w
        zero = jnp.zeros((ln,), F32)

        def zero_body(c, carry):
            row0_v[pl.ds(c * ln, ln)] = zero
            row1_v[pl.ds(c * ln, ln)] = zero
            return carry
        lax.fori_loop(0, n_experts // ln, zero_body, 0)

        def out_copy(tok, slot):
            return pltpu.make_async_copy(rows[slot], g_hbm.at[tok], sem.at[slot])

        def group_body(g, carry):
            tok0 = base + g * grp
            pltpu.sync_copy(idx_hbm.at[pl.ds(tok0 * hk, grp * hk)], idx_v)
            pltpu.sync_copy(gate_hbm.at[pl.ds(tok0 * hk, grp * hk)], gate_v)

            def pair_body(i, carry2):
                for slot in range(2):
                    t = i * 2 + slot
                    for u in range(nu):
                        sl = pl.ds(t * hk + u * ln, ln)
                        plsc.addupdate_scatter(rows[slot], [idx_v[sl]], gate_v[sl])
                    out_copy(tok0 + t, slot).start()
                for slot in range(2):
                    t = i * 2 + slot
                    out_copy(tok0 + t, slot).wait()
                    for u in range(nu):
                        plsc.store_scatter(rows[slot], [idx_v[pl.ds(t * hk + u * ln, ln)]], zero)
                return carry2
            lax.fori_loop(0, grp // 2, pair_body, 0)
            return carry
        lax.fori_loop(0, n_groups, group_body, 0)

    return gate_kernel(idx.reshape(-1), gate.reshape(-1))


def _peer_dense_kernel(h_ref, g_ref, u_ref, v_ref, o_ref):
    e = pl.program_id(1)
    s = _dot_nt(h_ref[...], u_ref[...])
    p = (jax.nn.gelu(s) * g_ref[...]).astype(v_ref.dtype)
    contrib = _dot(p, v_ref[...])

    @pl.when(e == 0)
    def _():
        o_ref[...] = contrib

    @pl.when(e > 0)
    def _():
        o_ref[...] += contrib


def _peer_dense(h, gmat, u, v, tb, eb):
    t, d = h.shape
    n_e = u.shape[0]
    return pl.pallas_call(
        _peer_dense_kernel,
        grid=(t // tb, n_e // eb),
        in_specs=[pl.BlockSpec((tb, d), lambda i, e: (i, 0)),
                  pl.BlockSpec((tb, eb), lambda i, e: (i, e)),
                  pl.BlockSpec((eb, d), lambda i, e: (e, 0)),
                  pl.BlockSpec((eb, d), lambda i, e: (e, 0))],
        out_specs=pl.BlockSpec((tb, d), lambda i, e: (i, 0)),
        out_shape=jax.ShapeDtypeStruct((t, d), F32),
        compiler_params=_params("parallel", "arbitrary"),
        name="peer_dense",
    )(h, gmat, u, v)


def _final_kernel(x1_ref, y_ref, mod_ref, g_ref, o_ref, *, normalize):
    m = mod_ref[0]
    x2 = x1_ref[0] + m[5:6] * y_ref[0]
    o_ref[0] = _rms(x2) * g_ref[...] if normalize else x2


def _final(x1, y, mod, g, tm, normalize):
    bsz, s, d = x1.shape
    blk = pl.BlockSpec((1, tm, d), lambda b, i: (b, i, 0))
    return pl.pallas_call(
        functools.partial(_final_kernel, normalize=normalize),
        grid=(bsz, s // tm),
        in_specs=[blk, blk, pl.BlockSpec((1, N_MOD, d), lambda b, i: (b, 0, 0)),
                  pl.BlockSpec((1, d), lambda b, i: (0, 0))],
        out_specs=blk,
        out_shape=jax.ShapeDtypeStruct((bsz, s, d), F32),
        compiler_params=_params("parallel", "parallel"),
        name="final",
    )(x1, y, mod, g)


def _split_w_in(w):
    gw = GDN_HEADS * GDN_HEAD_DIM
    sizes = [gw] * 4 + [GDN_HEADS] * 2 + [Q_LORA_RANK, KV_LORA_RANK, QK_ROPE_DIM]
    offs = [0]
    for sz in sizes:
        offs.append(offs[-1] + sz)
    parts = [w[:, offs[i]:offs[i + 1]] for i in range(len(sizes))]
    gq, gk, gv, gz, ga, gb, cq, ckv, kr = parts
    d = w.shape[0]
    half = QK_ROPE_DIM // 2
    zeros = lambda n: jnp.zeros((d, n), w.dtype)
    gdn = jnp.concatenate([gq, gk, gv, gz, ga, gb, zeros(LANES - 2 * GDN_HEADS)], axis=1)
    kr_sw = jnp.concatenate([kr[:, half:], kr[:, :half]], axis=1)
    mla = jnp.concatenate([cq, ckv, kr, zeros(LANES - QK_ROPE_DIM),
                           kr_sw, zeros(LANES - QK_ROPE_DIM)], axis=1)
    return jnp.concatenate([gdn, mla], axis=1).astype(BF16), gdn.shape[1], mla.shape[1]


def _split_w_uq(w):
    r = w.shape[0]
    half = QK_ROPE_DIM // 2
    hd = QK_NOPE_DIM + QK_ROPE_DIM
    main, swapped = [], []
    for h in range(MLA_HEADS):
        nope = w[:, h * hd:h * hd + QK_NOPE_DIM]
        rope = w[:, h * hd + QK_NOPE_DIM:(h + 1) * hd]
        main += [nope, rope, jnp.zeros((r, MLA_QK_PAD - hd), w.dtype)]
        swapped += [rope[:, half:], rope[:, :half], jnp.zeros((r, LANES - QK_ROPE_DIM), w.dtype)]
    return jnp.concatenate(main + swapped, axis=1).astype(BF16)


def _rotary_tables(positions):
    half = QK_ROPE_DIM // 2
    inv_freq = ROPE_THETA ** (-jnp.arange(half, dtype=F32) / half)
    ang = positions.astype(F32)[..., None] * inv_freq
    cos, sin = jnp.cos(ang), jnp.sin(ang)
    z = jnp.zeros(cos.shape[:-1] + (LANES - QK_ROPE_DIM,), F32)
    return (jnp.concatenate([cos, cos, z], axis=-1),
            jnp.concatenate([-sin, sin, z], axis=-1))


def _block(n, pref):
    return pref if n % pref == 0 else n


def kernel(x, c, positions, ln_mix_g, w_in, conv_w, a_log, dt_bias, gdn_norm_g, q_norm_g, w_uq, kv_norm_g, w_ukv, w_out, ln_ffn_g, w_pq, sub_keys, expert_u, expert_v, w_ada, b_ada, final_norm_g):
    bsz, s, d = x.shape
    depth = w_in.shape[0]
    cosr, sinr = _rotary_tables(positions)
    tm = _block(s, 256)
    for layer in range(depth):
        mod = _mod(c, w_ada[layer], b_ada[layer]).reshape(bsz, N_MOD, d)
        w_cat, gdn_cols, mla_cols = _split_w_in(w_in[layer])
        gdn_in, mla_in = _in_proj(x, mod, ln_mix_g[layer].reshape(1, d), w_cat, gdn_cols, mla_cols, tm)
        o_gdn = _gdn(gdn_in, conv_w[layer], a_log[layer], dt_bias[layer], gdn_norm_g[layer],
                     _block(s, 256))
        q, k, v = _mla_prep(mla_in, cosr, sinr, q_norm_g[layer].reshape(1, -1),
                            kv_norm_g[layer].reshape(1, -1), _split_w_uq(w_uq[layer]),
                            w_ukv[layer].astype(BF16), _block(s, 512))
        o_mla = _attn(q, k, v, _block(s, 1024))
        x1, h2, qp = _out_proj(o_gdn, o_mla, x, mod, w_out[layer].astype(BF16),
                               ln_ffn_g[layer].reshape(1, d), w_pq[layer].astype(BF16), tm)
        idx, gate = _peer_topk(qp.reshape(bsz * s, -1), sub_keys[layer].astype(BF16),
                               _block(bsz * s, 256))
        n_experts = expert_u.shape[1]
        gmat = _peer_gate_matrix(idx, gate, n_experts)
        y = _peer_dense(h2.reshape(bsz * s, d), gmat, expert_u[layer].astype(BF16),
                        expert_v[layer].astype(BF16), _block(bsz * s, 1024), _block(n_experts, 1024))
        last = layer + 1 == depth
        x = _final(x1, y.reshape(bsz, s, d), mod, final_norm_g.reshape(1, d), tm, normalize=last)
    return x
```

```python
import functools

import jax
import jax.numpy as jnp
from jax import lax
from jax.experimental import pallas as pl
from jax.experimental.pallas import tpu as pltpu
from jax.experimental.pallas import tpu_sc as plsc

GDN_HEADS = 4
GDN_HEAD_DIM = 128
CONV_WIDTH = 4
CHUNK = 64
MLA_HEADS = 4
QK_NOPE_DIM = 128
QK_ROPE_DIM = 64
V_HEAD_DIM = 128
Q_LORA_RANK = 384
KV_LORA_RANK = 256
ROPE_THETA = 10000.0
PEER_HEADS = 8
N_KEYS = 128
PEER_TOPK = 16
N_MOD = 6
EPS = 1e-6

LANES = 128
SUBLANES = 8
SC_CORES = 2
SC_SUBCORES = 16
SC_LANES = 16
SC_WORKERS = SC_CORES * SC_SUBCORES
VMEM_LIMIT = 48 * 1024 * 1024

F32 = jnp.float32
BF16 = jnp.bfloat16
HI = lax.Precision.HIGHEST


def _dot(a, b, precision=None):
    return jnp.dot(a, b, preferred_element_type=F32, precision=precision)


def _dot_nt(a, b, precision=None):
    return lax.dot_general(a, b, (((1,), (1,)), ((), ())),
                           preferred_element_type=F32, precision=precision)


def _dot_tn(a, b, precision=None):
    return lax.dot_general(a, b, (((0,), (0,)), ((), ())),
                           preferred_element_type=F32, precision=precision)


def _rms(x):
    return x * lax.rsqrt(jnp.mean(x * x, axis=-1, keepdims=True) + EPS)


def _silu(x):
    return x * jax.nn.sigmoid(x)


def _params(*sem):
    return pltpu.CompilerParams(dimension_semantics=sem, vmem_limit_bytes=VMEM_LIMIT)


def _mod_kernel(c_ref, w_ref, b_ref, o_ref):
    o_ref[...] = _dot(_silu(c_ref[...]), w_ref[...], HI) + b_ref[...]


def _mod(c, w_ada, b_ada):
    bsz, d = c.shape
    n = w_ada.shape[1]
    return pl.pallas_call(
        _mod_kernel,
        grid=(n // d,),
        in_specs=[pl.BlockSpec((bsz, d), lambda j: (0, 0)),
                  pl.BlockSpec((d, d), lambda j: (0, j)),
                  pl.BlockSpec((1, d), lambda j: (0, j))],
        out_specs=pl.BlockSpec((bsz, d), lambda j: (0, j)),
        out_shape=jax.ShapeDtypeStruct((bsz, n), F32),
        compiler_params=_params("arbitrary"),
        name="mod",
    )(c, w_ada, b_ada.reshape(1, n))


def _inproj_kernel(x_ref, mod_ref, g_ref, w_ref, gdn_ref, mla_ref):
    m = mod_ref[0]
    h = _rms(x_ref[0]) * g_ref[...] * (1.0 + m[1:2]) + m[0:1]
    p = _dot(h.astype(BF16), w_ref[...])
    gw = gdn_ref.shape[-1]
    gdn_ref[0] = p[:, :gw]
    mla_ref[0] = p[:, gw:]


def _in_proj(x, mod, g, w_cat, gdn_cols, mla_cols, tm):
    bsz, s, d = x.shape
    return pl.pallas_call(
        _inproj_kernel,
        grid=(bsz, s // tm),
        in_specs=[pl.BlockSpec((1, tm, d), lambda b, i: (b, i, 0)),
                  pl.BlockSpec((1, N_MOD, d), lambda b, i: (b, 0, 0)),
                  pl.BlockSpec((1, d), lambda b, i: (0, 0)),
                  pl.BlockSpec((d, gdn_cols + mla_cols), lambda b, i: (0, 0))],
        out_specs=[pl.BlockSpec((1, tm, gdn_cols), lambda b, i: (b, i, 0)),
                   pl.BlockSpec((1, tm, mla_cols), lambda b, i: (b, i, 0))],
        out_shape=[jax.ShapeDtypeStruct((bsz, s, gdn_cols), F32),
                   jax.ShapeDtypeStruct((bsz, s, mla_cols), F32)],
        compiler_params=_params("parallel", "parallel"),
        name="in_proj",
    )(x, mod, g, w_cat)


def _gdn_kernel(x_ref, cw_ref, alog_ref, dtb_ref, ng_ref, o_ref,
                state_ref, tail_ref, buf_ref, vnew_ref):
    sb = x_ref.shape[1]
    nchunk = sb // CHUNK
    hdim = GDN_HEAD_DIM
    gw = GDN_HEADS * hdim

    @pl.when(pl.program_id(1) == 0)
    def _():
        state_ref[...] = jnp.zeros_like(state_ref)
        tail_ref[...] = jnp.zeros_like(tail_ref)

    def conv_silu(slot):
        x = x_ref[0, :, slot * hdim:(slot + 1) * hdim]
        buf_ref[slot, 0:SUBLANES, :] = tail_ref[slot]
        buf_ref[slot, SUBLANES:SUBLANES + sb, :] = x
        tail_ref[slot] = x[sb - SUBLANES:sb, :]
        cw = cw_ref[slot]
        y = cw[CONV_WIDTH - 1:CONV_WIDTH] * x
        for j in range(CONV_WIDTH - 1):
            off = SUBLANES - (CONV_WIDTH - 1) + j
            y = y + cw[j:j + 1] * buf_ref[slot, off:off + sb, :]
        return _silu(y)

    def l2n(x):
        return x * lax.rsqrt(jnp.sum(x * x, axis=-1, keepdims=True) + EPS)

    ri = lax.broadcasted_iota(jnp.int32, (sb, sb), 0)
    ci = lax.broadcasted_iota(jnp.int32, (sb, sb), 1)
    same = (ri // CHUNK) == (ci // CHUNK)
    incl = same & (ci <= ri)
    strict = same & (ci < ri)
    eye = (ri == ci).astype(F32)
    levels = []
    bs = 1
    while bs < CHUNK:
        levels.append(((ri // bs) % 2 == 1) & ((ci // bs) % 2 == 0)
                      & ((ri // (2 * bs)) == (ci // (2 * bs))))
        bs *= 2
    rowmod = lax.broadcasted_iota(jnp.int32, (sb, hdim), 0) % CHUNK
    ab = x_ref[0, :, 4 * gw:4 * gw + LANES]

    for h in range(GDN_HEADS):
        q = l2n(conv_silu(h)) * (hdim ** -0.5)
        k = l2n(conv_silu(GDN_HEADS + h))
        v = conv_silu(2 * GDN_HEADS + h)
        g = -jnp.exp(alog_ref[h]) * jax.nn.softplus(ab[:, h:h + 1] + dtb_ref[h])
        beta = jax.nn.sigmoid(ab[:, GDN_HEADS + h:GDN_HEADS + h + 1])

        gc = g
        sh = 1
        while sh < CHUNK:
            gc = gc + jnp.where(rowmod >= sh, pltpu.roll(gc, sh, axis=0), 0.0)
            sh *= 2
        gcl = jnp.concatenate(
            [jnp.broadcast_to(gc[(n + 1) * CHUNK - 1:(n + 1) * CHUNK, :], (CHUNK, hdim))
             for n in range(nchunk)], axis=0)
        gc_row = jnp.broadcast_to(gc.T[0:1, :], (sb, sb))
        decay = jnp.where(incl, jnp.exp(jnp.where(incl, gc[:, 0:1] - gc_row, 0.0)), 0.0)

        kb = k * beta
        k16 = k.astype(BF16)
        mmat = jnp.where(strict, _dot_nt(kb.astype(BF16), k16) * decay, 0.0)
        tinv = eye - jnp.where(levels[0], mmat, 0.0)
        for off in levels[1:]:
            t16 = tinv.astype(BF16)
            a16 = jnp.where(off, mmat, 0.0).astype(BF16)
            tinv = tinv - _dot(_dot(t16, a16).astype(BF16), t16)

        egc = jnp.exp(gc)
        rhs = jnp.concatenate([v * beta, kb * egc], axis=1)
        sol = rhs + _dot((tinv - eye).astype(BF16), rhs.astype(BF16))
        u = sol[:, :hdim]
        w16 = sol[:, hdim:].astype(BF16)
        attn = _dot_nt(q.astype(BF16), k16) * decay
        qd16 = (q * egc).astype(BF16)
        kd16 = (k * jnp.exp(gcl - gc)).astype(BF16)
        g_last = jnp.exp(gcl)

        st = state_ref[h]
        o_parts = []
        for n in range(nchunk):
            lo = n * CHUNK
            st16 = st.astype(BF16)
            v_new = u[lo:lo + CHUNK] - _dot(w16[lo:lo + CHUNK], st16)
            vnew_ref[h, lo:lo + CHUNK, :] = v_new
            o_parts.append(_dot(qd16[lo:lo + CHUNK], st16))
            st = st * g_last[lo:lo + 1, :] + _dot_tn(kd16[lo:lo + CHUNK], v_new.astype(BF16))
        state_ref[h] = st
        o = jnp.concatenate(o_parts, axis=0) + _dot(attn.astype(BF16), vnew_ref[h].astype(BF16))

        z = x_ref[0, :, 3 * gw + h * hdim:3 * gw + (h + 1) * hdim]
        o = _rms(o) * ng_ref[...] * _silu(z)
        o_ref[0, :, h * hdim:(h + 1) * hdim] = o.astype(o_ref.dtype)


def _gdn(gdn_in, conv_w, a_log, dt_bias, norm_g, sb):
    bsz, s, cols = gdn_in.shape
    hdim = GDN_HEAD_DIM
    nh = GDN_HEADS
    cw = conv_w.reshape(CONV_WIDTH, 3 * nh, hdim).transpose(1, 0, 2)
    alog = jnp.broadcast_to(a_log.reshape(nh, 1, 1), (nh, 1, hdim)).astype(F32)
    dtb = jnp.broadcast_to(dt_bias.reshape(nh, 1, 1), (nh, 1, hdim)).astype(F32)
    full = lambda shape: pl.BlockSpec(shape, lambda b, i: (0,) * len(shape))
    return pl.pallas_call(
        _gdn_kernel,
        grid=(bsz, s // sb),
        in_specs=[pl.BlockSpec((1, sb, cols), lambda b, i: (b, i, 0)),
                  full(cw.shape), full(alog.shape), full(dtb.shape), full((1, hdim))],
        out_specs=pl.BlockSpec((1, sb, nh * hdim), lambda b, i: (b, i, 0)),
        out_shape=jax.ShapeDtypeStruct((bsz, s, nh * hdim), BF16),
        scratch_shapes=[pltpu.VMEM((nh, hdim, hdim), F32),
                        pltpu.VMEM((3 * nh, SUBLANES, hdim), F32),
                        pltpu.VMEM((3 * nh, sb + SUBLANES, hdim), F32),
                        pltpu.VMEM((nh, sb, hdim), F32)],
        compiler_params=_params("parallel", "arbitrary"),
        name="gdn",
    )(gdn_in, cw, alog, dtb, norm_g.reshape(1, hdim))


MLA_QK_PAD = 256


def _mla_prep_kernel(m_ref, cos_ref, sin_ref, gq_ref, gkv_ref, wq_ref, wkv_ref,
                     q_ref, k_ref, v_ref):
    m = m_ref[0]
    cosr = cos_ref[0]
    sinr = sin_ref[0]
    cq = m[:, :Q_LORA_RANK]
    ckv = m[:, Q_LORA_RANK:Q_LORA_RANK + KV_LORA_RANK]
    o = Q_LORA_RANK + KV_LORA_RANK
    kr = m[:, o:o + LANES]
    krs = m[:, o + LANES:o + 2 * LANES]
    scale = (QK_NOPE_DIM + QK_ROPE_DIM) ** -0.5
    qa = _dot((_rms(cq) * gq_ref[...]).astype(BF16), wq_ref[...])
    kva = _dot((_rms(ckv) * gkv_ref[...]).astype(BF16), wkv_ref[...])
    k_rope = (kr * cosr + krs * sinr).astype(k_ref.dtype)
    sw0 = MLA_HEADS * MLA_QK_PAD
    for h in range(MLA_HEADS):
        b0 = h * MLA_QK_PAD
        rope = qa[:, b0 + LANES:b0 + 2 * LANES] * cosr + qa[:, sw0 + h * LANES:sw0 + (h + 1) * LANES] * sinr
        q_ref[0, h, :, 0:LANES] = (qa[:, b0:b0 + LANES] * scale).astype(q_ref.dtype)
        q_ref[0, h, :, LANES:2 * LANES] = (rope * scale).astype(q_ref.dtype)
        c0 = h * (QK_NOPE_DIM + V_HEAD_DIM)
        k_ref[0, h, :, 0:LANES] = kva[:, c0:c0 + QK_NOPE_DIM].astype(k_ref.dtype)
        k_ref[0, h, :, LANES:2 * LANES] = k_rope
        v_ref[0, h] = kva[:, c0 + QK_NOPE_DIM:c0 + QK_NOPE_DIM + V_HEAD_DIM].astype(v_ref.dtype)


def _mla_prep(mla_in, cosr, sinr, gq, gkv, wq, wkv, tm):
    bsz, s, mc = mla_in.shape
    nh = MLA_HEADS
    return pl.pallas_call(
        _mla_prep_kernel,
        grid=(bsz, s // tm),
        in_specs=[pl.BlockSpec((1, tm, mc), lambda b, i: (b, i, 0)),
                  pl.BlockSpec((1, tm, LANES), lambda b, i: (b, i, 0)),
                  pl.BlockSpec((1, tm, LANES), lambda b, i: (b, i, 0)),
                  pl.BlockSpec((1, Q_LORA_RANK), lambda b, i: (0, 0)),
                  pl.BlockSpec((1, KV_LORA_RANK), lambda b, i: (0, 0)),
                  pl.BlockSpec(wq.shape, lambda b, i: (0, 0)),
                  pl.BlockSpec(wkv.shape, lambda b, i: (0, 0))],
        out_specs=[pl.BlockSpec((1, nh, tm, MLA_QK_PAD), lambda b, i: (b, 0, i, 0)),
                   pl.BlockSpec((1, nh, tm, MLA_QK_PAD), lambda b, i: (b, 0, i, 0)),
                   pl.BlockSpec((1, nh, tm, V_HEAD_DIM), lambda b, i: (b, 0, i, 0))],
        out_shape=[jax.ShapeDtypeStruct((bsz, nh, s, MLA_QK_PAD), BF16),
                   jax.ShapeDtypeStruct((bsz, nh, s, MLA_QK_PAD), BF16),
                   jax.ShapeDtypeStruct((bsz, nh, s, V_HEAD_DIM), BF16)],
        compiler_params=_params("parallel", "parallel"),
        name="mla_prep",
    )(mla_in, cosr, sinr, gq, gkv, wq, wkv)


def _fold_lanes(x, op):
    parts = [x[:, c * LANES:(c + 1) * LANES] for c in range(x.shape[1] // LANES)]
    while len(parts) > 1:
        parts = [op(parts[i], parts[i + 1]) for i in range(0, len(parts), 2)]
    return parts[0]


def _attn_kernel(q_ref, k_ref, v_ref, o_ref, m_ref, l_ref, acc_ref):
    i = pl.program_id(2)
    tq = q_ref.shape[2]
    q = q_ref[0, 0]
    m_ref[...] = jnp.full_like(m_ref, -jnp.inf)
    l_ref[...] = jnp.zeros_like(l_ref)
    acc_ref[...] = jnp.zeros_like(acc_ref)

    def step(j, diagonal):
        start = pl.multiple_of(j * tq, tq)
        s = _dot_nt(q, k_ref[0, 0, pl.ds(start, tq), :])
        if diagonal:
            row = lax.broadcasted_iota(jnp.int32, s.shape, 0)
            col = lax.broadcasted_iota(jnp.int32, s.shape, 1)
            s = jnp.where(col <= row, s, -jnp.inf)
        m_prev = m_ref[...]
        m_new = jnp.maximum(m_prev, jnp.max(_fold_lanes(s, jnp.maximum), axis=-1, keepdims=True))
        alpha = jnp.exp(m_prev - m_new)
        p = jnp.exp(s - m_new)
        l_ref[...] = alpha * l_ref[...] + jnp.sum(_fold_lanes(p, jnp.add), axis=-1, keepdims=True)
        acc_ref[...] = alpha * acc_ref[...] + _dot(p.astype(v_ref.dtype),
                                                   v_ref[0, 0, pl.ds(start, tq), :])
        m_ref[...] = m_new

    def body(j, carry):
        step(j, False)
        return carry
    lax.fori_loop(0, i, body, 0)
    step(i, True)
    o_ref[0] = (acc_ref[...] / l_ref[...]).astype(o_ref.dtype)


def _attn(q, k, v, tq):
    bsz, nh, s, dq = q.shape
    dv = v.shape[-1]
    return pl.pallas_call(
        _attn_kernel,
        grid=(bsz, nh, s // tq),
        in_specs=[pl.BlockSpec((1, 1, tq, dq), lambda b, h, i: (b, h, i, 0)),
                  pl.BlockSpec((1, 1, s, dq), lambda b, h, i: (b, h, 0, 0)),
                  pl.BlockSpec((1, 1, s, dv), lambda b, h, i: (b, h, 0, 0))],
        out_specs=pl.BlockSpec((1, tq, dv), lambda b, h, i: (b, i, h)),
        out_shape=jax.ShapeDtypeStruct((bsz, s, nh * dv), BF16),
        scratch_shapes=[pltpu.VMEM((tq, 1), F32), pltpu.VMEM((tq, 1), F32),
                        pltpu.VMEM((tq, dv), F32)],
        compiler_params=_params("parallel", "parallel", "arbitrary"),
        name="attn",
    )(q, k, v)


def _outproj_kernel(og_ref, om_ref, x_ref, mod_ref, wo_ref, g_ref, wpq_ref,
                    x1_ref, h2_ref, qp_ref):
    m = mod_ref[0]
    gw = og_ref.shape[-1]
    mixed = _dot(og_ref[0], wo_ref[0:gw, :]) + _dot(om_ref[0], wo_ref[gw:, :])
    x1 = x_ref[0] + m[2:3] * mixed
    h2 = _rms(x1) * g_ref[...] * (1.0 + m[4:5]) + m[3:4]
    x1_ref[0] = x1
    h2_ref[0] = h2.astype(h2_ref.dtype)
    qp_ref[0] = _dot(h2.astype(BF16), wpq_ref[...]).astype(qp_ref.dtype)


def _out_proj(o_gdn, o_mla, x, mod, w_out, g, w_pq, tm):
    bsz, s, d = x.shape
    gw = o_gdn.shape[-1]
    mw = o_mla.shape[-1]
    nq = w_pq.shape[1]
    return pl.pallas_call(
        _outproj_kernel,
        grid=(bsz, s // tm),
        in_specs=[pl.BlockSpec((1, tm, gw), lambda b, i: (b, i, 0)),
                  pl.BlockSpec((1, tm, mw), lambda b, i: (b, i, 0)),
                  pl.BlockSpec((1, tm, d), lambda b, i: (b, i, 0)),
                  pl.BlockSpec((1, N_MOD, d), lambda b, i: (b, 0, 0)),
                  pl.BlockSpec((gw + mw, d), lambda b, i: (0, 0)),
                  pl.BlockSpec((1, d), lambda b, i: (0, 0)),
                  pl.BlockSpec((d, nq), lambda b, i: (0, 0))],
        out_specs=[pl.BlockSpec((1, tm, d), lambda b, i: (b, i, 0)),
                   pl.BlockSpec((1, tm, d), lambda b, i: (b, i, 0)),
                   pl.BlockSpec((1, tm, nq), lambda b, i: (b, i, 0))],
        out_shape=[jax.ShapeDtypeStruct((bsz, s, d), F32),
                   jax.ShapeDtypeStruct((bsz, s, d), BF16),
                   jax.ShapeDtypeStruct((bsz, s, nq), BF16)],
        compiler_params=_params("parallel", "parallel"),
        name="out_proj",
    )(o_gdn, o_mla, x, mod, w_out, g, w_pq)


PEER_CAND_ROWS = 2 * SUBLANES + (SUBLANES - 1) * SUBLANES + SUBLANES


def _peer_topk_kernel(qp_ref, keys_ref, idx_ref, gate_ref,
                      stop_ref, itop_ref, cand_ref, cidx_ref, best_ref, idxt_ref, gatet_ref):
    tb = qp_ref.shape[0]
    kk = PEER_TOPK
    assert kk == 2 * SUBLANES

    def extract(vals, row, n):
        m = jnp.max(vals, axis=0, keepdims=True)
        pos = jnp.min(jnp.where(vals == m, row, n), axis=0, keepdims=True)
        return m, pos, jnp.where(row == pos, -jnp.inf, vals)

    row_k = lax.broadcasted_iota(jnp.int32, (N_KEYS, tb), 0)
    row_c = lax.broadcasted_iota(jnp.int32, (PEER_CAND_ROWS, tb), 0)
    row_8 = lax.broadcasted_iota(jnp.int32, (SUBLANES, tb), 0)
    for h in range(PEER_HEADS):
        for p in range(2):
            c0 = (2 * h + p) * N_KEYS
            vals = _dot_nt(keys_ref[p, h], qp_ref[:, c0:c0 + N_KEYS])
            for r in range(kk):
                m, pos, vals = extract(vals, row_k, N_KEYS)
                stop_ref[p, r:r + 1, :] = m
                itop_ref[p, r:r + 1, :] = pos
        cand_ref[0:kk, :] = stop_ref[0, 0:1, :] + stop_ref[1]
        cidx_ref[0:kk, :] = itop_ref[0, 0:1, :] * N_KEYS + itop_ref[1]
        s2 = stop_ref[1, 0:SUBLANES, :]
        i2 = itop_ref[1, 0:SUBLANES, :]
        for a in range(1, SUBLANES):
            r0 = kk + (a - 1) * SUBLANES
            cand_ref[r0:r0 + SUBLANES, :] = jnp.where(
                row_8 < kk // (a + 1), stop_ref[0, a:a + 1, :] + s2, -jnp.inf)
            cidx_ref[r0:r0 + SUBLANES, :] = itop_ref[0, a:a + 1, :] * N_KEYS + i2
        r0 = kk + (SUBLANES - 1) * SUBLANES
        cand_ref[r0:r0 + SUBLANES, :] = stop_ref[0, SUBLANES:kk, :] + stop_ref[1, 0:1, :]
        cidx_ref[r0:r0 + SUBLANES, :] = itop_ref[0, SUBLANES:kk, :] * N_KEYS + itop_ref[1, 0:1, :]
        vals = cand_ref[...]
        cidx = cidx_ref[...]
        for r in range(kk):
            m, pos, vals = extract(vals, row_c, PEER_CAND_ROWS)
            best_ref[r:r + 1, :] = m
            idxt_ref[h * kk + r:h * kk + r + 1, :] = jnp.max(
                jnp.where(row_c == pos, cidx, -1), axis=0, keepdims=True)
        best = best_ref[...]
        e = jnp.exp(best - best[0:1, :])
        gatet_ref[h * kk:(h + 1) * kk, :] = e / jnp.sum(e, axis=0, keepdims=True)
    idx_ref[...] = idxt_ref[...].T
    gate_ref[...] = gatet_ref[...].T


def _peer_topk(qp, keys, tb, part, nparts):
    nq = qp.shape[1]
    t = qp.shape[0] // nparts
    blk0 = part * (t // tb)
    hk = PEER_HEADS * PEER_TOPK
    kk = PEER_TOPK
    return pl.pallas_call(
        _peer_topk_kernel,
        grid=(t // tb,),
        in_specs=[pl.BlockSpec((tb, nq), lambda i: (i + blk0, 0)),
                  pl.BlockSpec(keys.shape, lambda i: (0, 0, 0, 0))],
        out_specs=[pl.BlockSpec((tb, hk), lambda i: (i, 0)),
                   pl.BlockSpec((tb, hk), lambda i: (i, 0))],
        out_shape=[jax.ShapeDtypeStruct((t, hk), jnp.int32),
                   jax.ShapeDtypeStruct((t, hk), F32)],
        scratch_shapes=[pltpu.VMEM((2, kk, tb), F32), pltpu.VMEM((2, kk, tb), jnp.int32),
                        pltpu.VMEM((PEER_CAND_ROWS, tb), F32),
                        pltpu.VMEM((PEER_CAND_ROWS, tb), jnp.int32),
                        pltpu.VMEM((kk, tb), F32),
                        pltpu.VMEM((hk, tb), jnp.int32), pltpu.VMEM((hk, tb), F32)],
        compiler_params=_params("parallel"),
        name="peer_topk",
    )(qp, keys)


PEER_TOKEN_GROUP = 16
PEER_PARTS = 4


def _peer_gate_matrix(idx, gate, n_experts):
    t_total, hk = idx.shape
    ln = SC_LANES
    nu = hk // ln
    grp = PEER_TOKEN_GROUP
    tok_per_w = t_total // SC_WORKERS
    n_groups = tok_per_w // grp
    assert n_groups * grp * SC_WORKERS == t_total and grp % 2 == 0
    mesh = plsc.VectorSubcoreMesh(core_axis_name="c", subcore_axis_name="s",
                                  num_cores=SC_CORES, num_subcores=SC_SUBCORES)

    @functools.partial(
        pl.kernel, mesh=mesh,
        compiler_params=pltpu.CompilerParams(needs_layout_passes=False),
        out_type=jax.ShapeDtypeStruct((t_total, n_experts), F32),
        scratch_types=[
            pltpu.VMEM((grp * hk,), jnp.int32),
            pltpu.VMEM((grp * hk,), F32),
            pltpu.VMEM((n_experts,), F32),
            pltpu.VMEM((n_experts,), F32),
            pltpu.SemaphoreType.DMA((2,)),
        ],
    )
    def gate_kernel(idx_hbm, gate_hbm, g_hbm, idx_v, gate_v, row0_v, row1_v, sem):
        rows = (row0_v, row1_v)
        wid = lax.axis_index("s") * SC_CORES + lax.axis_index("c")
        base = wid * tok_per_w
        zero = jnp.zeros((ln,), F32)

        def zero_body(c, carry):
            row0_v[pl.ds(c * ln, ln)] = zero
            row1_v[pl.ds(c * ln, ln)] = zero
            return carry
        lax.fori_loop(0, n_experts // ln, zero_body, 0)

        def out_copy(tok, slot):
            return pltpu.make_async_copy(rows[slot], g_hbm.at[tok], sem.at[slot])

        def group_body(g, carry):
            tok0 = base + g * grp
            pltpu.sync_copy(idx_hbm.at[pl.ds(tok0 * hk, grp * hk)], idx_v)
            pltpu.sync_copy(gate_hbm.at[pl.ds(tok0 * hk, grp * hk)], gate_v)

            def pair_body(i, carry2):
                for slot in range(2):
                    t = i * 2 + slot
                    for u in range(nu):
                        sl = pl.ds(t * hk + u * ln, ln)
                        plsc.addupdate_scatter(rows[slot], [idx_v[sl]], gate_v[sl])
                    out_copy(tok0 + t, slot).start()
                for slot in range(2):
                    t = i * 2 + slot
                    out_copy(tok0 + t, slot).wait()
                    for u in range(nu):
                        plsc.store_scatter(rows[slot], [idx_v[pl.ds(t * hk + u * ln, ln)]], zero)
                return carry2
            lax.fori_loop(0, grp // 2, pair_body, 0)
            return carry
        lax.fori_loop(0, n_groups, group_body, 0)

    return gate_kernel(idx.reshape(-1), gate.reshape(-1))


def _peer_dense_kernel(h_ref, g_ref, u_ref, v_ref, o_ref):
    e = pl.program_id(1)
    s = _dot_nt(h_ref[...], u_ref[...])
    p = (jax.nn.gelu(s) * g_ref[...]).astype(v_ref.dtype)
    contrib = _dot(p, v_ref[...])

    @pl.when(e == 0)
    def _():
        o_ref[...] = contrib

    @pl.when(e > 0)
    def _():
        o_ref[...] += contrib


def _peer_dense(h, gmat, u, v, tb, eb, part):
    t = gmat.shape[0]
    d = h.shape[1]
    n_e = u.shape[0]
    blk0 = part * (t // tb)
    return pl.pallas_call(
        _peer_dense_kernel,
        grid=(t // tb, n_e // eb),
        in_specs=[pl.BlockSpec((tb, d), lambda i, e: (i + blk0, 0)),
                  pl.BlockSpec((tb, eb), lambda i, e: (i, e)),
                  pl.BlockSpec((eb, d), lambda i, e: (e, 0)),
                  pl.BlockSpec((eb, d), lambda i, e: (e, 0))],
        out_specs=pl.BlockSpec((tb, d), lambda i, e: (i, 0)),
        out_shape=jax.ShapeDtypeStruct((t, d), F32),
        compiler_params=_params("parallel", "arbitrary"),
        name="peer_dense",
    )(h, gmat, u, v)


def _final_kernel(x1_ref, y_ref, mod_ref, g_ref, *rest, normalize):
    o_ref = rest[-1]
    m = mod_ref[0]
    x2 = x1_ref[0] + m[5:6] * y_ref[0]
    o_ref[0] = _rms(x2) * g_ref[...] if normalize else x2


def _final(x1, y, mod, g, tm, normalize, part, prev):
    bsz, s, d = x1.shape
    pb = y.shape[0]
    b0 = part * pb
    full_blk = pl.BlockSpec((1, tm, d), lambda b, i: (b + b0, i, 0))
    in_specs = [full_blk, pl.BlockSpec((1, tm, d), lambda b, i: (b, i, 0)),
                pl.BlockSpec((1, N_MOD, d), lambda b, i: (b + b0, 0, 0)),
                pl.BlockSpec((1, d), lambda b, i: (0, 0))]
    args = [x1, y, mod, g]
    aliases = {}
    if prev is not None:
        in_specs.append(pl.BlockSpec(memory_space=pl.ANY))
        args.append(prev)
        aliases = {4: 0}
    return pl.pallas_call(
        functools.partial(_final_kernel, normalize=normalize),
        grid=(pb, s // tm),
        in_specs=in_specs,
        out_specs=full_blk,
        out_shape=jax.ShapeDtypeStruct((bsz, s, d), F32),
        input_output_aliases=aliases,
        compiler_params=_params("parallel", "parallel"),
        name="final",
    )(*args)


def _split_w_in(w):
    gw = GDN_HEADS * GDN_HEAD_DIM
    sizes = [gw] * 4 + [GDN_HEADS] * 2 + [Q_LORA_RANK, KV_LORA_RANK, QK_ROPE_DIM]
    offs = [0]
    for sz in sizes:
        offs.append(offs[-1] + sz)
    parts = [w[:, offs[i]:offs[i + 1]] for i in range(len(sizes))]
    gq, gk, gv, gz, ga, gb, cq, ckv, kr = parts
    d = w.shape[0]
    half = QK_ROPE_DIM // 2
    zeros = lambda n: jnp.zeros((d, n), w.dtype)
    gdn = jnp.concatenate([gq, gk, gv, gz, ga, gb, zeros(LANES - 2 * GDN_HEADS)], axis=1)
    kr_sw = jnp.concatenate([kr[:, half:], kr[:, :half]], axis=1)
    mla = jnp.concatenate([cq, ckv, kr, zeros(LANES - QK_ROPE_DIM),
                           kr_sw, zeros(LANES - QK_ROPE_DIM)], axis=1)
    return jnp.concatenate([gdn, mla], axis=1).astype(BF16), gdn.shape[1], mla.shape[1]


def _split_w_uq(w):
    r = w.shape[0]
    half = QK_ROPE_DIM // 2
    hd = QK_NOPE_DIM + QK_ROPE_DIM
    main, swapped = [], []
    for h in range(MLA_HEADS):
        nope = w[:, h * hd:h * hd + QK_NOPE_DIM]
        rope = w[:, h * hd + QK_NOPE_DIM:(h + 1) * hd]
        main += [nope, rope, jnp.zeros((r, MLA_QK_PAD - hd), w.dtype)]
        swapped += [rope[:, half:], rope[:, :half], jnp.zeros((r, LANES - QK_ROPE_DIM), w.dtype)]
    return jnp.concatenate(main + swapped, axis=1).astype(BF16)


def _rotary_tables(positions):
    half = QK_ROPE_DIM // 2
    inv_freq = ROPE_THETA ** (-jnp.arange(half, dtype=F32) / half)
    ang = positions.astype(F32)[..., None] * inv_freq
    cos, sin = jnp.cos(ang), jnp.sin(ang)
    z = jnp.zeros(cos.shape[:-1] + (LANES - QK_ROPE_DIM,), F32)
    return (jnp.concatenate([cos, cos, z], axis=-1),
            jnp.concatenate([-sin, sin, z], axis=-1))


def _block(n, pref):
    return pref if n % pref == 0 else n


def kernel(x, c, positions, ln_mix_g, w_in, conv_w, a_log, dt_bias, gdn_norm_g, q_norm_g, w_uq, kv_norm_g, w_ukv, w_out, ln_ffn_g, w_pq, sub_keys, expert_u, expert_v, w_ada, b_ada, final_norm_g):
    bsz, s, d = x.shape
    depth = w_in.shape[0]
    cosr, sinr = _rotary_tables(positions)
    tm = _block(s, 256)
    for layer in range(depth):
        mod = _mod(c, w_ada[layer], b_ada[layer]).reshape(bsz, N_MOD, d)
        w_cat, gdn_cols, mla_cols = _split_w_in(w_in[layer])
        gdn_in, mla_in = _in_proj(x, mod, ln_mix_g[layer].reshape(1, d), w_cat, gdn_cols, mla_cols, tm)
        o_gdn = _gdn(gdn_in, conv_w[layer], a_log[layer], dt_bias[layer], gdn_norm_g[layer],
                     _block(s, 256))
        q, k, v = _mla_prep(mla_in, cosr, sinr, q_norm_g[layer].reshape(1, -1),
                            kv_norm_g[layer].reshape(1, -1), _split_w_uq(w_uq[layer]),
                            w_ukv[layer].astype(BF16), _block(s, 512))
        o_mla = _attn(q, k, v, _block(s, 1024))
        x1, h2, qp = _out_proj(o_gdn, o_mla, x, mod, w_out[layer].astype(BF16),
                               ln_ffn_g[layer].reshape(1, d), w_pq[layer].astype(BF16), tm)
        n_experts = expert_u.shape[1]
        u16 = expert_u[layer].astype(BF16)
        v16 = expert_v[layer].astype(BF16)
        qp2 = qp.reshape(bsz * s, -1)
        h22 = h2.reshape(bsz * s, d)
        nparts = PEER_PARTS if bsz % PEER_PARTS == 0 else 1
        tpart = bsz * s // nparts
        last = layer + 1 == depth
        out = None
        for part in range(nparts):
            idx, gate = _peer_topk(qp2, sub_keys[layer].astype(BF16), _block(tpart, 256), part, nparts)
            gmat = _peer_gate_matrix(idx, gate, n_experts)
            y = _peer_dense(h22, gmat, u16, v16, _block(tpart, 1024), _block(n_experts, 1024), part)
            out = _final(x1, y.reshape(bsz // nparts, s, d), mod, final_norm_g.reshape(1, d), tm,
                         last, part, out)
        x = out
    return x
```

```python
import functools

import jax
import jax.numpy as jnp
from jax import lax
from jax.experimental import pallas as pl
from jax.experimental.pallas import tpu as pltpu
from jax.experimental.pallas import tpu_sc as plsc

GDN_HEADS = 4
GDN_HEAD_DIM = 128
CONV_WIDTH = 4
CHUNK = 64
MLA_HEADS = 4
QK_NOPE_DIM = 128
QK_ROPE_DIM = 64
V_HEAD_DIM = 128
Q_LORA_RANK = 384
KV_LORA_RANK = 256
ROPE_THETA = 10000.0
PEER_HEADS = 8
N_KEYS = 128
PEER_TOPK = 16
N_MOD = 6
EPS = 1e-6

LANES = 128
SUBLANES = 8
SC_CORES = 2
SC_SUBCORES = 16
SC_LANES = 16
SC_WORKERS = SC_CORES * SC_SUBCORES
VMEM_LIMIT = 48 * 1024 * 1024

F32 = jnp.float32
BF16 = jnp.bfloat16
HI = lax.Precision.HIGHEST


def _dot(a, b, precision=None):
    return jnp.dot(a, b, preferred_element_type=F32, precision=precision)


def _dot_nt(a, b, precision=None):
    return lax.dot_general(a, b, (((1,), (1,)), ((), ())),
                           preferred_element_type=F32, precision=precision)


def _dot_tn(a, b, precision=None):
    return lax.dot_general(a, b, (((0,), (0,)), ((), ())),
                           preferred_element_type=F32, precision=precision)


def _rms(x):
    return x * lax.rsqrt(jnp.mean(x * x, axis=-1, keepdims=True) + EPS)


def _silu(x):
    return x * jax.nn.sigmoid(x)


def _params(*sem):
    return pltpu.CompilerParams(dimension_semantics=sem, vmem_limit_bytes=VMEM_LIMIT)


def _mod_kernel(c_ref, w_ref, b_ref, o_ref):
    o_ref[...] = _dot(_silu(c_ref[...]), w_ref[...], HI) + b_ref[...]


def _mod(c, w_ada, b_ada):
    bsz, d = c.shape
    n = w_ada.shape[1]
    return pl.pallas_call(
        _mod_kernel,
        grid=(n // d,),
        in_specs=[pl.BlockSpec((bsz, d), lambda j: (0, 0)),
                  pl.BlockSpec((d, d), lambda j: (0, j)),
                  pl.BlockSpec((1, d), lambda j: (0, j))],
        out_specs=pl.BlockSpec((bsz, d), lambda j: (0, j)),
        out_shape=jax.ShapeDtypeStruct((bsz, n), F32),
        compiler_params=_params("arbitrary"),
        name="mod",
    )(c, w_ada, b_ada.reshape(1, n))


def _inproj_kernel(x_ref, mod_ref, g_ref, w_ref, gdn_ref, mla_ref):
    m = mod_ref[0]
    h = _rms(x_ref[0]) * g_ref[...] * (1.0 + m[1:2]) + m[0:1]
    p = _dot(h.astype(BF16), w_ref[...])
    gw = gdn_ref.shape[-1]
    gdn_ref[0] = p[:, :gw]
    mla_ref[0] = p[:, gw:]


def _in_proj(x, mod, g, w_cat, gdn_cols, mla_cols, tm):
    bsz, s, d = x.shape
    return pl.pallas_call(
        _inproj_kernel,
        grid=(bsz, s // tm),
        in_specs=[pl.BlockSpec((1, tm, d), lambda b, i: (b, i, 0)),
                  pl.BlockSpec((1, N_MOD, d), lambda b, i: (b, 0, 0)),
                  pl.BlockSpec((1, d), lambda b, i: (0, 0)),
                  pl.BlockSpec((d, gdn_cols + mla_cols), lambda b, i: (0, 0))],
        out_specs=[pl.BlockSpec((1, tm, gdn_cols), lambda b, i: (b, i, 0)),
                   pl.BlockSpec((1, tm, mla_cols), lambda b, i: (b, i, 0))],
        out_shape=[jax.ShapeDtypeStruct((bsz, s, gdn_cols), F32),
                   jax.ShapeDtypeStruct((bsz, s, mla_cols), F32)],
        compiler_params=_params("parallel", "parallel"),
        name="in_proj",
    )(x, mod, g, w_cat)


def _gdn_kernel(x_ref, cw_ref, alog_ref, dtb_ref, ng_ref, o_ref,
                state_ref, tail_ref, buf_ref, vnew_ref):
    sb = x_ref.shape[1]
    nchunk = sb // CHUNK
    hdim = GDN_HEAD_DIM
    gw = GDN_HEADS * hdim

    @pl.when(pl.program_id(1) == 0)
    def _():
        state_ref[...] = jnp.zeros_like(state_ref)
        tail_ref[...] = jnp.zeros_like(tail_ref)

    def conv_silu(slot):
        x = x_ref[0, :, slot * hdim:(slot + 1) * hdim]
        buf_ref[slot, 0:SUBLANES, :] = tail_ref[slot]
        buf_ref[slot, SUBLANES:SUBLANES + sb, :] = x
        tail_ref[slot] = x[sb - SUBLANES:sb, :]
        cw = cw_ref[slot]
        y = cw[CONV_WIDTH - 1:CONV_WIDTH] * x
        for j in range(CONV_WIDTH - 1):
            off = SUBLANES - (CONV_WIDTH - 1) + j
            y = y + cw[j:j + 1] * buf_ref[slot, off:off + sb, :]
        return _silu(y)

    def l2n(x):
        return x * lax.rsqrt(jnp.sum(x * x, axis=-1, keepdims=True) + EPS)

    ri = lax.broadcasted_iota(jnp.int32, (sb, sb), 0)
    ci = lax.broadcasted_iota(jnp.int32, (sb, sb), 1)
    same = (ri // CHUNK) == (ci // CHUNK)
    incl = same & (ci <= ri)
    strict = same & (ci < ri)
    eye = (ri == ci).astype(F32)
    levels = []
    bs = 1
    while bs < CHUNK:
        levels.append(((ri // bs) % 2 == 1) & ((ci // bs) % 2 == 0)
                      & ((ri // (2 * bs)) == (ci // (2 * bs))))
        bs *= 2
    rowmod = lax.broadcasted_iota(jnp.int32, (sb, hdim), 0) % CHUNK
    ab = x_ref[0, :, 4 * gw:4 * gw + LANES]

    for h in range(GDN_HEADS):
        q = l2n(conv_silu(h)) * (hdim ** -0.5)
        k = l2n(conv_silu(GDN_HEADS + h))
        v = conv_silu(2 * GDN_HEADS + h)
        g = -jnp.exp(alog_ref[h]) * jax.nn.softplus(ab[:, h:h + 1] + dtb_ref[h])
        beta = jax.nn.sigmoid(ab[:, GDN_HEADS + h:GDN_HEADS + h + 1])

        gc = g
        sh = 1
        while sh < CHUNK:
            gc = gc + jnp.where(rowmod >= sh, pltpu.roll(gc, sh, axis=0), 0.0)
            sh *= 2
        gcl = jnp.concatenate(
            [jnp.broadcast_to(gc[(n + 1) * CHUNK - 1:(n + 1) * CHUNK, :], (CHUNK, hdim))
             for n in range(nchunk)], axis=0)
        gc_row = jnp.broadcast_to(gc.T[0:1, :], (sb, sb))
        decay = jnp.where(incl, jnp.exp(jnp.where(incl, gc[:, 0:1] - gc_row, 0.0)), 0.0)

        kb = k * beta
        k16 = k.astype(BF16)
        mmat = jnp.where(strict, _dot_nt(kb.astype(BF16), k16) * decay, 0.0)
        tinv = eye - jnp.where(levels[0], mmat, 0.0)
        for off in levels[1:]:
            t16 = tinv.astype(BF16)
            a16 = jnp.where(off, mmat, 0.0).astype(BF16)
            tinv = tinv - _dot(_dot(t16, a16).astype(BF16), t16)

        egc = jnp.exp(gc)
        rhs = jnp.concatenate([v * beta, kb * egc], axis=1)
        sol = rhs + _dot((tinv - eye).astype(BF16), rhs.astype(BF16))
        u = sol[:, :hdim]
        w16 = sol[:, hdim:].astype(BF16)
        attn = _dot_nt(q.astype(BF16), k16) * decay
        qd16 = (q * egc).astype(BF16)
        kd16 = (k * jnp.exp(gcl - gc)).astype(BF16)
        g_last = jnp.exp(gcl)

        st = state_ref[h]
        o_parts = []
        for n in range(nchunk):
            lo = n * CHUNK
            st16 = st.astype(BF16)
            v_new = u[lo:lo + CHUNK] - _dot(w16[lo:lo + CHUNK], st16)
            vnew_ref[h, lo:lo + CHUNK, :] = v_new
            o_parts.append(_dot(qd16[lo:lo + CHUNK], st16))
            st = st * g_last[lo:lo + 1, :] + _dot_tn(kd16[lo:lo + CHUNK], v_new.astype(BF16))
        state_ref[h] = st
        o = jnp.concatenate(o_parts, axis=0) + _dot(attn.astype(BF16), vnew_ref[h].astype(BF16))

        z = x_ref[0, :, 3 * gw + h * hdim:3 * gw + (h + 1) * hdim]
        o = _rms(o) * ng_ref[...] * _silu(z)
        o_ref[0, :, h * hdim:(h + 1) * hdim] = o.astype(o_ref.dtype)


def _gdn(gdn_in, conv_w, a_log, dt_bias, norm_g, sb):
    bsz, s, cols = gdn_in.shape
    hdim = GDN_HEAD_DIM
    nh = GDN_HEADS
    cw = conv_w.reshape(CONV_WIDTH, 3 * nh, hdim).transpose(1, 0, 2)
    alog = jnp.broadcast_to(a_log.reshape(nh, 1, 1), (nh, 1, hdim)).astype(F32)
    dtb = jnp.broadcast_to(dt_bias.reshape(nh, 1, 1), (nh, 1, hdim)).astype(F32)
    full = lambda shape: pl.BlockSpec(shape, lambda b, i: (0,) * len(shape))
    return pl.pallas_call(
        _gdn_kernel,
        grid=(bsz, s // sb),
        in_specs=[pl.BlockSpec((1, sb, cols), lambda b, i: (b, i, 0)),
                  full(cw.shape), full(alog.shape), full(dtb.shape), full((1, hdim))],
        out_specs=pl.BlockSpec((1, sb, nh * hdim), lambda b, i: (b, i, 0)),
        out_shape=jax.ShapeDtypeStruct((bsz, s, nh * hdim), BF16),
        scratch_shapes=[pltpu.VMEM((nh, hdim, hdim), F32),
                        pltpu.VMEM((3 * nh, SUBLANES, hdim), F32),
                        pltpu.VMEM((3 * nh, sb + SUBLANES, hdim), F32),
                        pltpu.VMEM((nh, sb, hdim), F32)],
        compiler_params=_params("parallel", "arbitrary"),
        name="gdn",
    )(gdn_in, cw, alog, dtb, norm_g.reshape(1, hdim))


MLA_QK_PAD = 256


def _mla_prep_kernel(m_ref, cos_ref, sin_ref, gq_ref, gkv_ref, wq_ref, wkv_ref,
                     q_ref, k_ref, v_ref):
    m = m_ref[0]
    cosr = cos_ref[0]
    sinr = sin_ref[0]
    cq = m[:, :Q_LORA_RANK]
    ckv = m[:, Q_LORA_RANK:Q_LORA_RANK + KV_LORA_RANK]
    o = Q_LORA_RANK + KV_LORA_RANK
    kr = m[:, o:o + LANES]
    krs = m[:, o + LANES:o + 2 * LANES]
    scale = (QK_NOPE_DIM + QK_ROPE_DIM) ** -0.5
    qa = _dot((_rms(cq) * gq_ref[...]).astype(BF16), wq_ref[...])
    kva = _dot((_rms(ckv) * gkv_ref[...]).astype(BF16), wkv_ref[...])
    k_rope = (kr * cosr + krs * sinr).astype(k_ref.dtype)
    sw0 = MLA_HEADS * MLA_QK_PAD
    for h in range(MLA_HEADS):
        b0 = h * MLA_QK_PAD
        rope = qa[:, b0 + LANES:b0 + 2 * LANES] * cosr + qa[:, sw0 + h * LANES:sw0 + (h + 1) * LANES] * sinr
        q_ref[0, h, :, 0:LANES] = (qa[:, b0:b0 + LANES] * scale).astype(q_ref.dtype)
        q_ref[0, h, :, LANES:2 * LANES] = (rope * scale).astype(q_ref.dtype)
        c0 = h * (QK_NOPE_DIM + V_HEAD_DIM)
        k_ref[0, h, :, 0:LANES] = kva[:, c0:c0 + QK_NOPE_DIM].astype(k_ref.dtype)
        k_ref[0, h, :, LANES:2 * LANES] = k_rope
        v_ref[0, h] = kva[:, c0 + QK_NOPE_DIM:c0 + QK_NOPE_DIM + V_HEAD_DIM].astype(v_ref.dtype)


def _mla_prep(mla_in, cosr, sinr, gq, gkv, wq, wkv, tm):
    bsz, s, mc = mla_in.shape
    nh = MLA_HEADS
    return pl.pallas_call(
        _mla_prep_kernel,
        grid=(bsz, s // tm),
        in_specs=[pl.BlockSpec((1, tm, mc), lambda b, i: (b, i, 0)),
                  pl.BlockSpec((1, tm, LANES), lambda b, i: (b, i, 0)),
                  pl.BlockSpec((1, tm, LANES), lambda b, i: (b, i, 0)),
                  pl.BlockSpec((1, Q_LORA_RANK), lambda b, i: (0, 0)),
                  pl.BlockSpec((1, KV_LORA_RANK), lambda b, i: (0, 0)),
                  pl.BlockSpec(wq.shape, lambda b, i: (0, 0)),
                  pl.BlockSpec(wkv.shape, lambda b, i: (0, 0))],
        out_specs=[pl.BlockSpec((1, nh, tm, MLA_QK_PAD), lambda b, i: (b, 0, i, 0)),
                   pl.BlockSpec((1, nh, tm, MLA_QK_PAD), lambda b, i: (b, 0, i, 0)),
                   pl.BlockSpec((1, nh, tm, V_HEAD_DIM), lambda b, i: (b, 0, i, 0))],
        out_shape=[jax.ShapeDtypeStruct((bsz, nh, s, MLA_QK_PAD), BF16),
                   jax.ShapeDtypeStruct((bsz, nh, s, MLA_QK_PAD), BF16),
                   jax.ShapeDtypeStruct((bsz, nh, s, V_HEAD_DIM), BF16)],
        compiler_params=_params("parallel", "parallel"),
        name="mla_prep",
    )(mla_in, cosr, sinr, gq, gkv, wq, wkv)


def _fold_lanes(x, op):
    parts = [x[:, c * LANES:(c + 1) * LANES] for c in range(x.shape[1] // LANES)]
    while len(parts) > 1:
        parts = [op(parts[i], parts[i + 1]) for i in range(0, len(parts), 2)]
    return parts[0]


def _attn_kernel(q_ref, k_ref, v_ref, o_ref, s_ref, m_ref, l_ref, acc_ref):
    i = pl.program_id(2)
    tq = q_ref.shape[2]
    m_ref[...] = jnp.full_like(m_ref, -jnp.inf)
    l_ref[...] = jnp.zeros_like(l_ref)
    acc_ref[...] = jnp.zeros_like(acc_ref)

    def scores(j, slot):
        start = pl.multiple_of(j * tq, tq)
        s_ref[slot] = _dot_nt(q_ref[0, 0], k_ref[0, 0, pl.ds(start, tq), :])

    def consume(j, slot, diagonal):
        start = pl.multiple_of(j * tq, tq)
        s = s_ref[slot]
        if diagonal:
            row = lax.broadcasted_iota(jnp.int32, s.shape, 0)
            col = lax.broadcasted_iota(jnp.int32, s.shape, 1)
            s = jnp.where(col <= row, s, -jnp.inf)
        m_prev = m_ref[...]
        m_new = jnp.maximum(m_prev, jnp.max(_fold_lanes(s, jnp.maximum), axis=-1, keepdims=True))
        alpha = jnp.exp(m_prev - m_new)
        p = jnp.exp(s - m_new)
        l_ref[...] = alpha * l_ref[...] + jnp.sum(_fold_lanes(p, jnp.add), axis=-1, keepdims=True)
        acc_ref[...] = alpha * acc_ref[...] + _dot(p.astype(v_ref.dtype),
                                                   v_ref[0, 0, pl.ds(start, tq), :])
        m_ref[...] = m_new

    scores(0, 0)

    def pair(t, carry):
        scores(2 * t + 1, 1)
        consume(2 * t, 0, False)
        scores(2 * t + 2, 0)
        consume(2 * t + 1, 1, False)
        return carry
    lax.fori_loop(0, i // 2, pair, 0)

    @pl.when(i % 2 == 1)
    def _():
        scores(i, 1)
        consume(i - 1, 0, False)
        consume(i, 1, True)

    @pl.when(i % 2 == 0)
    def _():
        consume(i, 0, True)

    o_ref[0] = (acc_ref[...] / l_ref[...]).astype(o_ref.dtype)


def _attn(q, k, v, tq):
    bsz, nh, s, dq = q.shape
    dv = v.shape[-1]
    return pl.pallas_call(
        _attn_kernel,
        grid=(bsz, nh, s // tq),
        in_specs=[pl.BlockSpec((1, 1, tq, dq), lambda b, h, i: (b, h, i, 0)),
                  pl.BlockSpec((1, 1, s, dq), lambda b, h, i: (b, h, 0, 0)),
                  pl.BlockSpec((1, 1, s, dv), lambda b, h, i: (b, h, 0, 0))],
        out_specs=pl.BlockSpec((1, tq, dv), lambda b, h, i: (b, i, h)),
        out_shape=jax.ShapeDtypeStruct((bsz, s, nh * dv), BF16),
        scratch_shapes=[pltpu.VMEM((2, tq, tq), F32),
                        pltpu.VMEM((tq, 1), F32), pltpu.VMEM((tq, 1), F32),
                        pltpu.VMEM((tq, dv), F32)],
        compiler_params=_params("parallel", "parallel", "arbitrary"),
        name="attn",
    )(q, k, v)


def _outproj_kernel(og_ref, om_ref, x_ref, mod_ref, wo_ref, g_ref, wpq_ref,
                    x1_ref, h2_ref, qp_ref):
    m = mod_ref[0]
    gw = og_ref.shape[-1]
    mixed = _dot(og_ref[0], wo_ref[0:gw, :]) + _dot(om_ref[0], wo_ref[gw:, :])
    x1 = x_ref[0] + m[2:3] * mixed
    h2 = _rms(x1) * g_ref[...] * (1.0 + m[4:5]) + m[3:4]
    x1_ref[0] = x1
    h2_ref[0] = h2.astype(h2_ref.dtype)
    qp_ref[0] = _dot(h2.astype(BF16), wpq_ref[...]).astype(qp_ref.dtype)


def _out_proj(o_gdn, o_mla, x, mod, w_out, g, w_pq, tm):
    bsz, s, d = x.shape
    gw = o_gdn.shape[-1]
    mw = o_mla.shape[-1]
    nq = w_pq.shape[1]
    return pl.pallas_call(
        _outproj_kernel,
        grid=(bsz, s // tm),
        in_specs=[pl.BlockSpec((1, tm, gw), lambda b, i: (b, i, 0)),
                  pl.BlockSpec((1, tm, mw), lambda b, i: (b, i, 0)),
                  pl.BlockSpec((1, tm, d), lambda b, i: (b, i, 0)),
                  pl.BlockSpec((1, N_MOD, d), lambda b, i: (b, 0, 0)),
                  pl.BlockSpec((gw + mw, d), lambda b, i: (0, 0)),
                  pl.BlockSpec((1, d), lambda b, i: (0, 0)),
                  pl.BlockSpec((d, nq), lambda b, i: (0, 0))],
        out_specs=[pl.BlockSpec((1, tm, d), lambda b, i: (b, i, 0)),
                   pl.BlockSpec((1, tm, d), lambda b, i: (b, i, 0)),
                   pl.BlockSpec((1, tm, nq), lambda b, i: (b, i, 0))],
        out_shape=[jax.ShapeDtypeStruct((bsz, s, d), F32),
                   jax.ShapeDtypeStruct((bsz, s, d), BF16),
                   jax.ShapeDtypeStruct((bsz, s, nq), BF16)],
        compiler_params=_params("parallel", "parallel"),
        name="out_proj",
    )(o_gdn, o_mla, x, mod, w_out, g, w_pq)


PEER_CAND_ROWS = 2 * SUBLANES + (SUBLANES - 1) * SUBLANES + SUBLANES


def _peer_topk_kernel(qp_ref, keys_ref, idx_ref, gate_ref,
                      stop_ref, itop_ref, cand_ref, cidx_ref, best_ref, idxt_ref, gatet_ref):
    tb = qp_ref.shape[0]
    kk = PEER_TOPK
    assert kk == 2 * SUBLANES

    def extract(vals, row, n):
        m = jnp.max(vals, axis=0, keepdims=True)
        pos = jnp.min(jnp.where(vals == m, row, n), axis=0, keepdims=True)
        return m, pos, jnp.where(row == pos, -jnp.inf, vals)

    row_k = lax.broadcasted_iota(jnp.int32, (N_KEYS, tb), 0)
    row_c = lax.broadcasted_iota(jnp.int32, (PEER_CAND_ROWS, tb), 0)
    row_8 = lax.broadcasted_iota(jnp.int32, (SUBLANES, tb), 0)
    for h in range(PEER_HEADS):
        for p in range(2):
            c0 = (2 * h + p) * N_KEYS
            vals = _dot_nt(keys_ref[p, h], qp_ref[:, c0:c0 + N_KEYS])
            for r in range(kk):
                m, pos, vals = extract(vals, row_k, N_KEYS)
                stop_ref[p, r:r + 1, :] = m
                itop_ref[p, r:r + 1, :] = pos
        cand_ref[0:kk, :] = stop_ref[0, 0:1, :] + stop_ref[1]
        cidx_ref[0:kk, :] = itop_ref[0, 0:1, :] * N_KEYS + itop_ref[1]
        s2 = stop_ref[1, 0:SUBLANES, :]
        i2 = itop_ref[1, 0:SUBLANES, :]
        for a in range(1, SUBLANES):
            r0 = kk + (a - 1) * SUBLANES
            cand_ref[r0:r0 + SUBLANES, :] = jnp.where(
                row_8 < kk // (a + 1), stop_ref[0, a:a + 1, :] + s2, -jnp.inf)
            cidx_ref[r0:r0 + SUBLANES, :] = itop_ref[0, a:a + 1, :] * N_KEYS + i2
        r0 = kk + (SUBLANES - 1) * SUBLANES
        cand_ref[r0:r0 + SUBLANES, :] = stop_ref[0, SUBLANES:kk, :] + stop_ref[1, 0:1, :]
        cidx_ref[r0:r0 + SUBLANES, :] = itop_ref[0, SUBLANES:kk, :] * N_KEYS + itop_ref[1, 0:1, :]
        vals = cand_ref[...]
        cidx = cidx_ref[...]
        for r in range(kk):
            m, pos, vals = extract(vals, row_c, PEER_CAND_ROWS)
            best_ref[r:r + 1, :] = m
            idxt_ref[h * kk + r:h * kk + r + 1, :] = jnp.max(
                jnp.where(row_c == pos, cidx, -1), axis=0, keepdims=True)
        best = best_ref[...]
        e = jnp.exp(best - best[0:1, :])
        gatet_ref[h * kk:(h + 1) * kk, :] = e / jnp.sum(e, axis=0, keepdims=True)
    idx_ref[...] = idxt_ref[...].T
    gate_ref[...] = gatet_ref[...].T


def _peer_topk(qp, keys, tb, part, nparts):
    nq = qp.shape[1]
    t = qp.shape[0] // nparts
    blk0 = part * (t // tb)
    hk = PEER_HEADS * PEER_TOPK
    kk = PEER_TOPK
    return pl.pallas_call(
        _peer_topk_kernel,
        grid=(t // tb,),
        in_specs=[pl.BlockSpec((tb, nq), lambda i: (i + blk0, 0)),
                  pl.BlockSpec(keys.shape, lambda i: (0, 0, 0, 0))],
        out_specs=[pl.BlockSpec((tb, hk), lambda i: (i, 0)),
                   pl.BlockSpec((tb, hk), lambda i: (i, 0))],
        out_shape=[jax.ShapeDtypeStruct((t, hk), jnp.int32),
                   jax.ShapeDtypeStruct((t, hk), F32)],
        scratch_shapes=[pltpu.VMEM((2, kk, tb), F32), pltpu.VMEM((2, kk, tb), jnp.int32),
                        pltpu.VMEM((PEER_CAND_ROWS, tb), F32),
                        pltpu.VMEM((PEER_CAND_ROWS, tb), jnp.int32),
                        pltpu.VMEM((kk, tb), F32),
                        pltpu.VMEM((hk, tb), jnp.int32), pltpu.VMEM((hk, tb), F32)],
        compiler_params=_params("parallel"),
        name="peer_topk",
    )(qp, keys)


PEER_TOKEN_GROUP = 16
PEER_PARTS = 4


def _peer_gate_matrix(idx, gate, n_experts):
    t_total, hk = idx.shape
    ln = SC_LANES
    nu = hk // ln
    grp = PEER_TOKEN_GROUP
    tok_per_w = t_total // SC_WORKERS
    n_groups = tok_per_w // grp
    assert n_groups * grp * SC_WORKERS == t_total and grp % 2 == 0
    mesh = plsc.VectorSubcoreMesh(core_axis_name="c", subcore_axis_name="s",
                                  num_cores=SC_CORES, num_subcores=SC_SUBCORES)

    @functools.partial(
        pl.kernel, mesh=mesh,
        compiler_params=pltpu.CompilerParams(needs_layout_passes=False),
        out_type=jax.ShapeDtypeStruct((t_total, n_experts), F32),
        scratch_types=[
            pltpu.VMEM((grp * hk,), jnp.int32),
            pltpu.VMEM((grp * hk,), F32),
            pltpu.VMEM((n_experts,), F32),
            pltpu.VMEM((n_experts,), F32),
            pltpu.SemaphoreType.DMA((2,)),
        ],
    )
    def gate_kernel(idx_hbm, gate_hbm, g_hbm, idx_v, gate_v, row0_v, row1_v, sem):
        rows = (row0_v, row1_v)
        wid = lax.axis_index("s") * SC_CORES + lax.axis_index("c")
        base = wid * tok_per_w
        zero = jnp.zeros((ln,), F32)

        def zero_body(c, carry):
            row0_v[pl.ds(c * ln, ln)] = zero
            row1_v[pl.ds(c * ln, ln)] = zero
            return carry
        lax.fori_loop(0, n_experts // ln, zero_body, 0)

        def out_copy(tok, slot):
            return pltpu.make_async_copy(rows[slot], g_hbm.at[tok], sem.at[slot])

        def group_body(g, carry):
            tok0 = base + g * grp
            pltpu.sync_copy(idx_hbm.at[pl.ds(tok0 * hk, grp * hk)], idx_v)
            pltpu.sync_copy(gate_hbm.at[pl.ds(tok0 * hk, grp * hk)], gate_v)

            def pair_body(i, carry2):
                for slot in range(2):
                    t = i * 2 + slot
                    for u in range(nu):
                        sl = pl.ds(t * hk + u * ln, ln)
                        plsc.addupdate_scatter(rows[slot], [idx_v[sl]], gate_v[sl])
                    out_copy(tok0 + t, slot).start()
                for slot in range(2):
                    t = i * 2 + slot
                    out_copy(tok0 + t, slot).wait()
                    for u in range(nu):
                        plsc.store_scatter(rows[slot], [idx_v[pl.ds(t * hk + u * ln, ln)]], zero)
                return carry2
            lax.fori_loop(0, grp // 2, pair_body, 0)
            return carry
        lax.fori_loop(0, n_groups, group_body, 0)

    return gate_kernel(idx.reshape(-1), gate.reshape(-1))


def _peer_dense_kernel(h_ref, g_ref, u_ref, v_ref, o_ref):
    e = pl.program_id(1)
    s = _dot_nt(h_ref[...], u_ref[...])
    p = (jax.nn.gelu(s) * g_ref[...]).astype(v_ref.dtype)
    contrib = _dot(p, v_ref[...])

    @pl.when(e == 0)
    def _():
        o_ref[...] = contrib

    @pl.when(e > 0)
    def _():
        o_ref[...] += contrib


def _peer_dense(h, gmat, u, v, tb, eb, part):
    t = gmat.shape[0]
    d = h.shape[1]
    n_e = u.shape[0]
    blk0 = part * (t // tb)
    return pl.pallas_call(
        _peer_dense_kernel,
        grid=(t // tb, n_e // eb),
        in_specs=[pl.BlockSpec((tb, d), lambda i, e: (i + blk0, 0)),
                  pl.BlockSpec((tb, eb), lambda i, e: (i, e)),
                  pl.BlockSpec((eb, d), lambda i, e: (e, 0)),
                  pl.BlockSpec((eb, d), lambda i, e: (e, 0))],
        out_specs=pl.BlockSpec((tb, d), lambda i, e: (i, 0)),
        out_shape=jax.ShapeDtypeStruct((t, d), F32),
        compiler_params=_params("parallel", "arbitrary"),
        name="peer_dense",
    )(h, gmat, u, v)


def _final_kernel(x1_ref, y_ref, mod_ref, g_ref, *rest, normalize):
    o_ref = rest[-1]
    m = mod_ref[0]
    x2 = x1_ref[0] + m[5:6] * y_ref[0]
    o_ref[0] = _rms(x2) * g_ref[...] if normalize else x2


def _final(x1, y, mod, g, tm, normalize, part, prev):
    bsz, s, d = x1.shape
    pb = y.shape[0]
    b0 = part * pb
    full_blk = pl.BlockSpec((1, tm, d), lambda b, i: (b + b0, i, 0))
    in_specs = [full_blk, pl.BlockSpec((1, tm, d), lambda b, i: (b, i, 0)),
                pl.BlockSpec((1, N_MOD, d), lambda b, i: (b + b0, 0, 0)),
                pl.BlockSpec((1, d), lambda b, i: (0, 0))]
    args = [x1, y, mod, g]
    aliases = {}
    if prev is not None:
        in_specs.append(pl.BlockSpec(memory_space=pl.ANY))
        args.append(prev)
        aliases = {4: 0}
    return pl.pallas_call(
        functools.partial(_final_kernel, normalize=normalize),
        grid=(pb, s // tm),
        in_specs=in_specs,
        out_specs=full_blk,
        out_shape=jax.ShapeDtypeStruct((bsz, s, d), F32),
        input_output_aliases=aliases,
        compiler_params=_params("parallel", "parallel"),
        name="final",
    )(*args)


def _split_w_in(w):
    gw = GDN_HEADS * GDN_HEAD_DIM
    sizes = [gw] * 4 + [GDN_HEADS] * 2 + [Q_LORA_RANK, KV_LORA_RANK, QK_ROPE_DIM]
    offs = [0]
    for sz in sizes:
        offs.append(offs[-1] + sz)
    parts = [w[:, offs[i]:offs[i + 1]] for i in range(len(sizes))]
    gq, gk, gv, gz, ga, gb, cq, ckv, kr = parts
    d = w.shape[0]
    half = QK_ROPE_DIM // 2
    zeros = lambda n: jnp.zeros((d, n), w.dtype)
    gdn = jnp.concatenate([gq, gk, gv, gz, ga, gb, zeros(LANES - 2 * GDN_HEADS)], axis=1)
    kr_sw = jnp.concatenate([kr[:, half:], kr[:, :half]], axis=1)
    mla = jnp.concatenate([cq, ckv, kr, zeros(LANES - QK_ROPE_DIM),
                           kr_sw, zeros(LANES - QK_ROPE_DIM)], axis=1)
    return jnp.concatenate([gdn, mla], axis=1).astype(BF16), gdn.shape[1], mla.shape[1]


def _split_w_uq(w):
    r = w.shape[0]
    half = QK_ROPE_DIM // 2
    hd = QK_NOPE_DIM + QK_ROPE_DIM
    main, swapped = [], []
    for h in range(MLA_HEADS):
        nope = w[:, h * hd:h * hd + QK_NOPE_DIM]
        rope = w[:, h * hd + QK_NOPE_DIM:(h + 1) * hd]
        main += [nope, rope, jnp.zeros((r, MLA_QK_PAD - hd), w.dtype)]
        swapped += [rope[:, half:], rope[:, :half], jnp.zeros((r, LANES - QK_ROPE_DIM), w.dtype)]
    return jnp.concatenate(main + swapped, axis=1).astype(BF16)


def _rotary_tables(positions):
    half = QK_ROPE_DIM // 2
    inv_freq = ROPE_THETA ** (-jnp.arange(half, dtype=F32) / half)
    ang = positions.astype(F32)[..., None] * inv_freq
    cos, sin = jnp.cos(ang), jnp.sin(ang)
    z = jnp.zeros(cos.shape[:-1] + (LANES - QK_ROPE_DIM,), F32)
    return (jnp.concatenate([cos, cos, z], axis=-1),
            jnp.concatenate([-sin, sin, z], axis=-1))


def _block(n, pref):
    return pref if n % pref == 0 else n


def kernel(x, c, positions, ln_mix_g, w_in, conv_w, a_log, dt_bias, gdn_norm_g, q_norm_g, w_uq, kv_norm_g, w_ukv, w_out, ln_ffn_g, w_pq, sub_keys, expert_u, expert_v, w_ada, b_ada, final_norm_g):
    bsz, s, d = x.shape
    depth = w_in.shape[0]
    cosr, sinr = _rotary_tables(positions)
    tm = _block(s, 512)
    for layer in range(depth):
        mod = _mod(c, w_ada[layer], b_ada[layer]).reshape(bsz, N_MOD, d)
        w_cat, gdn_cols, mla_cols = _split_w_in(w_in[layer])
        gdn_in, mla_in = _in_proj(x, mod, ln_mix_g[layer].reshape(1, d), w_cat, gdn_cols, mla_cols, tm)
        o_gdn = _gdn(gdn_in, conv_w[layer], a_log[layer], dt_bias[layer], gdn_norm_g[layer],
                     _block(s, 256))
        q, k, v = _mla_prep(mla_in, cosr, sinr, q_norm_g[layer].reshape(1, -1),
                            kv_norm_g[layer].reshape(1, -1), _split_w_uq(w_uq[layer]),
                            w_ukv[layer].astype(BF16), _block(s, 512))
        o_mla = _attn(q, k, v, _block(s, 1024))
        x1, h2, qp = _out_proj(o_gdn, o_mla, x, mod, w_out[layer].astype(BF16),
                               ln_ffn_g[layer].reshape(1, d), w_pq[layer].astype(BF16), tm)
        n_experts = expert_u.shape[1]
        u16 = expert_u[layer].astype(BF16)
        v16 = expert_v[layer].astype(BF16)
        qp2 = qp.reshape(bsz * s, -1)
        h22 = h2.reshape(bsz * s, d)
        nparts = PEER_PARTS if bsz % PEER_PARTS == 0 else 1
        tpart = bsz * s // nparts
        last = layer + 1 == depth
        out = None
        for part in range(nparts):
            idx, gate = _peer_topk(qp2, sub_keys[layer].astype(BF16), _block(tpart, 256), part, nparts)
            gmat = _peer_gate_matrix(idx, gate, n_experts)
            y = _peer_dense(h22, gmat, u16, v16, _block(tpart, 1024), _block(n_experts, 1024), part)
            out = _final(x1, y.reshape(bsz // nparts, s, d), mod, final_norm_g.reshape(1, d), tm,
                         last, part, out)
        x = out
    return x
```

```python
import functools

import jax
import jax.numpy as jnp
from jax import lax
from jax.experimental import pallas as pl
from jax.experimental.pallas import tpu as pltpu
from jax.experimental.pallas import tpu_sc as plsc

GDN_HEADS = 4
GDN_HEAD_DIM = 128
CONV_WIDTH = 4
CHUNK = 64
MLA_HEADS = 4
QK_NOPE_DIM = 128
QK_ROPE_DIM = 64
V_HEAD_DIM = 128
Q_LORA_RANK = 384
KV_LORA_RANK = 256
ROPE_THETA = 10000.0
PEER_HEADS = 8
N_KEYS = 128
PEER_TOPK = 16
N_MOD = 6
EPS = 1e-6

LANES = 128
SUBLANES = 8
SC_CORES = 2
SC_SUBCORES = 16
SC_LANES = 16
SC_WORKERS = SC_CORES * SC_SUBCORES
VMEM_LIMIT = 48 * 1024 * 1024

F32 = jnp.float32
BF16 = jnp.bfloat16
HI = lax.Precision.HIGHEST


def _dot(a, b, precision=None):
    return jnp.dot(a, b, preferred_element_type=F32, precision=precision)


def _dot_nt(a, b, precision=None):
    return lax.dot_general(a, b, (((1,), (1,)), ((), ())),
                           preferred_element_type=F32, precision=precision)


def _dot_tn(a, b, precision=None):
    return lax.dot_general(a, b, (((0,), (0,)), ((), ())),
                           preferred_element_type=F32, precision=precision)


def _rms(x):
    return x * lax.rsqrt(jnp.mean(x * x, axis=-1, keepdims=True) + EPS)


def _silu(x):
    return x * jax.nn.sigmoid(x)


def _params(*sem):
    return pltpu.CompilerParams(dimension_semantics=sem, vmem_limit_bytes=VMEM_LIMIT)


def _mod_kernel(c_ref, w_ref, b_ref, o_ref):
    o_ref[...] = _dot(_silu(c_ref[...]), w_ref[...], HI) + b_ref[...]


def _mod(c, w_ada, b_ada):
    bsz, d = c.shape
    n = w_ada.shape[1]
    return pl.pallas_call(
        _mod_kernel,
        grid=(n // d,),
        in_specs=[pl.BlockSpec((bsz, d), lambda j: (0, 0)),
                  pl.BlockSpec((d, d), lambda j: (0, j)),
                  pl.BlockSpec((1, d), lambda j: (0, j))],
        out_specs=pl.BlockSpec((bsz, d), lambda j: (0, j)),
        out_shape=jax.ShapeDtypeStruct((bsz, n), F32),
        compiler_params=_params("arbitrary"),
        name="mod",
    )(c, w_ada, b_ada.reshape(1, n))


def _inproj_kernel(x_ref, mod_ref, g_ref, w_ref, gdn_ref, mla_ref):
    m = mod_ref[0]
    h = _rms(x_ref[0]) * g_ref[...] * (1.0 + m[1:2]) + m[0:1]
    p = _dot(h.astype(BF16), w_ref[...])
    gw = gdn_ref.shape[-1]
    gdn_ref[0] = p[:, :gw]
    mla_ref[0] = p[:, gw:]


def _in_proj(x, mod, g, w_cat, gdn_cols, mla_cols, tm):
    bsz, s, d = x.shape
    return pl.pallas_call(
        _inproj_kernel,
        grid=(bsz, s // tm),
        in_specs=[pl.BlockSpec((1, tm, d), lambda b, i: (b, i, 0)),
                  pl.BlockSpec((1, N_MOD, d), lambda b, i: (b, 0, 0)),
                  pl.BlockSpec((1, d), lambda b, i: (0, 0)),
                  pl.BlockSpec((d, gdn_cols + mla_cols), lambda b, i: (0, 0))],
        out_specs=[pl.BlockSpec((1, tm, gdn_cols), lambda b, i: (b, i, 0)),
                   pl.BlockSpec((1, tm, mla_cols), lambda b, i: (b, i, 0))],
        out_shape=[jax.ShapeDtypeStruct((bsz, s, gdn_cols), F32),
                   jax.ShapeDtypeStruct((bsz, s, mla_cols), F32)],
        compiler_params=_params("parallel", "parallel"),
        name="in_proj",
    )(x, mod, g, w_cat)


def _gdn_kernel(x_ref, cw_ref, alog_ref, dtb_ref, ng_ref, o_ref,
                state_ref, tail_ref, buf_ref, vnew_ref):
    sb = x_ref.shape[1]
    nchunk = sb // CHUNK
    hdim = GDN_HEAD_DIM
    gw = GDN_HEADS * hdim

    @pl.when(pl.program_id(1) == 0)
    def _():
        state_ref[...] = jnp.zeros_like(state_ref)
        tail_ref[...] = jnp.zeros_like(tail_ref)

    def conv_silu(slot):
        x = x_ref[0, :, slot * hdim:(slot + 1) * hdim]
        buf_ref[slot, 0:SUBLANES, :] = tail_ref[slot]
        buf_ref[slot, SUBLANES:SUBLANES + sb, :] = x
        tail_ref[slot] = x[sb - SUBLANES:sb, :]
        cw = cw_ref[slot]
        y = cw[CONV_WIDTH - 1:CONV_WIDTH] * x
        for j in range(CONV_WIDTH - 1):
            off = SUBLANES - (CONV_WIDTH - 1) + j
            y = y + cw[j:j + 1] * buf_ref[slot, off:off + sb, :]
        return _silu(y)

    def l2n(x):
        return x * lax.rsqrt(jnp.sum(x * x, axis=-1, keepdims=True) + EPS)

    ri = lax.broadcasted_iota(jnp.int32, (sb, sb), 0)
    ci = lax.broadcasted_iota(jnp.int32, (sb, sb), 1)
    same = (ri // CHUNK) == (ci // CHUNK)
    incl = same & (ci <= ri)
    strict = same & (ci < ri)
    eye = (ri == ci).astype(F32)
    levels = []
    bs = 1
    while bs < CHUNK:
        levels.append(((ri // bs) % 2 == 1) & ((ci // bs) % 2 == 0)
                      & ((ri // (2 * bs)) == (ci // (2 * bs))))
        bs *= 2
    rowmod = lax.broadcasted_iota(jnp.int32, (sb, hdim), 0) % CHUNK
    ab = x_ref[0, :, 4 * gw:4 * gw + LANES]

    heads = range(GDN_HEADS)
    q = [l2n(conv_silu(h)) * (hdim ** -0.5) for h in heads]
    k = [l2n(conv_silu(GDN_HEADS + h)) for h in heads]
    v = [conv_silu(2 * GDN_HEADS + h) for h in heads]
    beta = [jax.nn.sigmoid(ab[:, GDN_HEADS + h:GDN_HEADS + h + 1]) for h in heads]

    gc = [-jnp.exp(alog_ref[h]) * jax.nn.softplus(ab[:, h:h + 1] + dtb_ref[h]) for h in heads]
    sh = 1
    while sh < CHUNK:
        gc = [g + jnp.where(rowmod >= sh, pltpu.roll(g, sh, axis=0), 0.0) for g in gc]
        sh *= 2
    gcl = [jnp.concatenate(
        [jnp.broadcast_to(g[(n + 1) * CHUNK - 1:(n + 1) * CHUNK, :], (CHUNK, hdim))
         for n in range(nchunk)], axis=0) for g in gc]
    decay = []
    for g in gc:
        gc_row = jnp.broadcast_to(g.T[0:1, :], (sb, sb))
        decay.append(jnp.where(incl, jnp.exp(jnp.where(incl, g[:, 0:1] - gc_row, 0.0)), 0.0))

    kb = [k[h] * beta[h] for h in heads]
    k16 = [k[h].astype(BF16) for h in heads]
    mmat = [jnp.where(strict, _dot_nt(kb[h].astype(BF16), k16[h]) * decay[h], 0.0) for h in heads]
    attn = [(_dot_nt(q[h].astype(BF16), k16[h]) * decay[h]).astype(BF16) for h in heads]
    tinv = [eye - jnp.where(levels[0], m, 0.0) for m in mmat]
    for off in levels[1:]:
        t16 = [t.astype(BF16) for t in tinv]
        ta = [_dot(t16[h], jnp.where(off, mmat[h], 0.0).astype(BF16)).astype(BF16) for h in heads]
        tinv = [tinv[h] - _dot(ta[h], t16[h]) for h in heads]

    egc = [jnp.exp(g) for g in gc]
    rhs = [jnp.concatenate([v[h] * beta[h], kb[h] * egc[h]], axis=1) for h in heads]
    sol = [rhs[h] + _dot((tinv[h] - eye).astype(BF16), rhs[h].astype(BF16)) for h in heads]
    u = [x[:, :hdim] for x in sol]
    w16 = [x[:, hdim:].astype(BF16) for x in sol]
    qd16 = [(q[h] * egc[h]).astype(BF16) for h in heads]
    kd16 = [(k[h] * jnp.exp(gcl[h] - gc[h])).astype(BF16) for h in heads]
    g_last = [jnp.exp(x) for x in gcl]

    st = [state_ref[h] for h in heads]
    o_inter = [[] for _ in heads]
    for n in range(nchunk):
        lo = n * CHUNK
        for h in heads:
            st16 = st[h].astype(BF16)
            v_new = u[h][lo:lo + CHUNK] - _dot(w16[h][lo:lo + CHUNK], st16)
            vnew_ref[h, lo:lo + CHUNK, :] = v_new
            o_inter[h].append(_dot(qd16[h][lo:lo + CHUNK], st16))
            st[h] = st[h] * g_last[h][lo:lo + 1, :] + _dot_tn(kd16[h][lo:lo + CHUNK],
                                                             v_new.astype(BF16))
    for h in heads:
        state_ref[h] = st[h]
        o = jnp.concatenate(o_inter[h], axis=0) + _dot(attn[h], vnew_ref[h].astype(BF16))
        z = x_ref[0, :, 3 * gw + h * hdim:3 * gw + (h + 1) * hdim]
        o = _rms(o) * ng_ref[...] * _silu(z)
        o_ref[0, :, h * hdim:(h + 1) * hdim] = o.astype(o_ref.dtype)


def _gdn(gdn_in, conv_w, a_log, dt_bias, norm_g, sb):
    bsz, s, cols = gdn_in.shape
    hdim = GDN_HEAD_DIM
    nh = GDN_HEADS
    cw = conv_w.reshape(CONV_WIDTH, 3 * nh, hdim).transpose(1, 0, 2)
    alog = jnp.broadcast_to(a_log.reshape(nh, 1, 1), (nh, 1, hdim)).astype(F32)
    dtb = jnp.broadcast_to(dt_bias.reshape(nh, 1, 1), (nh, 1, hdim)).astype(F32)
    full = lambda shape: pl.BlockSpec(shape, lambda b, i: (0,) * len(shape))
    return pl.pallas_call(
        _gdn_kernel,
        grid=(bsz, s // sb),
        in_specs=[pl.BlockSpec((1, sb, cols), lambda b, i: (b, i, 0)),
                  full(cw.shape), full(alog.shape), full(dtb.shape), full((1, hdim))],
        out_specs=pl.BlockSpec((1, sb, nh * hdim), lambda b, i: (b, i, 0)),
        out_shape=jax.ShapeDtypeStruct((bsz, s, nh * hdim), BF16),
        scratch_shapes=[pltpu.VMEM((nh, hdim, hdim), F32),
                        pltpu.VMEM((3 * nh, SUBLANES, hdim), F32),
                        pltpu.VMEM((3 * nh, sb + SUBLANES, hdim), F32),
                        pltpu.VMEM((nh, sb, hdim), F32)],
        compiler_params=_params("parallel", "arbitrary"),
        name="gdn",
    )(gdn_in, cw, alog, dtb, norm_g.reshape(1, hdim))


MLA_QK_PAD = 256


def _mla_prep_kernel(m_ref, cos_ref, sin_ref, gq_ref, gkv_ref, wq_ref, wkv_ref,
                     q_ref, k_ref, v_ref):
    m = m_ref[0]
    cosr = cos_ref[0]
    sinr = sin_ref[0]
    cq = m[:, :Q_LORA_RANK]
    ckv = m[:, Q_LORA_RANK:Q_LORA_RANK + KV_LORA_RANK]
    o = Q_LORA_RANK + KV_LORA_RANK
    kr = m[:, o:o + LANES]
    krs = m[:, o + LANES:o + 2 * LANES]
    scale = (QK_NOPE_DIM + QK_ROPE_DIM) ** -0.5
    qa = _dot((_rms(cq) * gq_ref[...]).astype(BF16), wq_ref[...])
    kva = _dot((_rms(ckv) * gkv_ref[...]).astype(BF16), wkv_ref[...])
    k_rope = (kr * cosr + krs * sinr).astype(k_ref.dtype)
    sw0 = MLA_HEADS * MLA_QK_PAD
    for h in range(MLA_HEADS):
        b0 = h * MLA_QK_PAD
        rope = qa[:, b0 + LANES:b0 + 2 * LANES] * cosr + qa[:, sw0 + h * LANES:sw0 + (h + 1) * LANES] * sinr
        q_ref[0, h, :, 0:LANES] = (qa[:, b0:b0 + LANES] * scale).astype(q_ref.dtype)
        q_ref[0, h, :, LANES:2 * LANES] = (rope * scale).astype(q_ref.dtype)
        c0 = h * (QK_NOPE_DIM + V_HEAD_DIM)
        k_ref[0, h, :, 0:LANES] = kva[:, c0:c0 + QK_NOPE_DIM].astype(k_ref.dtype)
        k_ref[0, h, :, LANES:2 * LANES] = k_rope
        v_ref[0, h] = kva[:, c0 + QK_NOPE_DIM:c0 + QK_NOPE_DIM + V_HEAD_DIM].astype(v_ref.dtype)


def _mla_prep(mla_in, cosr, sinr, gq, gkv, wq, wkv, tm):
    bsz, s, mc = mla_in.shape
    nh = MLA_HEADS
    return pl.pallas_call(
        _mla_prep_kernel,
        grid=(bsz, s // tm),
        in_specs=[pl.BlockSpec((1, tm, mc), lambda b, i: (b, i, 0)),
                  pl.BlockSpec((1, tm, LANES), lambda b, i: (b, i, 0)),
                  pl.BlockSpec((1, tm, LANES), lambda b, i: (b, i, 0)),
                  pl.BlockSpec((1, Q_LORA_RANK), lambda b, i: (0, 0)),
                  pl.BlockSpec((1, KV_LORA_RANK), lambda b, i: (0, 0)),
                  pl.BlockSpec(wq.shape, lambda b, i: (0, 0)),
                  pl.BlockSpec(wkv.shape, lambda b, i: (0, 0))],
        out_specs=[pl.BlockSpec((1, nh, tm, MLA_QK_PAD), lambda b, i: (b, 0, i, 0)),
                   pl.BlockSpec((1, nh, tm, MLA_QK_PAD), lambda b, i: (b, 0, i, 0)),
                   pl.BlockSpec((1, nh, tm, V_HEAD_DIM), lambda b, i: (b, 0, i, 0))],
        out_shape=[jax.ShapeDtypeStruct((bsz, nh, s, MLA_QK_PAD), BF16),
                   jax.ShapeDtypeStruct((bsz, nh, s, MLA_QK_PAD), BF16),
                   jax.ShapeDtypeStruct((bsz, nh, s, V_HEAD_DIM), BF16)],
        compiler_params=_params("parallel", "parallel"),
        name="mla_prep",
    )(mla_in, cosr, sinr, gq, gkv, wq, wkv)


def _fold_lanes(x, op):
    parts = [x[:, c * LANES:(c + 1) * LANES] for c in range(x.shape[1] // LANES)]
    while len(parts) > 1:
        parts = [op(parts[i], parts[i + 1]) for i in range(0, len(parts), 2)]
    return parts[0]


def _attn_kernel(q_ref, k_ref, v_ref, o_ref, s_ref, m_ref, l_ref, acc_ref):
    i = pl.program_id(2)
    tq = q_ref.shape[2]
    m_ref[...] = jnp.full_like(m_ref, -jnp.inf)
    l_ref[...] = jnp.zeros_like(l_ref)
    acc_ref[...] = jnp.zeros_like(acc_ref)

    def scores(j, slot):
        start = pl.multiple_of(j * tq, tq)
        s_ref[slot] = _dot_nt(q_ref[0, 0], k_ref[0, 0, pl.ds(start, tq), :])

    def consume(j, slot, diagonal):
        start = pl.multiple_of(j * tq, tq)
        s = s_ref[slot]
        if diagonal:
            row = lax.broadcasted_iota(jnp.int32, s.shape, 0)
            col = lax.broadcasted_iota(jnp.int32, s.shape, 1)
            s = jnp.where(col <= row, s, -jnp.inf)
        m_prev = m_ref[...]
        m_new = jnp.maximum(m_prev, jnp.max(_fold_lanes(s, jnp.maximum), axis=-1, keepdims=True))
        alpha = jnp.exp(m_prev - m_new)
        p = jnp.exp(s - m_new)
        l_ref[...] = alpha * l_ref[...] + jnp.sum(_fold_lanes(p, jnp.add), axis=-1, keepdims=True)
        acc_ref[...] = alpha * acc_ref[...] + _dot(p.astype(v_ref.dtype),
                                                   v_ref[0, 0, pl.ds(start, tq), :])
        m_ref[...] = m_new

    scores(0, 0)

    def pair(t, carry):
        scores(2 * t + 1, 1)
        consume(2 * t, 0, False)
        scores(2 * t + 2, 0)
        consume(2 * t + 1, 1, False)
        return carry
    lax.fori_loop(0, i // 2, pair, 0)

    @pl.when(i % 2 == 1)
    def _():
        scores(i, 1)
        consume(i - 1, 0, False)
        consume(i, 1, True)

    @pl.when(i % 2 == 0)
    def _():
        consume(i, 0, True)

    o_ref[0] = (acc_ref[...] / l_ref[...]).astype(o_ref.dtype)


def _attn(q, k, v, tq):
    bsz, nh, s, dq = q.shape
    dv = v.shape[-1]
    return pl.pallas_call(
        _attn_kernel,
        grid=(bsz, nh, s // tq),
        in_specs=[pl.BlockSpec((1, 1, tq, dq), lambda b, h, i: (b, h, i, 0)),
                  pl.BlockSpec((1, 1, s, dq), lambda b, h, i: (b, h, 0, 0)),
                  pl.BlockSpec((1, 1, s, dv), lambda b, h, i: (b, h, 0, 0))],
        out_specs=pl.BlockSpec((1, tq, dv), lambda b, h, i: (b, i, h)),
        out_shape=jax.ShapeDtypeStruct((bsz, s, nh * dv), BF16),
        scratch_shapes=[pltpu.VMEM((2, tq, tq), F32),
                        pltpu.VMEM((tq, 1), F32), pltpu.VMEM((tq, 1), F32),
                        pltpu.VMEM((tq, dv), F32)],
        compiler_params=_params("parallel", "parallel", "arbitrary"),
        name="attn",
    )(q, k, v)


def _outproj_kernel(og_ref, om_ref, x_ref, mod_ref, wo_ref, g_ref, wpq_ref,
                    x1_ref, h2_ref, qp_ref):
    m = mod_ref[0]
    gw = og_ref.shape[-1]
    mixed = _dot(og_ref[0], wo_ref[0:gw, :]) + _dot(om_ref[0], wo_ref[gw:, :])
    x1 = x_ref[0] + m[2:3] * mixed
    h2 = _rms(x1) * g_ref[...] * (1.0 + m[4:5]) + m[3:4]
    x1_ref[0] = x1
    h2_ref[0] = h2.astype(h2_ref.dtype)
    qp_ref[0] = _dot(h2.astype(BF16), wpq_ref[...]).astype(qp_ref.dtype)


def _out_proj(o_gdn, o_mla, x, mod, w_out, g, w_pq, tm):
    bsz, s, d = x.shape
    gw = o_gdn.shape[-1]
    mw = o_mla.shape[-1]
    nq = w_pq.shape[1]
    return pl.pallas_call(
        _outproj_kernel,
        grid=(bsz, s // tm),
        in_specs=[pl.BlockSpec((1, tm, gw), lambda b, i: (b, i, 0)),
                  pl.BlockSpec((1, tm, mw), lambda b, i: (b, i, 0)),
                  pl.BlockSpec((1, tm, d), lambda b, i: (b, i, 0)),
                  pl.BlockSpec((1, N_MOD, d), lambda b, i: (b, 0, 0)),
                  pl.BlockSpec((gw + mw, d), lambda b, i: (0, 0)),
                  pl.BlockSpec((1, d), lambda b, i: (0, 0)),
                  pl.BlockSpec((d, nq), lambda b, i: (0, 0))],
        out_specs=[pl.BlockSpec((1, tm, d), lambda b, i: (b, i, 0)),
                   pl.BlockSpec((1, tm, d), lambda b, i: (b, i, 0)),
                   pl.BlockSpec((1, tm, nq), lambda b, i: (b, i, 0))],
        out_shape=[jax.ShapeDtypeStruct((bsz, s, d), F32),
                   jax.ShapeDtypeStruct((bsz, s, d), BF16),
                   jax.ShapeDtypeStruct((bsz, s, nq), BF16)],
        compiler_params=_params("parallel", "parallel"),
        name="out_proj",
    )(o_gdn, o_mla, x, mod, w_out, g, w_pq)


PEER_CAND_ROWS = 2 * SUBLANES + (SUBLANES - 1) * SUBLANES + SUBLANES


def _peer_topk_kernel(qp_ref, keys_ref, idx_ref, gate_ref,
                      stop_ref, itop_ref, cand_ref, cidx_ref, best_ref, idxt_ref, gatet_ref):
    tb = qp_ref.shape[0]
    kk = PEER_TOPK
    assert kk == 2 * SUBLANES

    def extract(vals, row, n):
        m = jnp.max(vals, axis=0, keepdims=True)
        pos = jnp.min(jnp.where(vals == m, row, float(n)), axis=0, keepdims=True)
        return m, pos, jnp.where(row == pos, -jnp.inf, vals)

    row_k = lax.broadcasted_iota(jnp.int32, (N_KEYS, tb), 0).astype(F32)
    row_c = lax.broadcasted_iota(jnp.int32, (PEER_CAND_ROWS, tb), 0).astype(F32)
    row_8 = lax.broadcasted_iota(jnp.int32, (SUBLANES, tb), 0)
    for h in range(PEER_HEADS):
        vals = [_dot_nt(keys_ref[p, h], qp_ref[:, (2 * h + p) * N_KEYS:(2 * h + p + 1) * N_KEYS])
                for p in range(2)]
        for r in range(kk):
            for p in range(2):
                m, pos, vals[p] = extract(vals[p], row_k, N_KEYS)
                stop_ref[p, r:r + 1, :] = m
                itop_ref[p, r:r + 1, :] = pos
        cand_ref[0:kk, :] = stop_ref[0, 0:1, :] + stop_ref[1]
        cidx_ref[0:kk, :] = itop_ref[0, 0:1, :] * float(N_KEYS) + itop_ref[1]
        s2 = stop_ref[1, 0:SUBLANES, :]
        i2 = itop_ref[1, 0:SUBLANES, :]
        for a in range(1, SUBLANES):
            r0 = kk + (a - 1) * SUBLANES
            cand_ref[r0:r0 + SUBLANES, :] = jnp.where(
                row_8 < kk // (a + 1), stop_ref[0, a:a + 1, :] + s2, -jnp.inf)
            cidx_ref[r0:r0 + SUBLANES, :] = itop_ref[0, a:a + 1, :] * float(N_KEYS) + i2
        r0 = kk + (SUBLANES - 1) * SUBLANES
        cand_ref[r0:r0 + SUBLANES, :] = stop_ref[0, SUBLANES:kk, :] + stop_ref[1, 0:1, :]
        cidx_ref[r0:r0 + SUBLANES, :] = itop_ref[0, SUBLANES:kk, :] * float(N_KEYS) + itop_ref[1, 0:1, :]
        vals = cand_ref[...]
        cidx = cidx_ref[...]
        for r in range(kk):
            m, pos, vals = extract(vals, row_c, PEER_CAND_ROWS)
            best_ref[r:r + 1, :] = m
            idxt_ref[h * kk + r:h * kk + r + 1, :] = jnp.max(
                jnp.where(row_c == pos, cidx, -1.0), axis=0, keepdims=True)
        best = best_ref[...]
        e = jnp.exp(best - best[0:1, :])
        gatet_ref[h * kk:(h + 1) * kk, :] = e / jnp.sum(e, axis=0, keepdims=True)
    idx_ref[...] = idxt_ref[...].T.astype(jnp.int32)
    gate_ref[...] = gatet_ref[...].T


def _peer_topk(qp, keys, tb, part, nparts):
    nq = qp.shape[1]
    t = qp.shape[0] // nparts
    blk0 = part * (t // tb)
    hk = PEER_HEADS * PEER_TOPK
    kk = PEER_TOPK
    return pl.pallas_call(
        _peer_topk_kernel,
        grid=(t // tb,),
        in_specs=[pl.BlockSpec((tb, nq), lambda i: (i + blk0, 0)),
                  pl.BlockSpec(keys.shape, lambda i: (0, 0, 0, 0))],
        out_specs=[pl.BlockSpec((tb, hk), lambda i: (i, 0)),
                   pl.BlockSpec((tb, hk), lambda i: (i, 0))],
        out_shape=[jax.ShapeDtypeStruct((t, hk), jnp.int32),
                   jax.ShapeDtypeStruct((t, hk), F32)],
        scratch_shapes=[pltpu.VMEM((2, kk, tb), F32), pltpu.VMEM((2, kk, tb), F32),
                        pltpu.VMEM((PEER_CAND_ROWS, tb), F32),
                        pltpu.VMEM((PEER_CAND_ROWS, tb), F32),
                        pltpu.VMEM((kk, tb), F32),
                        pltpu.VMEM((hk, tb), F32), pltpu.VMEM((hk, tb), F32)],
        compiler_params=_params("parallel"),
        name="peer_topk",
    )(qp, keys)


PEER_TOKEN_GROUP = 16
PEER_PARTS = 4


def _peer_gate_matrix(idx, gate, n_experts):
    t_total, hk = idx.shape
    ln = SC_LANES
    nu = hk // ln
    grp = PEER_TOKEN_GROUP
    tok_per_w = t_total // SC_WORKERS
    n_groups = tok_per_w // grp
    assert n_groups * grp * SC_WORKERS == t_total and grp % 2 == 0
    mesh = plsc.VectorSubcoreMesh(core_axis_name="c", subcore_axis_name="s",
                                  num_cores=SC_CORES, num_subcores=SC_SUBCORES)

    @functools.partial(
        pl.kernel, mesh=mesh,
        compiler_params=pltpu.CompilerParams(needs_layout_passes=False),
        out_type=jax.ShapeDtypeStruct((t_total, n_experts), F32),
        scratch_types=[
            pltpu.VMEM((grp * hk,), jnp.int32),
            pltpu.VMEM((grp * hk,), F32),
            pltpu.VMEM((n_experts,), F32),
            pltpu.VMEM((n_experts,), F32),
            pltpu.SemaphoreType.DMA((2,)),
        ],
    )
    def gate_kernel(idx_hbm, gate_hbm, g_hbm, idx_v, gate_v, row0_v, row1_v, sem):
        rows = (row0_v, row1_v)
        wid = lax.axis_index("s") * SC_CORES + lax.axis_index("c")
        base = wid * tok_per_w
        zero = jnp.zeros((ln,), F32)

        def zero_body(c, carry):
            row0_v[pl.ds(c * ln, ln)] = zero
            row1_v[pl.ds(c * ln, ln)] = zero
            return carry
        lax.fori_loop(0, n_experts // ln, zero_body, 0)

        def out_copy(tok, slot):
            return pltpu.make_async_copy(rows[slot], g_hbm.at[tok], sem.at[slot])

        def group_body(g, carry):
            tok0 = base + g * grp
            pltpu.sync_copy(idx_hbm.at[pl.ds(tok0 * hk, grp * hk)], idx_v)
            pltpu.sync_copy(gate_hbm.at[pl.ds(tok0 * hk, grp * hk)], gate_v)

            def pair_body(i, carry2):
                for slot in range(2):
                    t = i * 2 + slot
                    for u in range(nu):
                        sl = pl.ds(t * hk + u * ln, ln)
                        plsc.addupdate_scatter(rows[slot], [idx_v[sl]], gate_v[sl])
                    out_copy(tok0 + t, slot).start()
                for slot in range(2):
                    t = i * 2 + slot
                    out_copy(tok0 + t, slot).wait()
                    for u in range(nu):
                        plsc.store_scatter(rows[slot], [idx_v[pl.ds(t * hk + u * ln, ln)]], zero)
                return carry2
            lax.fori_loop(0, grp // 2, pair_body, 0)
            return carry
        lax.fori_loop(0, n_groups, group_body, 0)

    return gate_kernel(idx.reshape(-1), gate.reshape(-1))


def _peer_dense_kernel(h_ref, g_ref, u_ref, v_ref, o_ref):
    e = pl.program_id(1)
    s = _dot_nt(h_ref[...], u_ref[...])
    p = (jax.nn.gelu(s) * g_ref[...]).astype(v_ref.dtype)
    contrib = _dot(p, v_ref[...])

    @pl.when(e == 0)
    def _():
        o_ref[...] = contrib

    @pl.when(e > 0)
    def _():
        o_ref[...] += contrib


def _peer_dense(h, gmat, u, v, tb, eb, part):
    t = gmat.shape[0]
    d = h.shape[1]
    n_e = u.shape[0]
    blk0 = part * (t // tb)
    return pl.pallas_call(
        _peer_dense_kernel,
        grid=(t // tb, n_e // eb),
        in_specs=[pl.BlockSpec((tb, d), lambda i, e: (i + blk0, 0)),
                  pl.BlockSpec((tb, eb), lambda i, e: (i, e)),
                  pl.BlockSpec((eb, d), lambda i, e: (e, 0)),
                  pl.BlockSpec((eb, d), lambda i, e: (e, 0))],
        out_specs=pl.BlockSpec((tb, d), lambda i, e: (i, 0)),
        out_shape=jax.ShapeDtypeStruct((t, d), F32),
        compiler_params=_params("parallel", "arbitrary"),
        name="peer_dense",
    )(h, gmat, u, v)


def _final_kernel(x1_ref, y_ref, mod_ref, g_ref, *rest, normalize):
    o_ref = rest[-1]
    m = mod_ref[0]
    x2 = x1_ref[0] + m[5:6] * y_ref[0]
    o_ref[0] = _rms(x2) * g_ref[...] if normalize else x2


def _final(x1, y, mod, g, tm, normalize, part, prev):
    bsz, s, d = x1.shape
    pb = y.shape[0]
    b0 = part * pb
    full_blk = pl.BlockSpec((1, tm, d), lambda b, i: (b + b0, i, 0))
    in_specs = [full_blk, pl.BlockSpec((1, tm, d), lambda b, i: (b, i, 0)),
                pl.BlockSpec((1, N_MOD, d), lambda b, i: (b + b0, 0, 0)),
                pl.BlockSpec((1, d), lambda b, i: (0, 0))]
    args = [x1, y, mod, g]
    aliases = {}
    if prev is not None:
        in_specs.append(pl.BlockSpec(memory_space=pl.ANY))
        args.append(prev)
        aliases = {4: 0}
    return pl.pallas_call(
        functools.partial(_final_kernel, normalize=normalize),
        grid=(pb, s // tm),
        in_specs=in_specs,
        out_specs=full_blk,
        out_shape=jax.ShapeDtypeStruct((bsz, s, d), F32),
        input_output_aliases=aliases,
        compiler_params=_params("parallel", "parallel"),
        name="final",
    )(*args)


def _split_w_in(w):
    gw = GDN_HEADS * GDN_HEAD_DIM
    sizes = [gw] * 4 + [GDN_HEADS] * 2 + [Q_LORA_RANK, KV_LORA_RANK, QK_ROPE_DIM]
    offs = [0]
    for sz in sizes:
        offs.append(offs[-1] + sz)
    parts = [w[:, offs[i]:offs[i + 1]] for i in range(len(sizes))]
    gq, gk, gv, gz, ga, gb, cq, ckv, kr = parts
    d = w.shape[0]
    half = QK_ROPE_DIM // 2
    zeros = lambda n: jnp.zeros((d, n), w.dtype)
    gdn = jnp.concatenate([gq, gk, gv, gz, ga, gb, zeros(LANES - 2 * GDN_HEADS)], axis=1)
    kr_sw = jnp.concatenate([kr[:, half:], kr[:, :half]], axis=1)
    mla = jnp.concatenate([cq, ckv, kr, zeros(LANES - QK_ROPE_DIM),
                           kr_sw, zeros(LANES - QK_ROPE_DIM)], axis=1)
    return jnp.concatenate([gdn, mla], axis=1).astype(BF16), gdn.shape[1], mla.shape[1]


def _split_w_uq(w):
    r = w.shape[0]
    half = QK_ROPE_DIM // 2
    hd = QK_NOPE_DIM + QK_ROPE_DIM
    main, swapped = [], []
    for h in range(MLA_HEADS):
        nope = w[:, h * hd:h * hd + QK_NOPE_DIM]
        rope = w[:, h * hd + QK_NOPE_DIM:(h + 1) * hd]
        main += [nope, rope, jnp.zeros((r, MLA_QK_PAD - hd), w.dtype)]
        swapped += [rope[:, half:], rope[:, :half], jnp.zeros((r, LANES - QK_ROPE_DIM), w.dtype)]
    return jnp.concatenate(main + swapped, axis=1).astype(BF16)


def _rotary_tables(positions):
    half = QK_ROPE_DIM // 2
    inv_freq = ROPE_THETA ** (-jnp.arange(half, dtype=F32) / half)
    ang = positions.astype(F32)[..., None] * inv_freq
    cos, sin = jnp.cos(ang), jnp.sin(ang)
    z = jnp.zeros(cos.shape[:-1] + (LANES - QK_ROPE_DIM,), F32)
    return (jnp.concatenate([cos, cos, z], axis=-1),
            jnp.concatenate([-sin, sin, z], axis=-1))


def _block(n, pref):
    return pref if n % pref == 0 else n


def kernel(x, c, positions, ln_mix_g, w_in, conv_w, a_log, dt_bias, gdn_norm_g, q_norm_g, w_uq, kv_norm_g, w_ukv, w_out, ln_ffn_g, w_pq, sub_keys, expert_u, expert_v, w_ada, b_ada, final_norm_g):
    bsz, s, d = x.shape
    depth = w_in.shape[0]
    cosr, sinr = _rotary_tables(positions)
    tm = _block(s, 512)
    for layer in range(depth):
        mod = _mod(c, w_ada[layer], b_ada[layer]).reshape(bsz, N_MOD, d)
        w_cat, gdn_cols, mla_cols = _split_w_in(w_in[layer])
        gdn_in, mla_in = _in_proj(x, mod, ln_mix_g[layer].reshape(1, d), w_cat, gdn_cols, mla_cols, tm)
        o_gdn = _gdn(gdn_in, conv_w[layer], a_log[layer], dt_bias[layer], gdn_norm_g[layer],
                     _block(s, 256))
        q, k, v = _mla_prep(mla_in, cosr, sinr, q_norm_g[layer].reshape(1, -1),
                            kv_norm_g[layer].reshape(1, -1), _split_w_uq(w_uq[layer]),
                            w_ukv[layer].astype(BF16), _block(s, 512))
        o_mla = _attn(q, k, v, _block(s, 1024))
        x1, h2, qp = _out_proj(o_gdn, o_mla, x, mod, w_out[layer].astype(BF16),
                               ln_ffn_g[layer].reshape(1, d), w_pq[layer].astype(BF16), tm)
        n_experts = expert_u.shape[1]
        u16 = expert_u[layer].astype(BF16)
        v16 = expert_v[layer].astype(BF16)
        qp2 = qp.reshape(bsz * s, -1)
        h22 = h2.reshape(bsz * s, d)
        nparts = PEER_PARTS if bsz % PEER_PARTS == 0 else 1
        tpart = bsz * s // nparts
        last = layer + 1 == depth
        out = None
        for part in range(nparts):
            idx, gate = _peer_topk(qp2, sub_keys[layer].astype(BF16), _block(tpart, 256), part, nparts)
            gmat = _peer_gate_matrix(idx, gate, n_experts)
            y = _peer_dense(h22, gmat, u16, v16, _block(tpart, 1024), _block(n_experts, 1024), part)
            out = _final(x1, y.reshape(bsz // nparts, s, d), mod, final_norm_g.reshape(1, d), tm,
                         last, part, out)
        x = out
    return x
```

```python
import functools

import jax
import jax.numpy as jnp
from jax import lax
from jax.experimental import pallas as pl
from jax.experimental.pallas import tpu as pltpu
from jax.experimental.pallas import tpu_sc as plsc

GDN_HEADS = 4
GDN_HEAD_DIM = 128
CONV_WIDTH = 4
CHUNK = 64
MLA_HEADS = 4
QK_NOPE_DIM = 128
QK_ROPE_DIM = 64
V_HEAD_DIM = 128
Q_LORA_RANK = 384
KV_LORA_RANK = 256
ROPE_THETA = 10000.0
PEER_HEADS = 8
N_KEYS = 128
PEER_TOPK = 16
N_MOD = 6
EPS = 1e-6

LANES = 128
SUBLANES = 8
SC_CORES = 2
SC_SUBCORES = 16
SC_LANES = 16
SC_WORKERS = SC_CORES * SC_SUBCORES
VMEM_LIMIT = 48 * 1024 * 1024
PEER_DENSE_VMEM = 56 * 1024 * 1024

F32 = jnp.float32
BF16 = jnp.bfloat16
HI = lax.Precision.HIGHEST


def _dot(a, b, precision=None):
    return jnp.dot(a, b, preferred_element_type=F32, precision=precision)


def _dot_nt(a, b, precision=None):
    return lax.dot_general(a, b, (((1,), (1,)), ((), ())),
                           preferred_element_type=F32, precision=precision)


def _dot_tn(a, b, precision=None):
    return lax.dot_general(a, b, (((0,), (0,)), ((), ())),
                           preferred_element_type=F32, precision=precision)


def _rms(x):
    return x * lax.rsqrt(jnp.mean(x * x, axis=-1, keepdims=True) + EPS)


def _silu(x):
    return x * jax.nn.sigmoid(x)


def _params(*sem):
    return pltpu.CompilerParams(dimension_semantics=sem, vmem_limit_bytes=VMEM_LIMIT)


def _mod_kernel(c_ref, w_ref, b_ref, o_ref):
    o_ref[...] = _dot(_silu(c_ref[...]), w_ref[...], HI) + b_ref[...]


def _mod(c, w_ada, b_ada):
    bsz, d = c.shape
    n = w_ada.shape[1]
    return pl.pallas_call(
        _mod_kernel,
        grid=(n // d,),
        in_specs=[pl.BlockSpec((bsz, d), lambda j: (0, 0)),
                  pl.BlockSpec((d, d), lambda j: (0, j)),
                  pl.BlockSpec((1, d), lambda j: (0, j))],
        out_specs=pl.BlockSpec((bsz, d), lambda j: (0, j)),
        out_shape=jax.ShapeDtypeStruct((bsz, n), F32),
        compiler_params=_params("arbitrary"),
        name="mod",
    )(c, w_ada, b_ada.reshape(1, n))


def _inproj_kernel(x_ref, mod_ref, g_ref, w_ref, gdn_ref, mla_ref):
    m = mod_ref[0]
    h = _rms(x_ref[0]) * g_ref[...] * (1.0 + m[1:2]) + m[0:1]
    p = _dot(h.astype(BF16), w_ref[...])
    gw = gdn_ref.shape[-1]
    gdn_ref[0] = p[:, :gw]
    mla_ref[0] = p[:, gw:]


def _in_proj(x, mod, g, w_cat, gdn_cols, mla_cols, tm):
    bsz, s, d = x.shape
    return pl.pallas_call(
        _inproj_kernel,
        grid=(bsz, s // tm),
        in_specs=[pl.BlockSpec((1, tm, d), lambda b, i: (b, i, 0)),
                  pl.BlockSpec((1, N_MOD, d), lambda b, i: (b, 0, 0)),
                  pl.BlockSpec((1, d), lambda b, i: (0, 0)),
                  pl.BlockSpec((d, gdn_cols + mla_cols), lambda b, i: (0, 0))],
        out_specs=[pl.BlockSpec((1, tm, gdn_cols), lambda b, i: (b, i, 0)),
                   pl.BlockSpec((1, tm, mla_cols), lambda b, i: (b, i, 0))],
        out_shape=[jax.ShapeDtypeStruct((bsz, s, gdn_cols), F32),
                   jax.ShapeDtypeStruct((bsz, s, mla_cols), F32)],
        compiler_params=_params("parallel", "parallel"),
        name="in_proj",
    )(x, mod, g, w_cat)


def _gdn_kernel(x_ref, cw_ref, alog_ref, dtb_ref, ng_ref, o_ref,
                state_ref, tail_ref, buf_ref, vnew_ref):
    sb = x_ref.shape[1]
    nchunk = sb // CHUNK
    hdim = GDN_HEAD_DIM
    gw = GDN_HEADS * hdim

    @pl.when(pl.program_id(1) == 0)
    def _():
        state_ref[...] = jnp.zeros_like(state_ref)
        tail_ref[...] = jnp.zeros_like(tail_ref)

    def conv_silu(slot):
        x = x_ref[0, :, slot * hdim:(slot + 1) * hdim]
        buf_ref[slot, 0:SUBLANES, :] = tail_ref[slot]
        buf_ref[slot, SUBLANES:SUBLANES + sb, :] = x
        tail_ref[slot] = x[sb - SUBLANES:sb, :]
        cw = cw_ref[slot]
        y = cw[CONV_WIDTH - 1:CONV_WIDTH] * x
        for j in range(CONV_WIDTH - 1):
            off = SUBLANES - (CONV_WIDTH - 1) + j
            y = y + cw[j:j + 1] * buf_ref[slot, off:off + sb, :]
        return _silu(y)

    def l2n(x):
        return x * lax.rsqrt(jnp.sum(x * x, axis=-1, keepdims=True) + EPS)

    ri = lax.broadcasted_iota(jnp.int32, (sb, sb), 0)
    ci = lax.broadcasted_iota(jnp.int32, (sb, sb), 1)
    same = (ri // CHUNK) == (ci // CHUNK)
    incl = same & (ci <= ri)
    strict = same & (ci < ri)
    eye = (ri == ci).astype(F32)
    levels = []
    bs = 1
    while bs < CHUNK:
        levels.append(((ri // bs) % 2 == 1) & ((ci // bs) % 2 == 0)
                      & ((ri // (2 * bs)) == (ci // (2 * bs))))
        bs *= 2
    rowmod = lax.broadcasted_iota(jnp.int32, (sb, hdim), 0) % CHUNK
    ab = x_ref[0, :, 4 * gw:4 * gw + LANES]

    heads = range(GDN_HEADS)
    q = [l2n(conv_silu(h)) * (hdim ** -0.5) for h in heads]
    k = [l2n(conv_silu(GDN_HEADS + h)) for h in heads]
    v = [conv_silu(2 * GDN_HEADS + h) for h in heads]
    beta = [jax.nn.sigmoid(ab[:, GDN_HEADS + h:GDN_HEADS + h + 1]) for h in heads]

    gc = [-jnp.exp(alog_ref[h]) * jax.nn.softplus(ab[:, h:h + 1] + dtb_ref[h]) for h in heads]
    sh = 1
    while sh < CHUNK:
        gc = [g + jnp.where(rowmod >= sh, pltpu.roll(g, sh, axis=0), 0.0) for g in gc]
        sh *= 2
    gcl = [jnp.concatenate(
        [jnp.broadcast_to(g[(n + 1) * CHUNK - 1:(n + 1) * CHUNK, :], (CHUNK, hdim))
         for n in range(nchunk)], axis=0) for g in gc]
    decay = []
    for g in gc:
        gc_row = jnp.broadcast_to(g.T[0:1, :], (sb, sb))
        decay.append(jnp.where(incl, jnp.exp(jnp.where(incl, g[:, 0:1] - gc_row, 0.0)), 0.0))

    kb = [k[h] * beta[h] for h in heads]
    k16 = [k[h].astype(BF16) for h in heads]
    mmat = [jnp.where(strict, _dot_nt(kb[h].astype(BF16), k16[h]) * decay[h], 0.0) for h in heads]
    attn = [(_dot_nt(q[h].astype(BF16), k16[h]) * decay[h]).astype(BF16) for h in heads]
    tinv = [eye - jnp.where(levels[0], m, 0.0) for m in mmat]
    for off in levels[1:]:
        t16 = [t.astype(BF16) for t in tinv]
        ta = [_dot(t16[h], jnp.where(off, mmat[h], 0.0).astype(BF16)).astype(BF16) for h in heads]
        tinv = [tinv[h] - _dot(ta[h], t16[h]) for h in heads]

    egc = [jnp.exp(g) for g in gc]
    rhs = [jnp.concatenate([v[h] * beta[h], kb[h] * egc[h]], axis=1) for h in heads]
    sol = [rhs[h] + _dot((tinv[h] - eye).astype(BF16), rhs[h].astype(BF16)) for h in heads]
    u = [x[:, :hdim] for x in sol]
    w16 = [x[:, hdim:].astype(BF16) for x in sol]
    qd16 = [(q[h] * egc[h]).astype(BF16) for h in heads]
    kd16 = [(k[h] * jnp.exp(gcl[h] - gc[h])).astype(BF16) for h in heads]
    g_last = [jnp.exp(x) for x in gcl]

    st = [state_ref[h] for h in heads]
    o_inter = [[] for _ in heads]
    for n in range(nchunk):
        lo = n * CHUNK
        for h in heads:
            st16 = st[h].astype(BF16)
            v_new = u[h][lo:lo + CHUNK] - _dot(w16[h][lo:lo + CHUNK], st16)
            vnew_ref[h, lo:lo + CHUNK, :] = v_new
            o_inter[h].append(_dot(qd16[h][lo:lo + CHUNK], st16))
            st[h] = st[h] * g_last[h][lo:lo + 1, :] + _dot_tn(kd16[h][lo:lo + CHUNK],
                                                             v_new.astype(BF16))
    for h in heads:
        state_ref[h] = st[h]
        o = jnp.concatenate(o_inter[h], axis=0) + _dot(attn[h], vnew_ref[h].astype(BF16))
        z = x_ref[0, :, 3 * gw + h * hdim:3 * gw + (h + 1) * hdim]
        o = _rms(o) * ng_ref[...] * _silu(z)
        o_ref[0, :, h * hdim:(h + 1) * hdim] = o.astype(o_ref.dtype)


def _gdn(gdn_in, conv_w, a_log, dt_bias, norm_g, sb):
    bsz, s, cols = gdn_in.shape
    hdim = GDN_HEAD_DIM
    nh = GDN_HEADS
    cw = conv_w.reshape(CONV_WIDTH, 3 * nh, hdim).transpose(1, 0, 2)
    alog = jnp.broadcast_to(a_log.reshape(nh, 1, 1), (nh, 1, hdim)).astype(F32)
    dtb = jnp.broadcast_to(dt_bias.reshape(nh, 1, 1), (nh, 1, hdim)).astype(F32)
    full = lambda shape: pl.BlockSpec(shape, lambda b, i: (0,) * len(shape))
    return pl.pallas_call(
        _gdn_kernel,
        grid=(bsz, s // sb),
        in_specs=[pl.BlockSpec((1, sb, cols), lambda b, i: (b, i, 0)),
                  full(cw.shape), full(alog.shape), full(dtb.shape), full((1, hdim))],
        out_specs=pl.BlockSpec((1, sb, nh * hdim), lambda b, i: (b, i, 0)),
        out_shape=jax.ShapeDtypeStruct((bsz, s, nh * hdim), BF16),
        scratch_shapes=[pltpu.VMEM((nh, hdim, hdim), F32),
                        pltpu.VMEM((3 * nh, SUBLANES, hdim), F32),
                        pltpu.VMEM((3 * nh, sb + SUBLANES, hdim), F32),
                        pltpu.VMEM((nh, sb, hdim), F32)],
        compiler_params=_params("parallel", "arbitrary"),
        name="gdn",
    )(gdn_in, cw, alog, dtb, norm_g.reshape(1, hdim))


MLA_QK_PAD = 256


def _mla_prep_kernel(m_ref, cos_ref, sin_ref, gq_ref, gkv_ref, wq_ref, wkv_ref,
                     q_ref, k_ref, v_ref):
    m = m_ref[0]
    cosr = cos_ref[0]
    sinr = sin_ref[0]
    cq = m[:, :Q_LORA_RANK]
    ckv = m[:, Q_LORA_RANK:Q_LORA_RANK + KV_LORA_RANK]
    o = Q_LORA_RANK + KV_LORA_RANK
    kr = m[:, o:o + LANES]
    krs = m[:, o + LANES:o + 2 * LANES]
    scale = (QK_NOPE_DIM + QK_ROPE_DIM) ** -0.5
    qa = _dot((_rms(cq) * gq_ref[...]).astype(BF16), wq_ref[...])
    kva = _dot((_rms(ckv) * gkv_ref[...]).astype(BF16), wkv_ref[...])
    k_rope = (kr * cosr + krs * sinr).astype(k_ref.dtype)
    sw0 = MLA_HEADS * MLA_QK_PAD
    for h in range(MLA_HEADS):
        b0 = h * MLA_QK_PAD
        rope = qa[:, b0 + LANES:b0 + 2 * LANES] * cosr + qa[:, sw0 + h * LANES:sw0 + (h + 1) * LANES] * sinr
        q_ref[0, h, :, 0:LANES] = (qa[:, b0:b0 + LANES] * scale).astype(q_ref.dtype)
        q_ref[0, h, :, LANES:2 * LANES] = (rope * scale).astype(q_ref.dtype)
        c0 = h * (QK_NOPE_DIM + V_HEAD_DIM)
        k_ref[0, h, :, 0:LANES] = kva[:, c0:c0 + QK_NOPE_DIM].astype(k_ref.dtype)
        k_ref[0, h, :, LANES:2 * LANES] = k_rope
        v_ref[0, h] = kva[:, c0 + QK_NOPE_DIM:c0 + QK_NOPE_DIM + V_HEAD_DIM].astype(v_ref.dtype)


def _mla_prep(mla_in, cosr, sinr, gq, gkv, wq, wkv, tm):
    bsz, s, mc = mla_in.shape
    nh = MLA_HEADS
    return pl.pallas_call(
        _mla_prep_kernel,
        grid=(bsz, s // tm),
        in_specs=[pl.BlockSpec((1, tm, mc), lambda b, i: (b, i, 0)),
                  pl.BlockSpec((1, tm, LANES), lambda b, i: (b, i, 0)),
                  pl.BlockSpec((1, tm, LANES), lambda b, i: (b, i, 0)),
                  pl.BlockSpec((1, Q_LORA_RANK), lambda b, i: (0, 0)),
                  pl.BlockSpec((1, KV_LORA_RANK), lambda b, i: (0, 0)),
                  pl.BlockSpec(wq.shape, lambda b, i: (0, 0)),
                  pl.BlockSpec(wkv.shape, lambda b, i: (0, 0))],
        out_specs=[pl.BlockSpec((1, nh, tm, MLA_QK_PAD), lambda b, i: (b, 0, i, 0)),
                   pl.BlockSpec((1, nh, tm, MLA_QK_PAD), lambda b, i: (b, 0, i, 0)),
                   pl.BlockSpec((1, nh, tm, V_HEAD_DIM), lambda b, i: (b, 0, i, 0))],
        out_shape=[jax.ShapeDtypeStruct((bsz, nh, s, MLA_QK_PAD), BF16),
                   jax.ShapeDtypeStruct((bsz, nh, s, MLA_QK_PAD), BF16),
                   jax.ShapeDtypeStruct((bsz, nh, s, V_HEAD_DIM), BF16)],
        compiler_params=_params("parallel", "parallel"),
        name="mla_prep",
    )(mla_in, cosr, sinr, gq, gkv, wq, wkv)


def _fold_lanes(x, op):
    parts = [x[:, c * LANES:(c + 1) * LANES] for c in range(x.shape[1] // LANES)]
    while len(parts) > 1:
        parts = [op(parts[i], parts[i + 1]) for i in range(0, len(parts), 2)]
    return parts[0]


def _attn_kernel(q_ref, k_ref, v_ref, o_ref, s_ref, m_ref, l_ref, acc_ref):
    i = pl.program_id(2)
    tq = q_ref.shape[2]
    m_ref[...] = jnp.full_like(m_ref, -jnp.inf)
    l_ref[...] = jnp.zeros_like(l_ref)
    acc_ref[...] = jnp.zeros_like(acc_ref)

    def scores(j, slot):
        start = pl.multiple_of(j * tq, tq)
        s_ref[slot] = _dot_nt(q_ref[0, 0], k_ref[0, 0, pl.ds(start, tq), :])

    def consume(j, slot, diagonal):
        start = pl.multiple_of(j * tq, tq)
        s = s_ref[slot]
        if diagonal:
            row = lax.broadcasted_iota(jnp.int32, s.shape, 0)
            col = lax.broadcasted_iota(jnp.int32, s.shape, 1)
            s = jnp.where(col <= row, s, -jnp.inf)
        m_prev = m_ref[...]
        m_new = jnp.maximum(m_prev, jnp.max(_fold_lanes(s, jnp.maximum), axis=-1, keepdims=True))
        alpha = jnp.exp(m_prev - m_new)
        p = jnp.exp(s - m_new)
        l_ref[...] = alpha * l_ref[...] + jnp.sum(_fold_lanes(p, jnp.add), axis=-1, keepdims=True)
        acc_ref[...] = alpha * acc_ref[...] + _dot(p.astype(v_ref.dtype),
                                                   v_ref[0, 0, pl.ds(start, tq), :])
        m_ref[...] = m_new

    scores(0, 0)

    def pair(t, carry):
        scores(2 * t + 1, 1)
        consume(2 * t, 0, False)
        scores(2 * t + 2, 0)
        consume(2 * t + 1, 1, False)
        return carry
    lax.fori_loop(0, i // 2, pair, 0)

    @pl.when(i % 2 == 1)
    def _():
        scores(i, 1)
        consume(i - 1, 0, False)
        consume(i, 1, True)

    @pl.when(i % 2 == 0)
    def _():
        consume(i, 0, True)

    o_ref[0] = (acc_ref[...] / l_ref[...]).astype(o_ref.dtype)


def _attn(q, k, v, tq):
    bsz, nh, s, dq = q.shape
    dv = v.shape[-1]
    return pl.pallas_call(
        _attn_kernel,
        grid=(bsz, nh, s // tq),
        in_specs=[pl.BlockSpec((1, 1, tq, dq), lambda b, h, i: (b, h, i, 0)),
                  pl.BlockSpec((1, 1, s, dq), lambda b, h, i: (b, h, 0, 0)),
                  pl.BlockSpec((1, 1, s, dv), lambda b, h, i: (b, h, 0, 0))],
        out_specs=pl.BlockSpec((1, tq, dv), lambda b, h, i: (b, i, h)),
        out_shape=jax.ShapeDtypeStruct((bsz, s, nh * dv), BF16),
        scratch_shapes=[pltpu.VMEM((2, tq, tq), F32),
                        pltpu.VMEM((tq, 1), F32), pltpu.VMEM((tq, 1), F32),
                        pltpu.VMEM((tq, dv), F32)],
        compiler_params=_params("parallel", "parallel", "arbitrary"),
        name="attn",
    )(q, k, v)


def _outproj_kernel(og_ref, om_ref, x_ref, mod_ref, wo_ref, g_ref, wpq_ref,
                    x1_ref, h2_ref, qp_ref):
    m = mod_ref[0]
    gw = og_ref.shape[-1]
    mixed = _dot(og_ref[0], wo_ref[0:gw, :]) + _dot(om_ref[0], wo_ref[gw:, :])
    x1 = x_ref[0] + m[2:3] * mixed
    h2 = _rms(x1) * g_ref[...] * (1.0 + m[4:5]) + m[3:4]
    x1_ref[0] = x1
    h2_ref[0] = h2.astype(h2_ref.dtype)
    qp_ref[0] = _dot(h2.astype(BF16), wpq_ref[...]).astype(qp_ref.dtype)


def _out_proj(o_gdn, o_mla, x, mod, w_out, g, w_pq, tm):
    bsz, s, d = x.shape
    gw = o_gdn.shape[-1]
    mw = o_mla.shape[-1]
    nq = w_pq.shape[1]
    return pl.pallas_call(
        _outproj_kernel,
        grid=(bsz, s // tm),
        in_specs=[pl.BlockSpec((1, tm, gw), lambda b, i: (b, i, 0)),
                  pl.BlockSpec((1, tm, mw), lambda b, i: (b, i, 0)),
                  pl.BlockSpec((1, tm, d), lambda b, i: (b, i, 0)),
                  pl.BlockSpec((1, N_MOD, d), lambda b, i: (b, 0, 0)),
                  pl.BlockSpec((gw + mw, d), lambda b, i: (0, 0)),
                  pl.BlockSpec((1, d), lambda b, i: (0, 0)),
                  pl.BlockSpec((d, nq), lambda b, i: (0, 0))],
        out_specs=[pl.BlockSpec((1, tm, d), lambda b, i: (b, i, 0)),
                   pl.BlockSpec((1, tm, d), lambda b, i: (b, i, 0)),
                   pl.BlockSpec((1, tm, nq), lambda b, i: (b, i, 0))],
        out_shape=[jax.ShapeDtypeStruct((bsz, s, d), F32),
                   jax.ShapeDtypeStruct((bsz, s, d), BF16),
                   jax.ShapeDtypeStruct((bsz, s, nq), BF16)],
        compiler_params=_params("parallel", "parallel"),
        name="out_proj",
    )(o_gdn, o_mla, x, mod, w_out, g, w_pq)


PEER_CAND_ROWS = 2 * SUBLANES + (SUBLANES - 1) * SUBLANES + SUBLANES


def _peer_topk_kernel(qp_ref, keys_ref, idx_ref, gate_ref,
                      stop_ref, itop_ref, cand_ref, cidx_ref, best_ref, idxt_ref, gatet_ref):
    tb = qp_ref.shape[0]
    kk = PEER_TOPK
    assert kk == 2 * SUBLANES

    def extract(vals, row, n):
        m = jnp.max(vals, axis=0, keepdims=True)
        pos = jnp.min(jnp.where(vals == m, row, float(n)), axis=0, keepdims=True)
        return m, pos, jnp.where(row == pos, -jnp.inf, vals)

    row_k = lax.broadcasted_iota(jnp.int32, (N_KEYS, tb), 0).astype(F32)
    row_c = lax.broadcasted_iota(jnp.int32, (PEER_CAND_ROWS, tb), 0).astype(F32)
    row_8 = lax.broadcasted_iota(jnp.int32, (SUBLANES, tb), 0)
    def build_candidates(h):
        cand_ref[h, 0:kk, :] = stop_ref[h, 0, 0:1, :] + stop_ref[h, 1]
        cidx_ref[h, 0:kk, :] = itop_ref[h, 0, 0:1, :] * float(N_KEYS) + itop_ref[h, 1]
        s2 = stop_ref[h, 1, 0:SUBLANES, :]
        i2 = itop_ref[h, 1, 0:SUBLANES, :]
        for a in range(1, SUBLANES):
            r0 = kk + (a - 1) * SUBLANES
            cand_ref[h, r0:r0 + SUBLANES, :] = jnp.where(
                row_8 < kk // (a + 1), stop_ref[h, 0, a:a + 1, :] + s2, -jnp.inf)
            cidx_ref[h, r0:r0 + SUBLANES, :] = itop_ref[h, 0, a:a + 1, :] * float(N_KEYS) + i2
        r0 = kk + (SUBLANES - 1) * SUBLANES
        cand_ref[h, r0:r0 + SUBLANES, :] = stop_ref[h, 0, SUBLANES:kk, :] + stop_ref[h, 1, 0:1, :]
        cidx_ref[h, r0:r0 + SUBLANES, :] = (itop_ref[h, 0, SUBLANES:kk, :] * float(N_KEYS)
                                           + itop_ref[h, 1, 0:1, :])

    for h in range(PEER_HEADS + 1):
        if h < PEER_HEADS:
            vals = [_dot_nt(keys_ref[p, h], qp_ref[:, (2 * h + p) * N_KEYS:(2 * h + p + 1) * N_KEYS])
                    for p in range(2)]
        if h >= 1:
            build_candidates(h - 1)
            cvals = cand_ref[h - 1]
            cidx = cidx_ref[h - 1]
        for r in range(kk):
            if h < PEER_HEADS:
                for p in range(2):
                    m, pos, vals[p] = extract(vals[p], row_k, N_KEYS)
                    stop_ref[h, p, r:r + 1, :] = m
                    itop_ref[h, p, r:r + 1, :] = pos
            if h >= 1:
                m, pos, cvals = extract(cvals, row_c, PEER_CAND_ROWS)
                best_ref[h - 1, r:r + 1, :] = m
                row_out = (h - 1) * kk + r
                idxt_ref[row_out:row_out + 1, :] = jnp.max(
                    jnp.where(row_c == pos, cidx, -1.0), axis=0, keepdims=True)
        if h >= 1:
            best = best_ref[h - 1]
            e = jnp.exp(best - best[0:1, :])
            gatet_ref[(h - 1) * kk:h * kk, :] = e / jnp.sum(e, axis=0, keepdims=True)
    idx_ref[...] = idxt_ref[...].T.astype(jnp.int32)
    gate_ref[...] = gatet_ref[...].T


def _peer_topk(qp, keys, tb, part, nparts):
    nq = qp.shape[1]
    t = qp.shape[0] // nparts
    blk0 = part * (t // tb)
    hk = PEER_HEADS * PEER_TOPK
    kk = PEER_TOPK
    return pl.pallas_call(
        _peer_topk_kernel,
        grid=(t // tb,),
        in_specs=[pl.BlockSpec((tb, nq), lambda i: (i + blk0, 0)),
                  pl.BlockSpec(keys.shape, lambda i: (0, 0, 0, 0))],
        out_specs=[pl.BlockSpec((tb, hk), lambda i: (i, 0)),
                   pl.BlockSpec((tb, hk), lambda i: (i, 0))],
        out_shape=[jax.ShapeDtypeStruct((t, hk), jnp.int32),
                   jax.ShapeDtypeStruct((t, hk), F32)],
        scratch_shapes=[pltpu.VMEM((PEER_HEADS, 2, kk, tb), F32),
                        pltpu.VMEM((PEER_HEADS, 2, kk, tb), F32),
                        pltpu.VMEM((PEER_HEADS, PEER_CAND_ROWS, tb), F32),
                        pltpu.VMEM((PEER_HEADS, PEER_CAND_ROWS, tb), F32),
                        pltpu.VMEM((PEER_HEADS, kk, tb), F32),
                        pltpu.VMEM((hk, tb), F32), pltpu.VMEM((hk, tb), F32)],
        compiler_params=_params("parallel"),
        name="peer_topk",
    )(qp, keys)


PEER_TOKEN_GROUP = 16
PEER_PARTS = 4


def _peer_gate_matrix(idx, gate, n_experts):
    t_total, hk = idx.shape
    ln = SC_LANES
    nu = hk // ln
    grp = PEER_TOKEN_GROUP
    tok_per_w = t_total // SC_WORKERS
    n_groups = tok_per_w // grp
    assert n_groups * grp * SC_WORKERS == t_total and grp % 2 == 0
    mesh = plsc.VectorSubcoreMesh(core_axis_name="c", subcore_axis_name="s",
                                  num_cores=SC_CORES, num_subcores=SC_SUBCORES)

    @functools.partial(
        pl.kernel, mesh=mesh,
        compiler_params=pltpu.CompilerParams(needs_layout_passes=False),
        out_type=jax.ShapeDtypeStruct((t_total, n_experts), F32),
        scratch_types=[
            pltpu.VMEM((grp * hk,), jnp.int32),
            pltpu.VMEM((grp * hk,), F32),
            pltpu.VMEM((n_experts,), F32),
            pltpu.VMEM((n_experts,), F32),
            pltpu.SemaphoreType.DMA((2,)),
        ],
    )
    def gate_kernel(idx_hbm, gate_hbm, g_hbm, idx_v, gate_v, row0_v, row1_v, sem):
        rows = (row0_v, row1_v)
        wid = lax.axis_index("s") * SC_CORES + lax.axis_index("c")
        base = wid * tok_per_w
        zero = jnp.zeros((ln,), F32)

        def zero_body(c, carry):
            row0_v[pl.ds(c * ln, ln)] = zero
            row1_v[pl.ds(c * ln, ln)] = zero
            return carry
        lax.fori_loop(0, n_experts // ln, zero_body, 0)

        def out_copy(tok, slot):
            return pltpu.make_async_copy(rows[slot], g_hbm.at[tok], sem.at[slot])

        def group_body(g, carry):
            tok0 = base + g * grp
            pltpu.sync_copy(idx_hbm.at[pl.ds(tok0 * hk, grp * hk)], idx_v)
            pltpu.sync_copy(gate_hbm.at[pl.ds(tok0 * hk, grp * hk)], gate_v)

            def pair_body(i, carry2):
                for slot in range(2):
                    t = i * 2 + slot
                    for u in range(nu):
                        sl = pl.ds(t * hk + u * ln, ln)
                        plsc.addupdate_scatter(rows[slot], [idx_v[sl]], gate_v[sl])
                    out_copy(tok0 + t, slot).start()
                for slot in range(2):
                    t = i * 2 + slot
                    out_copy(tok0 + t, slot).wait()
                    for u in range(nu):
                        plsc.store_scatter(rows[slot], [idx_v[pl.ds(t * hk + u * ln, ln)]], zero)
                return carry2
            lax.fori_loop(0, grp // 2, pair_body, 0)
            return carry
        lax.fori_loop(0, n_groups, group_body, 0)

    return gate_kernel(idx.reshape(-1), gate.reshape(-1))


def _peer_dense_kernel(h_ref, g_ref, u_ref, v_ref, x1_ref, mod_ref, ng_ref, *rest, normalize):
    o_ref = rest[-1]
    e = pl.program_id(1)
    s = _dot_nt(h_ref[...], u_ref[...])
    p = (jax.nn.gelu(s) * g_ref[...]).astype(v_ref.dtype)
    contrib = _dot(p, v_ref[...])

    @pl.when(e == 0)
    def _():
        o_ref[...] = contrib

    @pl.when(e > 0)
    def _():
        o_ref[...] += contrib

    @pl.when(e == pl.num_programs(1) - 1)
    def _():
        x2 = x1_ref[...] + mod_ref[0][5:6] * o_ref[...]
        o_ref[...] = _rms(x2) * ng_ref[...] if normalize else x2


def _peer_dense(h, gmat, u, v, x1, mod, ng, tb, eb, part, normalize, prev):
    t = gmat.shape[0]
    t_all, d = h.shape
    n_e = u.shape[0]
    blk0 = part * (t // tb)
    blk_per_batch = t_all // mod.shape[0] // tb
    tok_blk = pl.BlockSpec((tb, d), lambda i, e: (i + blk0, 0))
    in_specs = [tok_blk,
                pl.BlockSpec((tb, eb), lambda i, e: (i, e)),
                pl.BlockSpec((eb, d), lambda i, e: (e, 0)),
                pl.BlockSpec((eb, d), lambda i, e: (e, 0)),
                tok_blk,
                pl.BlockSpec((1, N_MOD, d), lambda i, e: ((i + blk0) // blk_per_batch, 0, 0)),
                pl.BlockSpec((1, d), lambda i, e: (0, 0))]
    args = [h, gmat, u, v, x1, mod, ng]
    aliases = {}
    if prev is not None:
        in_specs.append(pl.BlockSpec(memory_space=pl.ANY))
        args.append(prev)
        aliases = {len(args) - 1: 0}
    return pl.pallas_call(
        functools.partial(_peer_dense_kernel, normalize=normalize),
        grid=(t // tb, n_e // eb),
        in_specs=in_specs,
        out_specs=tok_blk,
        out_shape=jax.ShapeDtypeStruct((t_all, d), F32),
        input_output_aliases=aliases,
        compiler_params=pltpu.CompilerParams(dimension_semantics=("parallel", "arbitrary"),
                                             vmem_limit_bytes=PEER_DENSE_VMEM),
        name="peer_dense",
    )(*args)


def _split_w_in(w):
    gw = GDN_HEADS * GDN_HEAD_DIM
    sizes = [gw] * 4 + [GDN_HEADS] * 2 + [Q_LORA_RANK, KV_LORA_RANK, QK_ROPE_DIM]
    offs = [0]
    for sz in sizes:
        offs.append(offs[-1] + sz)
    parts = [w[:, offs[i]:offs[i + 1]] for i in range(len(sizes))]
    gq, gk, gv, gz, ga, gb, cq, ckv, kr = parts
    d = w.shape[0]
    half = QK_ROPE_DIM // 2
    zeros = lambda n: jnp.zeros((d, n), w.dtype)
    gdn = jnp.concatenate([gq, gk, gv, gz, ga, gb, zeros(LANES - 2 * GDN_HEADS)], axis=1)
    kr_sw = jnp.concatenate([kr[:, half:], kr[:, :half]], axis=1)
    mla = jnp.concatenate([cq, ckv, kr, zeros(LANES - QK_ROPE_DIM),
                           kr_sw, zeros(LANES - QK_ROPE_DIM)], axis=1)
    return jnp.concatenate([gdn, mla], axis=1).astype(BF16), gdn.shape[1], mla.shape[1]


def _split_w_uq(w):
    r = w.shape[0]
    half = QK_ROPE_DIM // 2
    hd = QK_NOPE_DIM + QK_ROPE_DIM
    main, swapped = [], []
    for h in range(MLA_HEADS):
        nope = w[:, h * hd:h * hd + QK_NOPE_DIM]
        rope = w[:, h * hd + QK_NOPE_DIM:(h + 1) * hd]
        main += [nope, rope, jnp.zeros((r, MLA_QK_PAD - hd), w.dtype)]
        swapped += [rope[:, half:], rope[:, :half], jnp.zeros((r, LANES - QK_ROPE_DIM), w.dtype)]
    return jnp.concatenate(main + swapped, axis=1).astype(BF16)


def _rotary_tables(positions):
    half = QK_ROPE_DIM // 2
    inv_freq = ROPE_THETA ** (-jnp.arange(half, dtype=F32) / half)
    ang = positions.astype(F32)[..., None] * inv_freq
    cos, sin = jnp.cos(ang), jnp.sin(ang)
    z = jnp.zeros(cos.shape[:-1] + (LANES - QK_ROPE_DIM,), F32)
    return (jnp.concatenate([cos, cos, z], axis=-1),
            jnp.concatenate([-sin, sin, z], axis=-1))


def _block(n, pref):
    return pref if n % pref == 0 else n


def kernel(x, c, positions, ln_mix_g, w_in, conv_w, a_log, dt_bias, gdn_norm_g, q_norm_g, w_uq, kv_norm_g, w_ukv, w_out, ln_ffn_g, w_pq, sub_keys, expert_u, expert_v, w_ada, b_ada, final_norm_g):
    bsz, s, d = x.shape
    depth = w_in.shape[0]
    cosr, sinr = _rotary_tables(positions)
    tm = _block(s, 512)
    for layer in range(depth):
        mod = _mod(c, w_ada[layer], b_ada[layer]).reshape(bsz, N_MOD, d)
        w_cat, gdn_cols, mla_cols = _split_w_in(w_in[layer])
        gdn_in, mla_in = _in_proj(x, mod, ln_mix_g[layer].reshape(1, d), w_cat, gdn_cols, mla_cols, tm)
        o_gdn = _gdn(gdn_in, conv_w[layer], a_log[layer], dt_bias[layer], gdn_norm_g[layer],
                     _block(s, 256))
        q, k, v = _mla_prep(mla_in, cosr, sinr, q_norm_g[layer].reshape(1, -1),
                            kv_norm_g[layer].reshape(1, -1), _split_w_uq(w_uq[layer]),
                            w_ukv[layer].astype(BF16), _block(s, 512))
        o_mla = _attn(q, k, v, _block(s, 1024))
        x1, h2, qp = _out_proj(o_gdn, o_mla, x, mod, w_out[layer].astype(BF16),
                               ln_ffn_g[layer].reshape(1, d), w_pq[layer].astype(BF16), tm)
        n_experts = expert_u.shape[1]
        u16 = expert_u[layer].astype(BF16)
        v16 = expert_v[layer].astype(BF16)
        qp2 = qp.reshape(bsz * s, -1)
        h22 = h2.reshape(bsz * s, d)
        nparts = PEER_PARTS if bsz % PEER_PARTS == 0 else 1
        tpart = bsz * s // nparts
        last = layer + 1 == depth
        out = None
        for part in range(nparts):
            idx, gate = _peer_topk(qp2, sub_keys[layer].astype(BF16), _block(tpart, 256), part, nparts)
            gmat = _peer_gate_matrix(idx, gate, n_experts)
            out = _peer_dense(h22, gmat, u16, v16, x1.reshape(bsz * s, d), mod,
                              final_norm_g.reshape(1, d), _block(tpart, 1024),
                              _block(n_experts, 1024), part, last, out)
        x = out.reshape(bsz, s, d)
    return x
```

```python
import functools

import jax
import jax.numpy as jnp
from jax import lax
from jax.experimental import pallas as pl
from jax.experimental.pallas import tpu as pltpu
from jax.experimental.pallas import tpu_sc as plsc

GDN_HEADS = 4
GDN_HEAD_DIM = 128
CONV_WIDTH = 4
CHUNK = 64
MLA_HEADS = 4
QK_NOPE_DIM = 128
QK_ROPE_DIM = 64
V_HEAD_DIM = 128
Q_LORA_RANK = 384
KV_LORA_RANK = 256
ROPE_THETA = 10000.0
PEER_HEADS = 8
N_KEYS = 128
PEER_TOPK = 16
N_MOD = 6
EPS = 1e-6

LANES = 128
SUBLANES = 8
SC_CORES = 2
SC_SUBCORES = 16
SC_LANES = 16
SC_WORKERS = SC_CORES * SC_SUBCORES
VMEM_LIMIT = 48 * 1024 * 1024
PEER_DENSE_VMEM = 56 * 1024 * 1024

F32 = jnp.float32
BF16 = jnp.bfloat16
HI = lax.Precision.HIGHEST


def _dot(a, b, precision=None):
    return jnp.dot(a, b, preferred_element_type=F32, precision=precision)


def _dot_nt(a, b, precision=None):
    return lax.dot_general(a, b, (((1,), (1,)), ((), ())),
                           preferred_element_type=F32, precision=precision)


def _dot_tn(a, b, precision=None):
    return lax.dot_general(a, b, (((0,), (0,)), ((), ())),
                           preferred_element_type=F32, precision=precision)


def _rms(x):
    return x * lax.rsqrt(jnp.mean(x * x, axis=-1, keepdims=True) + EPS)


def _silu(x):
    return x * jax.nn.sigmoid(x)


def _params(*sem):
    return pltpu.CompilerParams(dimension_semantics=sem, vmem_limit_bytes=VMEM_LIMIT)


def _mod_kernel(c_ref, w_ref, b_ref, o_ref):
    o_ref[...] = _dot(_silu(c_ref[...]), w_ref[...], HI) + b_ref[...]


def _mod(c, w_ada, b_ada):
    bsz, d = c.shape
    n = w_ada.shape[1]
    return pl.pallas_call(
        _mod_kernel,
        grid=(n // d,),
        in_specs=[pl.BlockSpec((bsz, d), lambda j: (0, 0)),
                  pl.BlockSpec((d, d), lambda j: (0, j)),
                  pl.BlockSpec((1, d), lambda j: (0, j))],
        out_specs=pl.BlockSpec((bsz, d), lambda j: (0, j)),
        out_shape=jax.ShapeDtypeStruct((bsz, n), F32),
        compiler_params=_params("arbitrary"),
        name="mod",
    )(c, w_ada, b_ada.reshape(1, n))


def _inproj_kernel(x_ref, mod_ref, g_ref, w_ref, gdn_ref, mla_ref):
    m = mod_ref[0]
    h = _rms(x_ref[0]) * g_ref[...] * (1.0 + m[1:2]) + m[0:1]
    p = _dot(h.astype(BF16), w_ref[...])
    gw = gdn_ref.shape[-1]
    gdn_ref[0] = p[:, :gw]
    mla_ref[0] = p[:, gw:]


def _in_proj(x, mod, g, w_cat, gdn_cols, mla_cols, tm):
    bsz, s, d = x.shape
    return pl.pallas_call(
        _inproj_kernel,
        grid=(bsz, s // tm),
        in_specs=[pl.BlockSpec((1, tm, d), lambda b, i: (b, i, 0)),
                  pl.BlockSpec((1, N_MOD, d), lambda b, i: (b, 0, 0)),
                  pl.BlockSpec((1, d), lambda b, i: (0, 0)),
                  pl.BlockSpec((d, gdn_cols + mla_cols), lambda b, i: (0, 0))],
        out_specs=[pl.BlockSpec((1, tm, gdn_cols), lambda b, i: (b, i, 0)),
                   pl.BlockSpec((1, tm, mla_cols), lambda b, i: (b, i, 0))],
        out_shape=[jax.ShapeDtypeStruct((bsz, s, gdn_cols), F32),
                   jax.ShapeDtypeStruct((bsz, s, mla_cols), F32)],
        compiler_params=_params("parallel", "parallel"),
        name="in_proj",
    )(x, mod, g, w_cat)


def _gdn_kernel(x_ref, cw_ref, alog_ref, dtb_ref, ng_ref, o_ref,
                state_ref, tail_ref, buf_ref, vnew_ref):
    sb = x_ref.shape[1]
    nchunk = sb // CHUNK
    hdim = GDN_HEAD_DIM
    gw = GDN_HEADS * hdim

    @pl.when(pl.program_id(1) == 0)
    def _():
        state_ref[...] = jnp.zeros_like(state_ref)
        tail_ref[...] = jnp.zeros_like(tail_ref)

    def conv_silu(slot):
        x = x_ref[0, :, slot * hdim:(slot + 1) * hdim]
        buf_ref[slot, 0:SUBLANES, :] = tail_ref[slot]
        buf_ref[slot, SUBLANES:SUBLANES + sb, :] = x
        tail_ref[slot] = x[sb - SUBLANES:sb, :]
        cw = cw_ref[slot]
        y = cw[CONV_WIDTH - 1:CONV_WIDTH] * x
        for j in range(CONV_WIDTH - 1):
            off = SUBLANES - (CONV_WIDTH - 1) + j
            y = y + cw[j:j + 1] * buf_ref[slot, off:off + sb, :]
        return _silu(y)

    def l2n(x):
        return x * lax.rsqrt(jnp.sum(x * x, axis=-1, keepdims=True) + EPS)

    ri = lax.broadcasted_iota(jnp.int32, (sb, sb), 0)
    ci = lax.broadcasted_iota(jnp.int32, (sb, sb), 1)
    same = (ri // CHUNK) == (ci // CHUNK)
    incl = same & (ci <= ri)
    strict = same & (ci < ri)
    eye = (ri == ci).astype(F32)
    levels = []
    bs = 1
    while bs < CHUNK:
        levels.append(((ri // bs) % 2 == 1) & ((ci // bs) % 2 == 0)
                      & ((ri // (2 * bs)) == (ci // (2 * bs))))
        bs *= 2
    rowmod = lax.broadcasted_iota(jnp.int32, (sb, hdim), 0) % CHUNK
    ab = x_ref[0, :, 4 * gw:4 * gw + LANES]

    heads = range(GDN_HEADS)
    q = [l2n(conv_silu(h)) * (hdim ** -0.5) for h in heads]
    k = [l2n(conv_silu(GDN_HEADS + h)) for h in heads]
    v = [conv_silu(2 * GDN_HEADS + h) for h in heads]
    beta = [jax.nn.sigmoid(ab[:, GDN_HEADS + h:GDN_HEADS + h + 1]) for h in heads]

    gc = [-jnp.exp(alog_ref[h]) * jax.nn.softplus(ab[:, h:h + 1] + dtb_ref[h]) for h in heads]
    sh = 1
    while sh < CHUNK:
        gc = [g + jnp.where(rowmod >= sh, pltpu.roll(g, sh, axis=0), 0.0) for g in gc]
        sh *= 2
    gcl = [jnp.concatenate(
        [jnp.broadcast_to(g[(n + 1) * CHUNK - 1:(n + 1) * CHUNK, :], (CHUNK, hdim))
         for n in range(nchunk)], axis=0) for g in gc]
    decay = []
    for g in gc:
        gc_row = jnp.broadcast_to(g.T[0:1, :], (sb, sb))
        decay.append(jnp.where(incl, jnp.exp(jnp.where(incl, g[:, 0:1] - gc_row, 0.0)), 0.0))

    kb = [k[h] * beta[h] for h in heads]
    k16 = [k[h].astype(BF16) for h in heads]
    mmat = [jnp.where(strict, _dot_nt(kb[h].astype(BF16), k16[h]) * decay[h], 0.0) for h in heads]
    attn = [(_dot_nt(q[h].astype(BF16), k16[h]) * decay[h]).astype(BF16) for h in heads]
    tinv = [eye - jnp.where(levels[0], m, 0.0) for m in mmat]
    for off in levels[1:]:
        t16 = [t.astype(BF16) for t in tinv]
        ta = [_dot(t16[h], jnp.where(off, mmat[h], 0.0).astype(BF16)).astype(BF16) for h in heads]
        tinv = [tinv[h] - _dot(ta[h], t16[h]) for h in heads]

    egc = [jnp.exp(g) for g in gc]
    rhs = [jnp.concatenate([v[h] * beta[h], kb[h] * egc[h]], axis=1) for h in heads]
    sol = [rhs[h] + _dot((tinv[h] - eye).astype(BF16), rhs[h].astype(BF16)) for h in heads]
    u = [x[:, :hdim] for x in sol]
    w16 = [x[:, hdim:].astype(BF16) for x in sol]
    qd16 = [(q[h] * egc[h]).astype(BF16) for h in heads]
    kd16 = [(k[h] * jnp.exp(gcl[h] - gc[h])).astype(BF16) for h in heads]
    g_last = [jnp.exp(x) for x in gcl]

    st = [state_ref[h] for h in heads]
    o_inter = [[] for _ in heads]
    for n in range(nchunk):
        lo = n * CHUNK
        for h in heads:
            st16 = st[h].astype(BF16)
            v_new = u[h][lo:lo + CHUNK] - _dot(w16[h][lo:lo + CHUNK], st16)
            vnew_ref[h, lo:lo + CHUNK, :] = v_new
            o_inter[h].append(_dot(qd16[h][lo:lo + CHUNK], st16))
            st[h] = st[h] * g_last[h][lo:lo + 1, :] + _dot_tn(kd16[h][lo:lo + CHUNK],
                                                             v_new.astype(BF16))
    for h in heads:
        state_ref[h] = st[h]
        o = jnp.concatenate(o_inter[h], axis=0) + _dot(attn[h], vnew_ref[h].astype(BF16))
        z = x_ref[0, :, 3 * gw + h * hdim:3 * gw + (h + 1) * hdim]
        o = _rms(o) * ng_ref[...] * _silu(z)
        o_ref[0, :, h * hdim:(h + 1) * hdim] = o.astype(o_ref.dtype)


def _gdn(gdn_in, conv_w, a_log, dt_bias, norm_g, sb):
    bsz, s, cols = gdn_in.shape
    hdim = GDN_HEAD_DIM
    nh = GDN_HEADS
    cw = conv_w.reshape(CONV_WIDTH, 3 * nh, hdim).transpose(1, 0, 2)
    alog = jnp.broadcast_to(a_log.reshape(nh, 1, 1), (nh, 1, hdim)).astype(F32)
    dtb = jnp.broadcast_to(dt_bias.reshape(nh, 1, 1), (nh, 1, hdim)).astype(F32)
    full = lambda shape: pl.BlockSpec(shape, lambda b, i: (0,) * len(shape))
    return pl.pallas_call(
        _gdn_kernel,
        grid=(bsz, s // sb),
        in_specs=[pl.BlockSpec((1, sb, cols), lambda b, i: (b, i, 0)),
                  full(cw.shape), full(alog.shape), full(dtb.shape), full((1, hdim))],
        out_specs=pl.BlockSpec((1, sb, nh * hdim), lambda b, i: (b, i, 0)),
        out_shape=jax.ShapeDtypeStruct((bsz, s, nh * hdim), BF16),
        scratch_shapes=[pltpu.VMEM((nh, hdim, hdim), F32),
                        pltpu.VMEM((3 * nh, SUBLANES, hdim), F32),
                        pltpu.VMEM((3 * nh, sb + SUBLANES, hdim), F32),
                        pltpu.VMEM((nh, sb, hdim), F32)],
        compiler_params=_params("parallel", "arbitrary"),
        name="gdn",
    )(gdn_in, cw, alog, dtb, norm_g.reshape(1, hdim))


MLA_QK_PAD = 256


def _mla_prep_kernel(m_ref, cos_ref, sin_ref, gq_ref, gkv_ref, wq_ref, wkv_ref,
                     q_ref, k_ref, v_ref):
    m = m_ref[0]
    cosr = cos_ref[0]
    sinr = sin_ref[0]
    cq = m[:, :Q_LORA_RANK]
    ckv = m[:, Q_LORA_RANK:Q_LORA_RANK + KV_LORA_RANK]
    o = Q_LORA_RANK + KV_LORA_RANK
    kr = m[:, o:o + LANES]
    krs = m[:, o + LANES:o + 2 * LANES]
    scale = (QK_NOPE_DIM + QK_ROPE_DIM) ** -0.5
    qa = _dot((_rms(cq) * gq_ref[...]).astype(BF16), wq_ref[...])
    kva = _dot((_rms(ckv) * gkv_ref[...]).astype(BF16), wkv_ref[...])
    k_rope = (kr * cosr + krs * sinr).astype(k_ref.dtype)
    sw0 = MLA_HEADS * MLA_QK_PAD
    for h in range(MLA_HEADS):
        b0 = h * MLA_QK_PAD
        rope = qa[:, b0 + LANES:b0 + 2 * LANES] * cosr + qa[:, sw0 + h * LANES:sw0 + (h + 1) * LANES] * sinr
        q_ref[0, h, :, 0:LANES] = (qa[:, b0:b0 + LANES] * scale).astype(q_ref.dtype)
        q_ref[0, h, :, LANES:2 * LANES] = (rope * scale).astype(q_ref.dtype)
        c0 = h * (QK_NOPE_DIM + V_HEAD_DIM)
        k_ref[0, h, :, 0:LANES] = kva[:, c0:c0 + QK_NOPE_DIM].astype(k_ref.dtype)
        k_ref[0, h, :, LANES:2 * LANES] = k_rope
        v_ref[0, h] = kva[:, c0 + QK_NOPE_DIM:c0 + QK_NOPE_DIM + V_HEAD_DIM].astype(v_ref.dtype)


def _mla_prep(mla_in, cosr, sinr, gq, gkv, wq, wkv, tm):
    bsz, s, mc = mla_in.shape
    nh = MLA_HEADS
    return pl.pallas_call(
        _mla_prep_kernel,
        grid=(bsz, s // tm),
        in_specs=[pl.BlockSpec((1, tm, mc), lambda b, i: (b, i, 0)),
                  pl.BlockSpec((1, tm, LANES), lambda b, i: (b, i, 0)),
                  pl.BlockSpec((1, tm, LANES), lambda b, i: (b, i, 0)),
                  pl.BlockSpec((1, Q_LORA_RANK), lambda b, i: (0, 0)),
                  pl.BlockSpec((1, KV_LORA_RANK), lambda b, i: (0, 0)),
                  pl.BlockSpec(wq.shape, lambda b, i: (0, 0)),
                  pl.BlockSpec(wkv.shape, lambda b, i: (0, 0))],
        out_specs=[pl.BlockSpec((1, nh, tm, MLA_QK_PAD), lambda b, i: (b, 0, i, 0)),
                   pl.BlockSpec((1, nh, tm, MLA_QK_PAD), lambda b, i: (b, 0, i, 0)),
                   pl.BlockSpec((1, nh, tm, V_HEAD_DIM), lambda b, i: (b, 0, i, 0))],
        out_shape=[jax.ShapeDtypeStruct((bsz, nh, s, MLA_QK_PAD), BF16),
                   jax.ShapeDtypeStruct((bsz, nh, s, MLA_QK_PAD), BF16),
                   jax.ShapeDtypeStruct((bsz, nh, s, V_HEAD_DIM), BF16)],
        compiler_params=_params("parallel", "parallel"),
        name="mla_prep",
    )(mla_in, cosr, sinr, gq, gkv, wq, wkv)


def _fold_lanes(x, op):
    parts = [x[:, c * LANES:(c + 1) * LANES] for c in range(x.shape[1] // LANES)]
    while len(parts) > 1:
        parts = [op(parts[i], parts[i + 1]) for i in range(0, len(parts), 2)]
    return parts[0]


def _attn_kernel(q_ref, k_ref, v_ref, o_ref, s_ref, m_ref, l_ref, acc_ref):
    i = pl.program_id(2)
    tq = q_ref.shape[2]
    m_ref[...] = jnp.full_like(m_ref, -jnp.inf)
    l_ref[...] = jnp.zeros_like(l_ref)
    acc_ref[...] = jnp.zeros_like(acc_ref)

    def scores(j, slot):
        start = pl.multiple_of(j * tq, tq)
        s_ref[slot] = _dot_nt(q_ref[0, 0], k_ref[0, 0, pl.ds(start, tq), :])

    def consume(j, slot, diagonal):
        start = pl.multiple_of(j * tq, tq)
        s = s_ref[slot]
        if diagonal:
            row = lax.broadcasted_iota(jnp.int32, s.shape, 0)
            col = lax.broadcasted_iota(jnp.int32, s.shape, 1)
            s = jnp.where(col <= row, s, -jnp.inf)
        m_prev = m_ref[...]
        m_new = jnp.maximum(m_prev, jnp.max(_fold_lanes(s, jnp.maximum), axis=-1, keepdims=True))
        alpha = jnp.exp(m_prev - m_new)
        p = jnp.exp(s - m_new)
        l_ref[...] = alpha * l_ref[...] + jnp.sum(_fold_lanes(p, jnp.add), axis=-1, keepdims=True)
        acc_ref[...] = alpha * acc_ref[...] + _dot(p.astype(v_ref.dtype),
                                                   v_ref[0, 0, pl.ds(start, tq), :])
        m_ref[...] = m_new

    scores(0, 0)

    def pair(t, carry):
        scores(2 * t + 1, 1)
        consume(2 * t, 0, False)
        scores(2 * t + 2, 0)
        consume(2 * t + 1, 1, False)
        return carry
    lax.fori_loop(0, i // 2, pair, 0)

    @pl.when(i % 2 == 1)
    def _():
        scores(i, 1)
        consume(i - 1, 0, False)
        consume(i, 1, True)

    @pl.when(i % 2 == 0)
    def _():
        consume(i, 0, True)

    o_ref[0] = (acc_ref[...] / l_ref[...]).astype(o_ref.dtype)


def _attn(q, k, v, tq):
    bsz, nh, s, dq = q.shape
    dv = v.shape[-1]
    return pl.pallas_call(
        _attn_kernel,
        grid=(bsz, nh, s // tq),
        in_specs=[pl.BlockSpec((1, 1, tq, dq), lambda b, h, i: (b, h, i, 0)),
                  pl.BlockSpec((1, 1, s, dq), lambda b, h, i: (b, h, 0, 0)),
                  pl.BlockSpec((1, 1, s, dv), lambda b, h, i: (b, h, 0, 0))],
        out_specs=pl.BlockSpec((1, tq, dv), lambda b, h, i: (b, i, h)),
        out_shape=jax.ShapeDtypeStruct((bsz, s, nh * dv), BF16),
        scratch_shapes=[pltpu.VMEM((2, tq, tq), F32),
                        pltpu.VMEM((tq, 1), F32), pltpu.VMEM((tq, 1), F32),
                        pltpu.VMEM((tq, dv), F32)],
        compiler_params=_params("parallel", "parallel", "arbitrary"),
        name="attn",
    )(q, k, v)


def _outproj_kernel(og_ref, om_ref, x_ref, mod_ref, wo_ref, g_ref, wpq_ref, keys_ref,
                    x1_ref, h2_ref, sc_ref):
    m = mod_ref[0]
    gw = og_ref.shape[-1]
    mixed = _dot(og_ref[0], wo_ref[0:gw, :]) + _dot(om_ref[0], wo_ref[gw:, :])
    x1 = x_ref[0] + m[2:3] * mixed
    h2 = _rms(x1) * g_ref[...] * (1.0 + m[4:5]) + m[3:4]
    x1_ref[0] = x1
    h2_ref[0] = h2.astype(h2_ref.dtype)
    qp = _dot(h2.astype(BF16), wpq_ref[...]).astype(BF16)
    for h in range(PEER_HEADS):
        for p in range(2):
            c0 = (2 * h + p) * N_KEYS
            sc_ref[0, :, c0:c0 + N_KEYS] = _dot_nt(qp[:, c0:c0 + N_KEYS], keys_ref[p, h])


def _out_proj(o_gdn, o_mla, x, mod, w_out, g, w_pq, keys, tm):
    bsz, s, d = x.shape
    gw = o_gdn.shape[-1]
    mw = o_mla.shape[-1]
    nq = w_pq.shape[1]
    return pl.pallas_call(
        _outproj_kernel,
        grid=(bsz, s // tm),
        in_specs=[pl.BlockSpec((1, tm, gw), lambda b, i: (b, i, 0)),
                  pl.BlockSpec((1, tm, mw), lambda b, i: (b, i, 0)),
                  pl.BlockSpec((1, tm, d), lambda b, i: (b, i, 0)),
                  pl.BlockSpec((1, N_MOD, d), lambda b, i: (b, 0, 0)),
                  pl.BlockSpec((gw + mw, d), lambda b, i: (0, 0)),
                  pl.BlockSpec((1, d), lambda b, i: (0, 0)),
                  pl.BlockSpec((d, nq), lambda b, i: (0, 0)),
                  pl.BlockSpec(keys.shape, lambda b, i: (0, 0, 0, 0))],
        out_specs=[pl.BlockSpec((1, tm, d), lambda b, i: (b, i, 0)),
                   pl.BlockSpec((1, tm, d), lambda b, i: (b, i, 0)),
                   pl.BlockSpec((1, tm, nq), lambda b, i: (b, i, 0))],
        out_shape=[jax.ShapeDtypeStruct((bsz, s, d), F32),
                   jax.ShapeDtypeStruct((bsz, s, d), BF16),
                   jax.ShapeDtypeStruct((bsz, s, nq), F32)],
        compiler_params=_params("parallel", "parallel"),
        name="out_proj",
    )(o_gdn, o_mla, x, mod, w_out, g, w_pq, keys)


PEER_SORT_GROUP = 256


def _peer_key_topk(scores, part, nparts):
    r_all, nk = scores.shape
    ln = SC_LANES
    assert PEER_TOPK == ln
    nv = nk // ln
    grp = PEER_SORT_GROUP
    rows = r_all // nparts
    rows_per_w = rows // SC_WORKERS
    n_groups = rows_per_w // grp
    assert n_groups * grp * SC_WORKERS * nparts == r_all
    row_base = part * rows
    mesh = plsc.VectorSubcoreMesh(core_axis_name="c", subcore_axis_name="s",
                                  num_cores=SC_CORES, num_subcores=SC_SUBCORES)

    @functools.partial(
        pl.kernel, mesh=mesh,
        compiler_params=pltpu.CompilerParams(needs_layout_passes=False),
        out_type=[jax.ShapeDtypeStruct((rows * ln,), F32),
                  jax.ShapeDtypeStruct((rows * ln,), jnp.int32)],
        scratch_types=[pltpu.VMEM((grp * nk,), F32),
                       pltpu.VMEM((grp * ln,), F32),
                       pltpu.VMEM((grp * ln,), jnp.int32)],
    )
    def keys_kernel(s_hbm, ov_hbm, oi_hbm, buf, outv, outi):
        wid = lax.axis_index("s") * SC_CORES + lax.axis_index("c")
        lane = lax.iota(jnp.int32, ln)

        def merge(a, b):
            ka, va = a
            kb, vb = b
            kbr = lax.rev(kb, (0,))
            vbr = lax.rev(vb, (0,))
            take_a = ka >= kbr
            kc = jnp.where(take_a, ka, kbr)
            vc = jnp.where(take_a, va, vbr)
            vi, ki = plsc.sort_key_val(vc, kc, descending=False)
            return plsc.sort_key_val(ki, vi, descending=True)

        def one_row(r):
            parts = []
            for j in range(nv):
                kj = buf[pl.ds(r * nk + j * ln, ln)]
                parts.append(plsc.sort_key_val(kj, lane + j * ln, descending=True))
            while len(parts) > 1:
                parts = [merge(parts[i], parts[i + 1]) for i in range(0, len(parts), 2)]
            outv[pl.ds(r * ln, ln)] = parts[0][0]
            outi[pl.ds(r * ln, ln)] = parts[0][1]

        def group_body(g, carry):
            row0 = wid * rows_per_w + g * grp
            pltpu.sync_copy(s_hbm.at[pl.ds((row_base + row0) * nk, grp * nk)], buf)

            def rows_body(i, c2):
                one_row(2 * i)
                one_row(2 * i + 1)
                return c2
            lax.fori_loop(0, grp // 2, rows_body, 0)
            pltpu.sync_copy(outv, ov_hbm.at[pl.ds(row0 * ln, grp * ln)])
            pltpu.sync_copy(outi, oi_hbm.at[pl.ds(row0 * ln, grp * ln)])
            return carry
        lax.fori_loop(0, n_groups, group_body, 0)

    ov, oi = keys_kernel(scores.reshape(-1))
    return ov.reshape(rows, ln), oi.reshape(rows, ln)


PEER_CAND_ROWS = 2 * SUBLANES + (SUBLANES - 1) * SUBLANES + SUBLANES


def _peer_select_kernel(stop_in_ref, itop_in_ref, idx_ref, gate_ref,
                        stop_ref, itop_ref, cand_ref, cidx_ref, best_ref, idxt_ref, gatet_ref):
    tb = stop_in_ref.shape[0]
    kk = PEER_TOPK
    assert kk == 2 * SUBLANES
    stop_ref[...] = stop_in_ref[...].T
    itop_ref[...] = itop_in_ref[...].astype(F32).T

    def extract(vals, row, n):
        m = jnp.max(vals, axis=0, keepdims=True)
        pos = jnp.min(jnp.where(vals == m, row, float(n)), axis=0, keepdims=True)
        return m, pos, jnp.where(row == pos, -jnp.inf, vals)

    row_c = lax.broadcasted_iota(jnp.int32, (PEER_CAND_ROWS, tb), 0).astype(F32)
    row_8 = lax.broadcasted_iota(jnp.int32, (SUBLANES, tb), 0)

    def build_candidates(h):
        r1 = 2 * h * kk
        r2 = r1 + kk
        cand_ref[h, 0:kk, :] = stop_ref[r1:r1 + 1, :] + stop_ref[r2:r2 + kk, :]
        cidx_ref[h, 0:kk, :] = itop_ref[r1:r1 + 1, :] * float(N_KEYS) + itop_ref[r2:r2 + kk, :]
        s2 = stop_ref[r2:r2 + SUBLANES, :]
        i2 = itop_ref[r2:r2 + SUBLANES, :]
        for a in range(1, SUBLANES):
            r0 = kk + (a - 1) * SUBLANES
            cand_ref[h, r0:r0 + SUBLANES, :] = jnp.where(
                row_8 < kk // (a + 1), stop_ref[r1 + a:r1 + a + 1, :] + s2, -jnp.inf)
            cidx_ref[h, r0:r0 + SUBLANES, :] = itop_ref[r1 + a:r1 + a + 1, :] * float(N_KEYS) + i2
        r0 = kk + (SUBLANES - 1) * SUBLANES
        cand_ref[h, r0:r0 + SUBLANES, :] = (stop_ref[r1 + SUBLANES:r1 + kk, :]
                                           + stop_ref[r2:r2 + 1, :])
        cidx_ref[h, r0:r0 + SUBLANES, :] = (itop_ref[r1 + SUBLANES:r1 + kk, :] * float(N_KEYS)
                                           + itop_ref[r2:r2 + 1, :])

    for h in range(PEER_HEADS):
        build_candidates(h)
    cvals = [cand_ref[h] for h in range(PEER_HEADS)]
    cidx = [cidx_ref[h] for h in range(PEER_HEADS)]
    for r in range(kk):
        for h in range(PEER_HEADS):
            m, pos, cvals[h] = extract(cvals[h], row_c, PEER_CAND_ROWS)
            best_ref[h, r:r + 1, :] = m
            idxt_ref[h * kk + r:h * kk + r + 1, :] = jnp.max(
                jnp.where(row_c == pos, cidx[h], -1.0), axis=0, keepdims=True)
    for h in range(PEER_HEADS):
        best = best_ref[h]
        e = jnp.exp(best - best[0:1, :])
        gatet_ref[h * kk:(h + 1) * kk, :] = e / jnp.sum(e, axis=0, keepdims=True)
    idx_ref[...] = idxt_ref[...].T.astype(jnp.int32)
    gate_ref[...] = gatet_ref[...].T


def _peer_select(stop, itop, tb):
    t, w = stop.shape
    hk = PEER_HEADS * PEER_TOPK
    kk = PEER_TOPK
    return pl.pallas_call(
        _peer_select_kernel,
        grid=(t // tb,),
        in_specs=[pl.BlockSpec((tb, w), lambda i: (i, 0)),
                  pl.BlockSpec((tb, w), lambda i: (i, 0))],
        out_specs=[pl.BlockSpec((tb, hk), lambda i: (i, 0)),
                   pl.BlockSpec((tb, hk), lambda i: (i, 0))],
        out_shape=[jax.ShapeDtypeStruct((t, hk), jnp.int32),
                   jax.ShapeDtypeStruct((t, hk), F32)],
        scratch_shapes=[pltpu.VMEM((w, tb), F32), pltpu.VMEM((w, tb), F32),
                        pltpu.VMEM((PEER_HEADS, PEER_CAND_ROWS, tb), F32),
                        pltpu.VMEM((PEER_HEADS, PEER_CAND_ROWS, tb), F32),
                        pltpu.VMEM((PEER_HEADS, kk, tb), F32),
                        pltpu.VMEM((hk, tb), F32), pltpu.VMEM((hk, tb), F32)],
        compiler_params=_params("parallel"),
        name="peer_select",
    )(stop, itop)


PEER_TOKEN_GROUP = 16
PEER_PARTS = 4


def _peer_gate_matrix(idx, gate, n_experts):
    t_total, hk = idx.shape
    ln = SC_LANES
    nu = hk // ln
    grp = PEER_TOKEN_GROUP
    tok_per_w = t_total // SC_WORKERS
    n_groups = tok_per_w // grp
    assert n_groups * grp * SC_WORKERS == t_total and grp % 2 == 0
    mesh = plsc.VectorSubcoreMesh(core_axis_name="c", subcore_axis_name="s",
                                  num_cores=SC_CORES, num_subcores=SC_SUBCORES)

    @functools.partial(
        pl.kernel, mesh=mesh,
        compiler_params=pltpu.CompilerParams(needs_layout_passes=False),
        out_type=jax.ShapeDtypeStruct((t_total, n_experts), F32),
        scratch_types=[
            pltpu.VMEM((grp * hk,), jnp.int32),
            pltpu.VMEM((grp * hk,), F32),
            pltpu.VMEM((n_experts,), F32),
            pltpu.VMEM((n_experts,), F32),
            pltpu.SemaphoreType.DMA((2,)),
        ],
    )
    def gate_kernel(idx_hbm, gate_hbm, g_hbm, idx_v, gate_v, row0_v, row1_v, sem):
        rows = (row0_v, row1_v)
        wid = lax.axis_index("s") * SC_CORES + lax.axis_index("c")
        base = wid * tok_per_w
        zero = jnp.zeros((ln,), F32)

        def zero_body(c, carry):
            row0_v[pl.ds(c * ln, ln)] = zero
            row1_v[pl.ds(c * ln, ln)] = zero
            return carry
        lax.fori_loop(0, n_experts // ln, zero_body, 0)

        def out_copy(tok, slot):
            return pltpu.make_async_copy(rows[slot], g_hbm.at[tok], sem.at[slot])

        def group_body(g, carry):
            tok0 = base + g * grp
            pltpu.sync_copy(idx_hbm.at[pl.ds(tok0 * hk, grp * hk)], idx_v)
            pltpu.sync_copy(gate_hbm.at[pl.ds(tok0 * hk, grp * hk)], gate_v)

            def pair_body(i, carry2):
                for slot in range(2):
                    t = i * 2 + slot
                    for u in range(nu):
                        sl = pl.ds(t * hk + u * ln, ln)
                        plsc.addupdate_scatter(rows[slot], [idx_v[sl]], gate_v[sl])
                    out_copy(tok0 + t, slot).start()
                for slot in range(2):
                    t = i * 2 + slot
                    out_copy(tok0 + t, slot).wait()
                    for u in range(nu):
                        plsc.store_scatter(rows[slot], [idx_v[pl.ds(t * hk + u * ln, ln)]], zero)
                return carry2
            lax.fori_loop(0, grp // 2, pair_body, 0)
            return carry
        lax.fori_loop(0, n_groups, group_body, 0)

    return gate_kernel(idx.reshape(-1), gate.reshape(-1))


def _peer_dense_kernel(h_ref, g_ref, u_ref, v_ref, x1_ref, mod_ref, ng_ref, *rest, normalize):
    o_ref = rest[-1]
    e = pl.program_id(1)
    s = _dot_nt(h_ref[...], u_ref[...])
    p = (jax.nn.gelu(s) * g_ref[...]).astype(v_ref.dtype)
    contrib = _dot(p, v_ref[...])

    @pl.when(e == 0)
    def _():
        o_ref[...] = contrib

    @pl.when(e > 0)
    def _():
        o_ref[...] += contrib

    @pl.when(e == pl.num_programs(1) - 1)
    def _():
        x2 = x1_ref[...] + mod_ref[0][5:6] * o_ref[...]
        o_ref[...] = _rms(x2) * ng_ref[...] if normalize else x2


def _peer_dense(h, gmat, u, v, x1, mod, ng, tb, eb, part, normalize, prev):
    t = gmat.shape[0]
    t_all, d = h.shape
    n_e = u.shape[0]
    blk0 = part * (t // tb)
    blk_per_batch = t_all // mod.shape[0] // tb
    tok_blk = pl.BlockSpec((tb, d), lambda i, e: (i + blk0, 0))
    in_specs = [tok_blk,
                pl.BlockSpec((tb, eb), lambda i, e: (i, e)),
                pl.BlockSpec((eb, d), lambda i, e: (e, 0)),
                pl.BlockSpec((eb, d), lambda i, e: (e, 0)),
                tok_blk,
                pl.BlockSpec((1, N_MOD, d), lambda i, e: ((i + blk0) // blk_per_batch, 0, 0)),
                pl.BlockSpec((1, d), lambda i, e: (0, 0))]
    args = [h, gmat, u, v, x1, mod, ng]
    aliases = {}
    if prev is not None:
        in_specs.append(pl.BlockSpec(memory_space=pl.ANY))
        args.append(prev)
        aliases = {len(args) - 1: 0}
    return pl.pallas_call(
        functools.partial(_peer_dense_kernel, normalize=normalize),
        grid=(t // tb, n_e // eb),
        in_specs=in_specs,
        out_specs=tok_blk,
        out_shape=jax.ShapeDtypeStruct((t_all, d), F32),
        input_output_aliases=aliases,
        compiler_params=pltpu.CompilerParams(dimension_semantics=("parallel", "arbitrary"),
                                             vmem_limit_bytes=PEER_DENSE_VMEM),
        name="peer_dense",
    )(*args)


def _split_w_in(w):
    gw = GDN_HEADS * GDN_HEAD_DIM
    sizes = [gw] * 4 + [GDN_HEADS] * 2 + [Q_LORA_RANK, KV_LORA_RANK, QK_ROPE_DIM]
    offs = [0]
    for sz in sizes:
        offs.append(offs[-1] + sz)
    parts = [w[:, offs[i]:offs[i + 1]] for i in range(len(sizes))]
    gq, gk, gv, gz, ga, gb, cq, ckv, kr = parts
    d = w.shape[0]
    half = QK_ROPE_DIM // 2
    zeros = lambda n: jnp.zeros((d, n), w.dtype)
    gdn = jnp.concatenate([gq, gk, gv, gz, ga, gb, zeros(LANES - 2 * GDN_HEADS)], axis=1)
    kr_sw = jnp.concatenate([kr[:, half:], kr[:, :half]], axis=1)
    mla = jnp.concatenate([cq, ckv, kr, zeros(LANES - QK_ROPE_DIM),
                           kr_sw, zeros(LANES - QK_ROPE_DIM)], axis=1)
    return jnp.concatenate([gdn, mla], axis=1).astype(BF16), gdn.shape[1], mla.shape[1]


def _split_w_uq(w):
    r = w.shape[0]
    half = QK_ROPE_DIM // 2
    hd = QK_NOPE_DIM + QK_ROPE_DIM
    main, swapped = [], []
    for h in range(MLA_HEADS):
        nope = w[:, h * hd:h * hd + QK_NOPE_DIM]
        rope = w[:, h * hd + QK_NOPE_DIM:(h + 1) * hd]
        main += [nope, rope, jnp.zeros((r, MLA_QK_PAD - hd), w.dtype)]
        swapped += [rope[:, half:], rope[:, :half], jnp.zeros((r, LANES - QK_ROPE_DIM), w.dtype)]
    return jnp.concatenate(main + swapped, axis=1).astype(BF16)


def _rotary_tables(positions):
    half = QK_ROPE_DIM // 2
    inv_freq = ROPE_THETA ** (-jnp.arange(half, dtype=F32) / half)
    ang = positions.astype(F32)[..., None] * inv_freq
    cos, sin = jnp.cos(ang), jnp.sin(ang)
    z = jnp.zeros(cos.shape[:-1] + (LANES - QK_ROPE_DIM,), F32)
    return (jnp.concatenate([cos, cos, z], axis=-1),
            jnp.concatenate([-sin, sin, z], axis=-1))


def _block(n, pref):
    return pref if n % pref == 0 else n


def kernel(x, c, positions, ln_mix_g, w_in, conv_w, a_log, dt_bias, gdn_norm_g, q_norm_g, w_uq, kv_norm_g, w_ukv, w_out, ln_ffn_g, w_pq, sub_keys, expert_u, expert_v, w_ada, b_ada, final_norm_g):
    bsz, s, d = x.shape
    depth = w_in.shape[0]
    cosr, sinr = _rotary_tables(positions)
    tm = _block(s, 512)
    for layer in range(depth):
        mod = _mod(c, w_ada[layer], b_ada[layer]).reshape(bsz, N_MOD, d)
        w_cat, gdn_cols, mla_cols = _split_w_in(w_in[layer])
        gdn_in, mla_in = _in_proj(x, mod, ln_mix_g[layer].reshape(1, d), w_cat, gdn_cols, mla_cols, tm)
        o_gdn = _gdn(gdn_in, conv_w[layer], a_log[layer], dt_bias[layer], gdn_norm_g[layer],
                     _block(s, 256))
        q, k, v = _mla_prep(mla_in, cosr, sinr, q_norm_g[layer].reshape(1, -1),
                            kv_norm_g[layer].reshape(1, -1), _split_w_uq(w_uq[layer]),
                            w_ukv[layer].astype(BF16), _block(s, 512))
        o_mla = _attn(q, k, v, _block(s, 1024))
        x1, h2, scores = _out_proj(o_gdn, o_mla, x, mod, w_out[layer].astype(BF16),
                                   ln_ffn_g[layer].reshape(1, d), w_pq[layer].astype(BF16),
                                   sub_keys[layer].astype(BF16), tm)
        n_experts = expert_u.shape[1]
        u16 = expert_u[layer].astype(BF16)
        v16 = expert_v[layer].astype(BF16)
        score_rows = scores.reshape(-1, N_KEYS)
        h22 = h2.reshape(bsz * s, d)
        nparts = PEER_PARTS if bsz % PEER_PARTS == 0 else 1
        tpart = bsz * s // nparts
        last = layer + 1 == depth
        out = None
        for part in range(nparts):
            stop, itop = _peer_key_topk(score_rows, part, nparts)
            idx, gate = _peer_select(stop.reshape(tpart, -1), itop.reshape(tpart, -1),
                                     _block(tpart, 256))
            gmat = _peer_gate_matrix(idx, gate, n_experts)
            out = _peer_dense(h22, gmat, u16, v16, x1.reshape(bsz * s, d), mod,
                              final_norm_g.reshape(1, d), _block(tpart, 1024),
                              _block(n_experts, 1024), part, last, out)
        x = out.reshape(bsz, s, d)
    return x
```

```python
import functools

import jax
import jax.numpy as jnp
from jax import lax
from jax.experimental import pallas as pl
from jax.experimental.pallas import tpu as pltpu
from jax.experimental.pallas import tpu_sc as plsc

GDN_HEADS = 4
GDN_HEAD_DIM = 128
CONV_WIDTH = 4
CHUNK = 64
MLA_HEADS = 4
QK_NOPE_DIM = 128
QK_ROPE_DIM = 64
V_HEAD_DIM = 128
Q_LORA_RANK = 384
KV_LORA_RANK = 256
ROPE_THETA = 10000.0
PEER_HEADS = 8
N_KEYS = 128
PEER_TOPK = 16
N_MOD = 6
EPS = 1e-6

LANES = 128
SUBLANES = 8
SC_CORES = 2
SC_SUBCORES = 16
SC_LANES = 16
SC_WORKERS = SC_CORES * SC_SUBCORES
VMEM_LIMIT = 48 * 1024 * 1024
PEER_DENSE_VMEM = 56 * 1024 * 1024

F32 = jnp.float32
BF16 = jnp.bfloat16
HI = lax.Precision.HIGHEST


def _dot(a, b, precision=None):
    return jnp.dot(a, b, preferred_element_type=F32, precision=precision)


def _dot_nt(a, b, precision=None):
    return lax.dot_general(a, b, (((1,), (1,)), ((), ())),
                           preferred_element_type=F32, precision=precision)


def _dot_tn(a, b, precision=None):
    return lax.dot_general(a, b, (((0,), (0,)), ((), ())),
                           preferred_element_type=F32, precision=precision)


def _rms(x):
    return x * lax.rsqrt(jnp.mean(x * x, axis=-1, keepdims=True) + EPS)


def _silu(x):
    return x * jax.nn.sigmoid(x)


def _params(*sem):
    return pltpu.CompilerParams(dimension_semantics=sem, vmem_limit_bytes=VMEM_LIMIT)


def _mod_kernel(c_ref, w_ref, b_ref, o_ref):
    o_ref[...] = _dot(_silu(c_ref[...]), w_ref[...], HI) + b_ref[...]


def _mod(c, w_ada, b_ada):
    bsz, d = c.shape
    n = w_ada.shape[1]
    return pl.pallas_call(
        _mod_kernel,
        grid=(n // d,),
        in_specs=[pl.BlockSpec((bsz, d), lambda j: (0, 0)),
                  pl.BlockSpec((d, d), lambda j: (0, j)),
                  pl.BlockSpec((1, d), lambda j: (0, j))],
        out_specs=pl.BlockSpec((bsz, d), lambda j: (0, j)),
        out_shape=jax.ShapeDtypeStruct((bsz, n), F32),
        compiler_params=_params("arbitrary"),
        name="mod",
    )(c, w_ada, b_ada.reshape(1, n))


def _inproj_kernel(x_ref, mod_ref, g_ref, w_ref, gdn_ref, mla_ref):
    m = mod_ref[0]
    h = _rms(x_ref[0]) * g_ref[...] * (1.0 + m[1:2]) + m[0:1]
    p = _dot(h.astype(BF16), w_ref[...])
    gw = gdn_ref.shape[-1]
    gdn_ref[0] = p[:, :gw]
    mla_ref[0] = p[:, gw:]


def _in_proj(x, mod, g, w_cat, gdn_cols, mla_cols, tm):
    bsz, s, d = x.shape
    return pl.pallas_call(
        _inproj_kernel,
        grid=(bsz, s // tm),
        in_specs=[pl.BlockSpec((1, tm, d), lambda b, i: (b, i, 0)),
                  pl.BlockSpec((1, N_MOD, d), lambda b, i: (b, 0, 0)),
                  pl.BlockSpec((1, d), lambda b, i: (0, 0)),
                  pl.BlockSpec((d, gdn_cols + mla_cols), lambda b, i: (0, 0))],
        out_specs=[pl.BlockSpec((1, tm, gdn_cols), lambda b, i: (b, i, 0)),
                   pl.BlockSpec((1, tm, mla_cols), lambda b, i: (b, i, 0))],
        out_shape=[jax.ShapeDtypeStruct((bsz, s, gdn_cols), F32),
                   jax.ShapeDtypeStruct((bsz, s, mla_cols), F32)],
        compiler_params=_params("parallel", "parallel"),
        name="in_proj",
    )(x, mod, g, w_cat)


def _gdn_kernel(x_ref, cw_ref, alog_ref, dtb_ref, ng_ref, o_ref,
                state_ref, tail_ref, buf_ref, vnew_ref):
    sb = x_ref.shape[1]
    nchunk = sb // CHUNK
    hdim = GDN_HEAD_DIM
    gw = GDN_HEADS * hdim

    @pl.when(pl.program_id(1) == 0)
    def _():
        state_ref[...] = jnp.zeros_like(state_ref)
        tail_ref[...] = jnp.zeros_like(tail_ref)

    def conv_silu(slot):
        x = x_ref[0, :, slot * hdim:(slot + 1) * hdim]
        buf_ref[slot, 0:SUBLANES, :] = tail_ref[slot]
        buf_ref[slot, SUBLANES:SUBLANES + sb, :] = x
        tail_ref[slot] = x[sb - SUBLANES:sb, :]
        cw = cw_ref[slot]
        y = cw[CONV_WIDTH - 1:CONV_WIDTH] * x
        for j in range(CONV_WIDTH - 1):
            off = SUBLANES - (CONV_WIDTH - 1) + j
            y = y + cw[j:j + 1] * buf_ref[slot, off:off + sb, :]
        return _silu(y)

    def l2n(x):
        return x * lax.rsqrt(jnp.sum(x * x, axis=-1, keepdims=True) + EPS)

    ri = lax.broadcasted_iota(jnp.int32, (sb, sb), 0)
    ci = lax.broadcasted_iota(jnp.int32, (sb, sb), 1)
    same = (ri // CHUNK) == (ci // CHUNK)
    incl = same & (ci <= ri)
    strict = same & (ci < ri)
    eye = (ri == ci).astype(F32)
    levels = []
    bs = 1
    while bs < CHUNK:
        levels.append(((ri // bs) % 2 == 1) & ((ci // bs) % 2 == 0)
                      & ((ri // (2 * bs)) == (ci // (2 * bs))))
        bs *= 2
    rowmod = lax.broadcasted_iota(jnp.int32, (sb, hdim), 0) % CHUNK
    ab = x_ref[0, :, 4 * gw:4 * gw + LANES]

    heads = range(GDN_HEADS)
    q = [l2n(conv_silu(h)) * (hdim ** -0.5) for h in heads]
    k = [l2n(conv_silu(GDN_HEADS + h)) for h in heads]
    v = [conv_silu(2 * GDN_HEADS + h) for h in heads]
    beta = [jax.nn.sigmoid(ab[:, GDN_HEADS + h:GDN_HEADS + h + 1]) for h in heads]

    gc = [-jnp.exp(alog_ref[h]) * jax.nn.softplus(ab[:, h:h + 1] + dtb_ref[h]) for h in heads]
    sh = 1
    while sh < CHUNK:
        gc = [g + jnp.where(rowmod >= sh, pltpu.roll(g, sh, axis=0), 0.0) for g in gc]
        sh *= 2
    gcl = [jnp.concatenate(
        [jnp.broadcast_to(g[(n + 1) * CHUNK - 1:(n + 1) * CHUNK, :], (CHUNK, hdim))
         for n in range(nchunk)], axis=0) for g in gc]
    decay = []
    for g in gc:
        gc_row = jnp.broadcast_to(g.T[0:1, :], (sb, sb))
        decay.append(jnp.where(incl, jnp.exp(jnp.where(incl, g[:, 0:1] - gc_row, 0.0)), 0.0))

    kb = [k[h] * beta[h] for h in heads]
    k16 = [k[h].astype(BF16) for h in heads]
    mmat = [jnp.where(strict, _dot_nt(kb[h].astype(BF16), k16[h]) * decay[h], 0.0) for h in heads]
    attn = [(_dot_nt(q[h].astype(BF16), k16[h]) * decay[h]).astype(BF16) for h in heads]
    tinv = [eye - jnp.where(levels[0], m, 0.0) for m in mmat]
    for off in levels[1:]:
        t16 = [t.astype(BF16) for t in tinv]
        ta = [_dot(t16[h], jnp.where(off, mmat[h], 0.0).astype(BF16)).astype(BF16) for h in heads]
        tinv = [tinv[h] - _dot(ta[h], t16[h]) for h in heads]

    egc = [jnp.exp(g) for g in gc]
    rhs = [jnp.concatenate([v[h] * beta[h], kb[h] * egc[h]], axis=1) for h in heads]
    sol = [rhs[h] + _dot((tinv[h] - eye).astype(BF16), rhs[h].astype(BF16)) for h in heads]
    u = [x[:, :hdim] for x in sol]
    w16 = [x[:, hdim:].astype(BF16) for x in sol]
    qd16 = [(q[h] * egc[h]).astype(BF16) for h in heads]
    kd16 = [(k[h] * jnp.exp(gcl[h] - gc[h])).astype(BF16) for h in heads]
    g_last = [jnp.exp(x) for x in gcl]

    st = [state_ref[h] for h in heads]
    o_inter = [[] for _ in heads]
    for n in range(nchunk):
        lo = n * CHUNK
        for h in heads:
            st16 = st[h].astype(BF16)
            v_new = u[h][lo:lo + CHUNK] - _dot(w16[h][lo:lo + CHUNK], st16)
            vnew_ref[h, lo:lo + CHUNK, :] = v_new
            o_inter[h].append(_dot(qd16[h][lo:lo + CHUNK], st16))
            st[h] = st[h] * g_last[h][lo:lo + 1, :] + _dot_tn(kd16[h][lo:lo + CHUNK],
                                                             v_new.astype(BF16))
    for h in heads:
        state_ref[h] = st[h]
        o = jnp.concatenate(o_inter[h], axis=0) + _dot(attn[h], vnew_ref[h].astype(BF16))
        z = x_ref[0, :, 3 * gw + h * hdim:3 * gw + (h + 1) * hdim]
        o = _rms(o) * ng_ref[...] * _silu(z)
        o_ref[0, :, h * hdim:(h + 1) * hdim] = o.astype(o_ref.dtype)


def _gdn(gdn_in, conv_w, a_log, dt_bias, norm_g, sb):
    bsz, s, cols = gdn_in.shape
    hdim = GDN_HEAD_DIM
    nh = GDN_HEADS
    cw = conv_w.reshape(CONV_WIDTH, 3 * nh, hdim).transpose(1, 0, 2)
    alog = jnp.broadcast_to(a_log.reshape(nh, 1, 1), (nh, 1, hdim)).astype(F32)
    dtb = jnp.broadcast_to(dt_bias.reshape(nh, 1, 1), (nh, 1, hdim)).astype(F32)
    full = lambda shape: pl.BlockSpec(shape, lambda b, i: (0,) * len(shape))
    return pl.pallas_call(
        _gdn_kernel,
        grid=(bsz, s // sb),
        in_specs=[pl.BlockSpec((1, sb, cols), lambda b, i: (b, i, 0)),
                  full(cw.shape), full(alog.shape), full(dtb.shape), full((1, hdim))],
        out_specs=pl.BlockSpec((1, sb, nh * hdim), lambda b, i: (b, i, 0)),
        out_shape=jax.ShapeDtypeStruct((bsz, s, nh * hdim), BF16),
        scratch_shapes=[pltpu.VMEM((nh, hdim, hdim), F32),
                        pltpu.VMEM((3 * nh, SUBLANES, hdim), F32),
                        pltpu.VMEM((3 * nh, sb + SUBLANES, hdim), F32),
                        pltpu.VMEM((nh, sb, hdim), F32)],
        compiler_params=_params("parallel", "arbitrary"),
        name="gdn",
    )(gdn_in, cw, alog, dtb, norm_g.reshape(1, hdim))


MLA_QK_PAD = 256


def _mla_prep_kernel(m_ref, cos_ref, sin_ref, gq_ref, gkv_ref, wq_ref, wkv_ref,
                     q_ref, k_ref, v_ref):
    m = m_ref[0]
    cosr = cos_ref[0]
    sinr = sin_ref[0]
    cq = m[:, :Q_LORA_RANK]
    ckv = m[:, Q_LORA_RANK:Q_LORA_RANK + KV_LORA_RANK]
    o = Q_LORA_RANK + KV_LORA_RANK
    kr = m[:, o:o + LANES]
    krs = m[:, o + LANES:o + 2 * LANES]
    scale = (QK_NOPE_DIM + QK_ROPE_DIM) ** -0.5
    qa = _dot((_rms(cq) * gq_ref[...]).astype(BF16), wq_ref[...])
    kva = _dot((_rms(ckv) * gkv_ref[...]).astype(BF16), wkv_ref[...])
    k_rope = (kr * cosr + krs * sinr).astype(k_ref.dtype)
    sw0 = MLA_HEADS * MLA_QK_PAD
    for h in range(MLA_HEADS):
        b0 = h * MLA_QK_PAD
        rope = qa[:, b0 + LANES:b0 + 2 * LANES] * cosr + qa[:, sw0 + h * LANES:sw0 + (h + 1) * LANES] * sinr
        q_ref[0, h, :, 0:LANES] = (qa[:, b0:b0 + LANES] * scale).astype(q_ref.dtype)
        q_ref[0, h, :, LANES:2 * LANES] = (rope * scale).astype(q_ref.dtype)
        c0 = h * (QK_NOPE_DIM + V_HEAD_DIM)
        k_ref[0, h, :, 0:LANES] = kva[:, c0:c0 + QK_NOPE_DIM].astype(k_ref.dtype)
        k_ref[0, h, :, LANES:2 * LANES] = k_rope
        v_ref[0, h] = kva[:, c0 + QK_NOPE_DIM:c0 + QK_NOPE_DIM + V_HEAD_DIM].astype(v_ref.dtype)


def _mla_prep(mla_in, cosr, sinr, gq, gkv, wq, wkv, tm):
    bsz, s, mc = mla_in.shape
    nh = MLA_HEADS
    return pl.pallas_call(
        _mla_prep_kernel,
        grid=(bsz, s // tm),
        in_specs=[pl.BlockSpec((1, tm, mc), lambda b, i: (b, i, 0)),
                  pl.BlockSpec((1, tm, LANES), lambda b, i: (b, i, 0)),
                  pl.BlockSpec((1, tm, LANES), lambda b, i: (b, i, 0)),
                  pl.BlockSpec((1, Q_LORA_RANK), lambda b, i: (0, 0)),
                  pl.BlockSpec((1, KV_LORA_RANK), lambda b, i: (0, 0)),
                  pl.BlockSpec(wq.shape, lambda b, i: (0, 0)),
                  pl.BlockSpec(wkv.shape, lambda b, i: (0, 0))],
        out_specs=[pl.BlockSpec((1, nh, tm, MLA_QK_PAD), lambda b, i: (b, 0, i, 0)),
                   pl.BlockSpec((1, nh, tm, MLA_QK_PAD), lambda b, i: (b, 0, i, 0)),
                   pl.BlockSpec((1, nh, tm, V_HEAD_DIM), lambda b, i: (b, 0, i, 0))],
        out_shape=[jax.ShapeDtypeStruct((bsz, nh, s, MLA_QK_PAD), BF16),
                   jax.ShapeDtypeStruct((bsz, nh, s, MLA_QK_PAD), BF16),
                   jax.ShapeDtypeStruct((bsz, nh, s, V_HEAD_DIM), BF16)],
        compiler_params=_params("parallel", "parallel"),
        name="mla_prep",
    )(mla_in, cosr, sinr, gq, gkv, wq, wkv)


def _fold_lanes(x, op):
    parts = [x[:, c * LANES:(c + 1) * LANES] for c in range(x.shape[1] // LANES)]
    while len(parts) > 1:
        parts = [op(parts[i], parts[i + 1]) for i in range(0, len(parts), 2)]
    return parts[0]


def _attn_kernel(q_ref, k_ref, v_ref, o_ref, s_ref, m_ref, l_ref, acc_ref):
    i = pl.program_id(2)
    tq = q_ref.shape[2]
    m_ref[...] = jnp.full_like(m_ref, -jnp.inf)
    l_ref[...] = jnp.zeros_like(l_ref)
    acc_ref[...] = jnp.zeros_like(acc_ref)

    def scores(j, slot):
        start = pl.multiple_of(j * tq, tq)
        s_ref[slot] = _dot_nt(q_ref[0, 0], k_ref[0, 0, pl.ds(start, tq), :])

    def consume(j, slot, diagonal):
        start = pl.multiple_of(j * tq, tq)
        s = s_ref[slot]
        if diagonal:
            row = lax.broadcasted_iota(jnp.int32, s.shape, 0)
            col = lax.broadcasted_iota(jnp.int32, s.shape, 1)
            s = jnp.where(col <= row, s, -jnp.inf)
        m_prev = m_ref[...]
        m_new = jnp.maximum(m_prev, jnp.max(_fold_lanes(s, jnp.maximum), axis=-1, keepdims=True))
        alpha = jnp.exp(m_prev - m_new)
        p = jnp.exp(s - m_new)
        l_ref[...] = alpha * l_ref[...] + jnp.sum(_fold_lanes(p, jnp.add), axis=-1, keepdims=True)
        acc_ref[...] = alpha * acc_ref[...] + _dot(p.astype(v_ref.dtype),
                                                   v_ref[0, 0, pl.ds(start, tq), :])
        m_ref[...] = m_new

    scores(0, 0)

    def pair(t, carry):
        scores(2 * t + 1, 1)
        consume(2 * t, 0, False)
        scores(2 * t + 2, 0)
        consume(2 * t + 1, 1, False)
        return carry
    lax.fori_loop(0, i // 2, pair, 0)

    @pl.when(i % 2 == 1)
    def _():
        scores(i, 1)
        consume(i - 1, 0, False)
        consume(i, 1, True)

    @pl.when(i % 2 == 0)
    def _():
        consume(i, 0, True)

    o_ref[0] = (acc_ref[...] / l_ref[...]).astype(o_ref.dtype)


def _attn(q, k, v, tq):
    bsz, nh, s, dq = q.shape
    dv = v.shape[-1]
    return pl.pallas_call(
        _attn_kernel,
        grid=(bsz, nh, s // tq),
        in_specs=[pl.BlockSpec((1, 1, tq, dq), lambda b, h, i: (b, h, i, 0)),
                  pl.BlockSpec((1, 1, s, dq), lambda b, h, i: (b, h, 0, 0)),
                  pl.BlockSpec((1, 1, s, dv), lambda b, h, i: (b, h, 0, 0))],
        out_specs=pl.BlockSpec((1, tq, dv), lambda b, h, i: (b, i, h)),
        out_shape=jax.ShapeDtypeStruct((bsz, s, nh * dv), BF16),
        scratch_shapes=[pltpu.VMEM((2, tq, tq), F32),
                        pltpu.VMEM((tq, 1), F32), pltpu.VMEM((tq, 1), F32),
                        pltpu.VMEM((tq, dv), F32)],
        compiler_params=_params("parallel", "parallel", "arbitrary"),
        name="attn",
    )(q, k, v)


def _outproj_kernel(og_ref, om_ref, x_ref, mod_ref, wo_ref, g_ref, wpq_ref, keys_ref,
                    x1_ref, h2_ref, sc_ref):
    m = mod_ref[0]
    gw = og_ref.shape[-1]
    mixed = _dot(og_ref[0], wo_ref[0:gw, :]) + _dot(om_ref[0], wo_ref[gw:, :])
    x1 = x_ref[0] + m[2:3] * mixed
    h2 = _rms(x1) * g_ref[...] * (1.0 + m[4:5]) + m[3:4]
    x1_ref[0] = x1
    h2_ref[0] = h2.astype(h2_ref.dtype)
    qp = _dot(h2.astype(BF16), wpq_ref[...]).astype(BF16)
    for h in range(PEER_HEADS):
        for p in range(2):
            c0 = (2 * h + p) * N_KEYS
            sc_ref[0, :, c0:c0 + N_KEYS] = _dot_nt(qp[:, c0:c0 + N_KEYS], keys_ref[p, h])


def _out_proj(o_gdn, o_mla, x, mod, w_out, g, w_pq, keys, tm):
    bsz, s, d = x.shape
    gw = o_gdn.shape[-1]
    mw = o_mla.shape[-1]
    nq = w_pq.shape[1]
    return pl.pallas_call(
        _outproj_kernel,
        grid=(bsz, s // tm),
        in_specs=[pl.BlockSpec((1, tm, gw), lambda b, i: (b, i, 0)),
                  pl.BlockSpec((1, tm, mw), lambda b, i: (b, i, 0)),
                  pl.BlockSpec((1, tm, d), lambda b, i: (b, i, 0)),
                  pl.BlockSpec((1, N_MOD, d), lambda b, i: (b, 0, 0)),
                  pl.BlockSpec((gw + mw, d), lambda b, i: (0, 0)),
                  pl.BlockSpec((1, d), lambda b, i: (0, 0)),
                  pl.BlockSpec((d, nq), lambda b, i: (0, 0)),
                  pl.BlockSpec(keys.shape, lambda b, i: (0, 0, 0, 0))],
        out_specs=[pl.BlockSpec((1, tm, d), lambda b, i: (b, i, 0)),
                   pl.BlockSpec((1, tm, d), lambda b, i: (b, i, 0)),
                   pl.BlockSpec((1, tm, nq), lambda b, i: (b, i, 0))],
        out_shape=[jax.ShapeDtypeStruct((bsz, s, d), F32),
                   jax.ShapeDtypeStruct((bsz, s, d), BF16),
                   jax.ShapeDtypeStruct((bsz, s, nq), F32)],
        compiler_params=_params("parallel", "parallel"),
        name="out_proj",
    )(o_gdn, o_mla, x, mod, w_out, g, w_pq, keys)


PEER_SORT_TOKENS = 8


def _peer_key_topk(scores, part, nparts, after=None):
    t_all, width = scores.shape
    nk = N_KEYS
    ln = SC_LANES
    assert PEER_TOPK == ln
    nv = nk // ln
    rpt = width // nk
    grp = PEER_SORT_TOKENS
    toks = t_all // nparts
    tok_per_w = toks // SC_WORKERS
    n_groups = tok_per_w // grp
    assert n_groups * grp * SC_WORKERS * nparts == t_all and rpt % 2 == 0
    tok_base = part * toks
    rows = toks * rpt
    mesh = plsc.VectorSubcoreMesh(core_axis_name="c", subcore_axis_name="s",
                                  num_cores=SC_CORES, num_subcores=SC_SUBCORES)

    @functools.partial(
        pl.kernel, mesh=mesh,
        compiler_params=pltpu.CompilerParams(needs_layout_passes=False),
        out_type=[jax.ShapeDtypeStruct((rows * ln,), F32),
                  jax.ShapeDtypeStruct((rows * ln,), jnp.int32)],
        scratch_types=[pltpu.VMEM((grp, width), F32),
                       pltpu.VMEM((grp * rpt * ln,), F32),
                       pltpu.VMEM((grp * rpt * ln,), jnp.int32)],
    )
    def keys_kernel(s_hbm, *rest):
        ov_hbm, oi_hbm, buf, outv, outi = rest[-5:]
        wid = lax.axis_index("s") * SC_CORES + lax.axis_index("c")
        lane = lax.iota(jnp.int32, ln)

        def merge(a, b):
            ka, va = a
            kb, vb = b
            kbr = lax.rev(kb, (0,))
            vbr = lax.rev(vb, (0,))
            take_a = ka >= kbr
            kc = jnp.where(take_a, ka, kbr)
            vc = jnp.where(take_a, va, vbr)
            vi, ki = plsc.sort_key_val(vc, kc, descending=False)
            return plsc.sort_key_val(ki, vi, descending=True)

        def one_row(t, row):
            parts = []
            for j in range(nv):
                kj = buf[t, pl.ds(row * nk + j * ln, ln)]
                parts.append(plsc.sort_key_val(kj, lane + j * ln, descending=True))
            while len(parts) > 1:
                parts = [merge(parts[i], parts[i + 1]) for i in range(0, len(parts), 2)]
            out = (t * rpt + row) * ln
            outv[pl.ds(out, ln)] = parts[0][0]
            outi[pl.ds(out, ln)] = parts[0][1]

        def group_body(g, carry):
            tok0 = wid * tok_per_w + g * grp
            pltpu.sync_copy(s_hbm.at[pl.ds(tok_base + tok0, grp)], buf)

            def rows_body(i, c2):
                t = i // (rpt // 2)
                row = (i % (rpt // 2)) * 2
                one_row(t, row)
                one_row(t, row + 1)
                return c2
            lax.fori_loop(0, grp * rpt // 2, rows_body, 0)
            pltpu.sync_copy(outv, ov_hbm.at[pl.ds(tok0 * rpt * ln, grp * rpt * ln)])
            pltpu.sync_copy(outi, oi_hbm.at[pl.ds(tok0 * rpt * ln, grp * rpt * ln)])
            return carry
        lax.fori_loop(0, n_groups, group_body, 0)

    args = (scores,) if after is None else (scores, after)
    ov, oi = keys_kernel(*args)
    return ov.reshape(toks, rpt * ln), oi.reshape(toks, rpt * ln)


PEER_CAND_ROWS = 2 * SUBLANES + (SUBLANES - 1) * SUBLANES + SUBLANES


def _peer_select_kernel(stop_in_ref, itop_in_ref, idx_ref, gate_ref,
                        stop_ref, itop_ref, cand_ref, cidx_ref, best_ref, idxt_ref, gatet_ref):
    tb = stop_in_ref.shape[0]
    kk = PEER_TOPK
    assert kk == 2 * SUBLANES
    stop_ref[...] = stop_in_ref[...].T
    itop_ref[...] = itop_in_ref[...].astype(F32).T

    def extract(vals, row, n):
        m = jnp.max(vals, axis=0, keepdims=True)
        pos = jnp.min(jnp.where(vals == m, row, float(n)), axis=0, keepdims=True)
        return m, pos, jnp.where(row == pos, -jnp.inf, vals)

    row_c = lax.broadcasted_iota(jnp.int32, (PEER_CAND_ROWS, tb), 0).astype(F32)
    row_8 = lax.broadcasted_iota(jnp.int32, (SUBLANES, tb), 0)

    def build_candidates(h):
        r1 = 2 * h * kk
        r2 = r1 + kk
        cand_ref[h, 0:kk, :] = stop_ref[r1:r1 + 1, :] + stop_ref[r2:r2 + kk, :]
        cidx_ref[h, 0:kk, :] = itop_ref[r1:r1 + 1, :] * float(N_KEYS) + itop_ref[r2:r2 + kk, :]
        s2 = stop_ref[r2:r2 + SUBLANES, :]
        i2 = itop_ref[r2:r2 + SUBLANES, :]
        for a in range(1, SUBLANES):
            r0 = kk + (a - 1) * SUBLANES
            cand_ref[h, r0:r0 + SUBLANES, :] = jnp.where(
                row_8 < kk // (a + 1), stop_ref[r1 + a:r1 + a + 1, :] + s2, -jnp.inf)
            cidx_ref[h, r0:r0 + SUBLANES, :] = itop_ref[r1 + a:r1 + a + 1, :] * float(N_KEYS) + i2
        r0 = kk + (SUBLANES - 1) * SUBLANES
        cand_ref[h, r0:r0 + SUBLANES, :] = (stop_ref[r1 + SUBLANES:r1 + kk, :]
                                           + stop_ref[r2:r2 + 1, :])
        cidx_ref[h, r0:r0 + SUBLANES, :] = (itop_ref[r1 + SUBLANES:r1 + kk, :] * float(N_KEYS)
                                           + itop_ref[r2:r2 + 1, :])

    for h in range(PEER_HEADS):
        build_candidates(h)
    cvals = [cand_ref[h] for h in range(PEER_HEADS)]
    cidx = [cidx_ref[h] for h in range(PEER_HEADS)]
    for r in range(kk):
        for h in range(PEER_HEADS):
            m, pos, cvals[h] = extract(cvals[h], row_c, PEER_CAND_ROWS)
            best_ref[h, r:r + 1, :] = m
            idxt_ref[h * kk + r:h * kk + r + 1, :] = jnp.max(
                jnp.where(row_c == pos, cidx[h], -1.0), axis=0, keepdims=True)
    for h in range(PEER_HEADS):
        best = best_ref[h]
        e = jnp.exp(best - best[0:1, :])
        gatet_ref[h * kk:(h + 1) * kk, :] = e / jnp.sum(e, axis=0, keepdims=True)
    idx_ref[...] = idxt_ref[...].T.astype(jnp.int32)
    gate_ref[...] = gatet_ref[...].T


def _peer_select(stop, itop, tb):
    t, w = stop.shape
    hk = PEER_HEADS * PEER_TOPK
    kk = PEER_TOPK
    return pl.pallas_call(
        _peer_select_kernel,
        grid=(t // tb,),
        in_specs=[pl.BlockSpec((tb, w), lambda i: (i, 0)),
                  pl.BlockSpec((tb, w), lambda i: (i, 0))],
        out_specs=[pl.BlockSpec((tb, hk), lambda i: (i, 0)),
                   pl.BlockSpec((tb, hk), lambda i: (i, 0))],
        out_shape=[jax.ShapeDtypeStruct((t, hk), jnp.int32),
                   jax.ShapeDtypeStruct((t, hk), F32)],
        scratch_shapes=[pltpu.VMEM((w, tb), F32), pltpu.VMEM((w, tb), F32),
                        pltpu.VMEM((PEER_HEADS, PEER_CAND_ROWS, tb), F32),
                        pltpu.VMEM((PEER_HEADS, PEER_CAND_ROWS, tb), F32),
                        pltpu.VMEM((PEER_HEADS, kk, tb), F32),
                        pltpu.VMEM((hk, tb), F32), pltpu.VMEM((hk, tb), F32)],
        compiler_params=_params("parallel"),
        name="peer_select",
    )(stop, itop)


PEER_TOKEN_GROUP = 16
PEER_PARTS = 4


def _peer_gate_matrix(idx, gate, n_experts):
    t_total, hk = idx.shape
    ln = SC_LANES
    nu = hk // ln
    grp = PEER_TOKEN_GROUP
    tok_per_w = t_total // SC_WORKERS
    n_groups = tok_per_w // grp
    assert n_groups * grp * SC_WORKERS == t_total and grp % 2 == 0
    mesh = plsc.VectorSubcoreMesh(core_axis_name="c", subcore_axis_name="s",
                                  num_cores=SC_CORES, num_subcores=SC_SUBCORES)

    @functools.partial(
        pl.kernel, mesh=mesh,
        compiler_params=pltpu.CompilerParams(needs_layout_passes=False),
        out_type=jax.ShapeDtypeStruct((t_total, n_experts), F32),
        scratch_types=[
            pltpu.VMEM((grp * hk,), jnp.int32),
            pltpu.VMEM((grp * hk,), F32),
            pltpu.VMEM((n_experts,), F32),
            pltpu.VMEM((n_experts,), F32),
            pltpu.SemaphoreType.DMA((2,)),
        ],
    )
    def gate_kernel(idx_hbm, gate_hbm, g_hbm, idx_v, gate_v, row0_v, row1_v, sem):
        rows = (row0_v, row1_v)
        wid = lax.axis_index("s") * SC_CORES + lax.axis_index("c")
        base = wid * tok_per_w
        zero = jnp.zeros((ln,), F32)

        def zero_body(c, carry):
            row0_v[pl.ds(c * ln, ln)] = zero
            row1_v[pl.ds(c * ln, ln)] = zero
            return carry
        lax.fori_loop(0, n_experts // ln, zero_body, 0)

        def out_copy(tok, slot):
            return pltpu.make_async_copy(rows[slot], g_hbm.at[tok], sem.at[slot])

        def group_body(g, carry):
            tok0 = base + g * grp
            pltpu.sync_copy(idx_hbm.at[pl.ds(tok0 * hk, grp * hk)], idx_v)
            pltpu.sync_copy(gate_hbm.at[pl.ds(tok0 * hk, grp * hk)], gate_v)

            def pair_body(i, carry2):
                for slot in range(2):
                    t = i * 2 + slot
                    for u in range(nu):
                        sl = pl.ds(t * hk + u * ln, ln)
                        plsc.addupdate_scatter(rows[slot], [idx_v[sl]], gate_v[sl])
                    out_copy(tok0 + t, slot).start()
                for slot in range(2):
                    t = i * 2 + slot
                    out_copy(tok0 + t, slot).wait()
                    for u in range(nu):
                        plsc.store_scatter(rows[slot], [idx_v[pl.ds(t * hk + u * ln, ln)]], zero)
                return carry2
            lax.fori_loop(0, grp // 2, pair_body, 0)
            return carry
        lax.fori_loop(0, n_groups, group_body, 0)

    return gate_kernel(idx.reshape(-1), gate.reshape(-1))


def _peer_dense_kernel(h_ref, g_ref, u_ref, v_ref, x1_ref, mod_ref, ng_ref, *rest, normalize):
    o_ref = rest[-1]
    e = pl.program_id(1)
    s = _dot_nt(h_ref[...], u_ref[...])
    p = (jax.nn.gelu(s) * g_ref[...]).astype(v_ref.dtype)
    contrib = _dot(p, v_ref[...])

    @pl.when(e == 0)
    def _():
        o_ref[...] = contrib

    @pl.when(e > 0)
    def _():
        o_ref[...] += contrib

    @pl.when(e == pl.num_programs(1) - 1)
    def _():
        x2 = x1_ref[...] + mod_ref[0][5:6] * o_ref[...]
        o_ref[...] = _rms(x2) * ng_ref[...] if normalize else x2


def _peer_dense(h, gmat, u, v, x1, mod, ng, tb, eb, part, normalize, prev, after=None):
    t = gmat.shape[0]
    t_all, d = h.shape
    n_e = u.shape[0]
    blk0 = part * (t // tb)
    blk_per_batch = t_all // mod.shape[0] // tb
    tok_blk = pl.BlockSpec((tb, d), lambda i, e: (i + blk0, 0))
    in_specs = [tok_blk,
                pl.BlockSpec((tb, eb), lambda i, e: (i, e)),
                pl.BlockSpec((eb, d), lambda i, e: (e, 0)),
                pl.BlockSpec((eb, d), lambda i, e: (e, 0)),
                tok_blk,
                pl.BlockSpec((1, N_MOD, d), lambda i, e: ((i + blk0) // blk_per_batch, 0, 0)),
                pl.BlockSpec((1, d), lambda i, e: (0, 0))]
    args = [h, gmat, u, v, x1, mod, ng]
    aliases = {}
    if prev is not None:
        in_specs.append(pl.BlockSpec(memory_space=pl.ANY))
        args.append(prev)
        aliases = {len(args) - 1: 0}
    if after is not None:
        in_specs.append(pl.BlockSpec(memory_space=pl.ANY))
        args.append(after)
    return pl.pallas_call(
        functools.partial(_peer_dense_kernel, normalize=normalize),
        grid=(t // tb, n_e // eb),
        in_specs=in_specs,
        out_specs=tok_blk,
        out_shape=jax.ShapeDtypeStruct((t_all, d), F32),
        input_output_aliases=aliases,
        compiler_params=pltpu.CompilerParams(dimension_semantics=("parallel", "arbitrary"),
                                             vmem_limit_bytes=PEER_DENSE_VMEM),
        name="peer_dense",
    )(*args)


def _split_w_in(w):
    gw = GDN_HEADS * GDN_HEAD_DIM
    sizes = [gw] * 4 + [GDN_HEADS] * 2 + [Q_LORA_RANK, KV_LORA_RANK, QK_ROPE_DIM]
    offs = [0]
    for sz in sizes:
        offs.append(offs[-1] + sz)
    parts = [w[:, offs[i]:offs[i + 1]] for i in range(len(sizes))]
    gq, gk, gv, gz, ga, gb, cq, ckv, kr = parts
    d = w.shape[0]
    half = QK_ROPE_DIM // 2
    zeros = lambda n: jnp.zeros((d, n), w.dtype)
    gdn = jnp.concatenate([gq, gk, gv, gz, ga, gb, zeros(LANES - 2 * GDN_HEADS)], axis=1)
    kr_sw = jnp.concatenate([kr[:, half:], kr[:, :half]], axis=1)
    mla = jnp.concatenate([cq, ckv, kr, zeros(LANES - QK_ROPE_DIM),
                           kr_sw, zeros(LANES - QK_ROPE_DIM)], axis=1)
    return jnp.concatenate([gdn, mla], axis=1).astype(BF16), gdn.shape[1], mla.shape[1]


def _split_w_uq(w):
    r = w.shape[0]
    half = QK_ROPE_DIM // 2
    hd = QK_NOPE_DIM + QK_ROPE_DIM
    main, swapped = [], []
    for h in range(MLA_HEADS):
        nope = w[:, h * hd:h * hd + QK_NOPE_DIM]
        rope = w[:, h * hd + QK_NOPE_DIM:(h + 1) * hd]
        main += [nope, rope, jnp.zeros((r, MLA_QK_PAD - hd), w.dtype)]
        swapped += [rope[:, half:], rope[:, :half], jnp.zeros((r, LANES - QK_ROPE_DIM), w.dtype)]
    return jnp.concatenate(main + swapped, axis=1).astype(BF16)


def _rotary_tables(positions):
    half = QK_ROPE_DIM // 2
    inv_freq = ROPE_THETA ** (-jnp.arange(half, dtype=F32) / half)
    ang = positions.astype(F32)[..., None] * inv_freq
    cos, sin = jnp.cos(ang), jnp.sin(ang)
    z = jnp.zeros(cos.shape[:-1] + (LANES - QK_ROPE_DIM,), F32)
    return (jnp.concatenate([cos, cos, z], axis=-1),
            jnp.concatenate([-sin, sin, z], axis=-1))


def _block(n, pref):
    return pref if n % pref == 0 else n


def kernel(x, c, positions, ln_mix_g, w_in, conv_w, a_log, dt_bias, gdn_norm_g, q_norm_g, w_uq, kv_norm_g, w_ukv, w_out, ln_ffn_g, w_pq, sub_keys, expert_u, expert_v, w_ada, b_ada, final_norm_g):
    bsz, s, d = x.shape
    depth = w_in.shape[0]
    cosr, sinr = _rotary_tables(positions)
    tm = _block(s, 512)
    for layer in range(depth):
        mod = _mod(c, w_ada[layer], b_ada[layer]).reshape(bsz, N_MOD, d)
        w_cat, gdn_cols, mla_cols = _split_w_in(w_in[layer])
        gdn_in, mla_in = _in_proj(x, mod, ln_mix_g[layer].reshape(1, d), w_cat, gdn_cols, mla_cols, tm)
        o_gdn = _gdn(gdn_in, conv_w[layer], a_log[layer], dt_bias[layer], gdn_norm_g[layer],
                     _block(s, 256))
        q, k, v = _mla_prep(mla_in, cosr, sinr, q_norm_g[layer].reshape(1, -1),
                            kv_norm_g[layer].reshape(1, -1), _split_w_uq(w_uq[layer]),
                            w_ukv[layer].astype(BF16), _block(s, 512))
        o_mla = _attn(q, k, v, _block(s, 1024))
        x1, h2, scores = _out_proj(o_gdn, o_mla, x, mod, w_out[layer].astype(BF16),
                                   ln_ffn_g[layer].reshape(1, d), w_pq[layer].astype(BF16),
                                   sub_keys[layer].astype(BF16), tm)
        n_experts = expert_u.shape[1]
        u16 = expert_u[layer].astype(BF16)
        v16 = expert_v[layer].astype(BF16)
        scores2 = scores.reshape(bsz * s, -1)
        h22 = h2.reshape(bsz * s, d)
        x12 = x1.reshape(bsz * s, d)
        nparts = PEER_PARTS if bsz % PEER_PARTS == 0 else 1
        tpart = bsz * s // nparts
        last = layer + 1 == depth
        sel = {}
        gmats = {}

        def select(part):
            after = gmats.get(part - 2)
            stop, itop = _peer_key_topk(scores2, part, nparts, after)
            sel[part] = _peer_select(stop, itop, _block(tpart, 256))

        out = None
        select(0)
        for part in range(nparts):
            if part + 1 < nparts:
                select(part + 1)
            idx, gate = sel[part]
            gmats[part] = _peer_gate_matrix(idx, gate, n_experts)
            after = sel[part + 1][0] if part + 1 < nparts else None
            out = _peer_dense(h22, gmats[part], u16, v16, x12, mod, final_norm_g.reshape(1, d),
                              _block(tpart, 1024), _block(n_experts, 1024), part, last, out, after)
        x = out.reshape(bsz, s, d)
    return x
```

```python
import functools

import jax
import jax.numpy as jnp
from jax import lax
from jax.experimental import pallas as pl
from jax.experimental.pallas import tpu as pltpu
from jax.experimental.pallas import tpu_sc as plsc

GDN_HEADS = 4
GDN_HEAD_DIM = 128
CONV_WIDTH = 4
CHUNK = 64
MLA_HEADS = 4
QK_NOPE_DIM = 128
QK_ROPE_DIM = 64
V_HEAD_DIM = 128
Q_LORA_RANK = 384
KV_LORA_RANK = 256
ROPE_THETA = 10000.0
PEER_HEADS = 8
N_KEYS = 128
PEER_TOPK = 16
N_MOD = 6
EPS = 1e-6

LANES = 128
SUBLANES = 8
SC_CORES = 2
SC_SUBCORES = 16
SC_LANES = 16
SC_WORKERS = SC_CORES * SC_SUBCORES
VMEM_LIMIT = 48 * 1024 * 1024
PEER_DENSE_VMEM = 56 * 1024 * 1024

F32 = jnp.float32
BF16 = jnp.bfloat16
HI = lax.Precision.HIGHEST


def _dot(a, b, precision=None):
    return jnp.dot(a, b, preferred_element_type=F32, precision=precision)


def _dot_nt(a, b, precision=None):
    return lax.dot_general(a, b, (((1,), (1,)), ((), ())),
                           preferred_element_type=F32, precision=precision)


def _dot_tn(a, b, precision=None):
    return lax.dot_general(a, b, (((0,), (0,)), ((), ())),
                           preferred_element_type=F32, precision=precision)


def _rms(x):
    return x * lax.rsqrt(jnp.mean(x * x, axis=-1, keepdims=True) + EPS)


def _silu(x):
    return x * jax.nn.sigmoid(x)


def _params(*sem):
    return pltpu.CompilerParams(dimension_semantics=sem, vmem_limit_bytes=VMEM_LIMIT)


def _mod_kernel(c_ref, w_ref, b_ref, o_ref):
    o_ref[...] = _dot(_silu(c_ref[...]), w_ref[...], HI) + b_ref[...]


def _mod(c, w_ada, b_ada):
    bsz, d = c.shape
    n = w_ada.shape[1]
    return pl.pallas_call(
        _mod_kernel,
        grid=(n // d,),
        in_specs=[pl.BlockSpec((bsz, d), lambda j: (0, 0)),
                  pl.BlockSpec((d, d), lambda j: (0, j)),
                  pl.BlockSpec((1, d), lambda j: (0, j))],
        out_specs=pl.BlockSpec((bsz, d), lambda j: (0, j)),
        out_shape=jax.ShapeDtypeStruct((bsz, n), F32),
        compiler_params=_params("arbitrary"),
        name="mod",
    )(c, w_ada, b_ada.reshape(1, n))


def _inproj_kernel(x_ref, mod_ref, g_ref, w_ref, gdn_ref, mla_ref):
    m = mod_ref[0]
    h = _rms(x_ref[0]) * g_ref[...] * (1.0 + m[1:2]) + m[0:1]
    p = _dot(h.astype(BF16), w_ref[...])
    gw = gdn_ref.shape[-1]
    gdn_ref[0] = p[:, :gw]
    mla_ref[0] = p[:, gw:]


def _in_proj(x, mod, g, w_cat, gdn_cols, mla_cols, tm):
    bsz, s, d = x.shape
    return pl.pallas_call(
        _inproj_kernel,
        grid=(bsz, s // tm),
        in_specs=[pl.BlockSpec((1, tm, d), lambda b, i: (b, i, 0)),
                  pl.BlockSpec((1, N_MOD, d), lambda b, i: (b, 0, 0)),
                  pl.BlockSpec((1, d), lambda b, i: (0, 0)),
                  pl.BlockSpec((d, gdn_cols + mla_cols), lambda b, i: (0, 0))],
        out_specs=[pl.BlockSpec((1, tm, gdn_cols), lambda b, i: (b, i, 0)),
                   pl.BlockSpec((1, tm, mla_cols), lambda b, i: (b, i, 0))],
        out_shape=[jax.ShapeDtypeStruct((bsz, s, gdn_cols), F32),
                   jax.ShapeDtypeStruct((bsz, s, mla_cols), F32)],
        compiler_params=_params("parallel", "parallel"),
        name="in_proj",
    )(x, mod, g, w_cat)


def _gdn_kernel(x_ref, cw_ref, alog_ref, dtb_ref, ng_ref, o_ref,
                state_ref, tail_ref, buf_ref, vnew_ref):
    sb = x_ref.shape[1]
    nchunk = sb // CHUNK
    hdim = GDN_HEAD_DIM
    gw = GDN_HEADS * hdim

    @pl.when(pl.program_id(1) == 0)
    def _():
        state_ref[...] = jnp.zeros_like(state_ref)
        tail_ref[...] = jnp.zeros_like(tail_ref)

    def conv_silu(slot):
        x = x_ref[0, :, slot * hdim:(slot + 1) * hdim]
        buf_ref[slot, 0:SUBLANES, :] = tail_ref[slot]
        buf_ref[slot, SUBLANES:SUBLANES + sb, :] = x
        tail_ref[slot] = x[sb - SUBLANES:sb, :]
        cw = cw_ref[slot]
        y = cw[CONV_WIDTH - 1:CONV_WIDTH] * x
        for j in range(CONV_WIDTH - 1):
            off = SUBLANES - (CONV_WIDTH - 1) + j
            y = y + cw[j:j + 1] * buf_ref[slot, off:off + sb, :]
        return _silu(y)

    def l2n(x):
        return x * lax.rsqrt(jnp.sum(x * x, axis=-1, keepdims=True) + EPS)

    ri = lax.broadcasted_iota(jnp.int32, (sb, sb), 0)
    ci = lax.broadcasted_iota(jnp.int32, (sb, sb), 1)
    same = (ri // CHUNK) == (ci // CHUNK)
    incl = same & (ci <= ri)
    strict = same & (ci < ri)
    eye = (ri == ci).astype(F32)
    levels = []
    bs = 1
    while bs < CHUNK:
        levels.append(((ri // bs) % 2 == 1) & ((ci // bs) % 2 == 0)
                      & ((ri // (2 * bs)) == (ci // (2 * bs))))
        bs *= 2
    rowmod = lax.broadcasted_iota(jnp.int32, (sb, hdim), 0) % CHUNK
    ab = x_ref[0, :, 4 * gw:4 * gw + LANES]

    heads = range(GDN_HEADS)
    q = [l2n(conv_silu(h)) * (hdim ** -0.5) for h in heads]
    k = [l2n(conv_silu(GDN_HEADS + h)) for h in heads]
    v = [conv_silu(2 * GDN_HEADS + h) for h in heads]
    beta = [jax.nn.sigmoid(ab[:, GDN_HEADS + h:GDN_HEADS + h + 1]) for h in heads]

    gc = [-jnp.exp(alog_ref[h]) * jax.nn.softplus(ab[:, h:h + 1] + dtb_ref[h]) for h in heads]
    sh = 1
    while sh < CHUNK:
        gc = [g + jnp.where(rowmod >= sh, pltpu.roll(g, sh, axis=0), 0.0) for g in gc]
        sh *= 2
    gcl = [jnp.concatenate(
        [jnp.broadcast_to(g[(n + 1) * CHUNK - 1:(n + 1) * CHUNK, :], (CHUNK, hdim))
         for n in range(nchunk)], axis=0) for g in gc]
    decay = []
    for g in gc:
        gc_row = jnp.broadcast_to(g.T[0:1, :], (sb, sb))
        decay.append(jnp.where(incl, jnp.exp(jnp.where(incl, g[:, 0:1] - gc_row, 0.0)), 0.0))

    kb = [k[h] * beta[h] for h in heads]
    k16 = [k[h].astype(BF16) for h in heads]
    mmat = [jnp.where(strict, _dot_nt(kb[h].astype(BF16), k16[h]) * decay[h], 0.0) for h in heads]
    attn = [(_dot_nt(q[h].astype(BF16), k16[h]) * decay[h]).astype(BF16) for h in heads]
    tinv = [eye - jnp.where(levels[0], m, 0.0) for m in mmat]
    for off in levels[1:]:
        t16 = [t.astype(BF16) for t in tinv]
        ta = [_dot(t16[h], jnp.where(off, mmat[h], 0.0).astype(BF16)).astype(BF16) for h in heads]
        tinv = [tinv[h] - _dot(ta[h], t16[h]) for h in heads]

    egc = [jnp.exp(g) for g in gc]
    rhs = [jnp.concatenate([v[h] * beta[h], kb[h] * egc[h]], axis=1) for h in heads]
    sol = [rhs[h] + _dot((tinv[h] - eye).astype(BF16), rhs[h].astype(BF16)) for h in heads]
    u = [x[:, :hdim] for x in sol]
    w16 = [x[:, hdim:].astype(BF16) for x in sol]
    qd16 = [(q[h] * egc[h]).astype(BF16) for h in heads]
    kd16 = [(k[h] * jnp.exp(gcl[h] - gc[h])).astype(BF16) for h in heads]
    g_last = [jnp.exp(x) for x in gcl]

    st = [state_ref[h] for h in heads]
    o_inter = [[] for _ in heads]
    for n in range(nchunk):
        lo = n * CHUNK
        for h in heads:
            st16 = st[h].astype(BF16)
            v_new = u[h][lo:lo + CHUNK] - _dot(w16[h][lo:lo + CHUNK], st16)
            vnew_ref[h, lo:lo + CHUNK, :] = v_new
            o_inter[h].append(_dot(qd16[h][lo:lo + CHUNK], st16))
            st[h] = st[h] * g_last[h][lo:lo + 1, :] + _dot_tn(kd16[h][lo:lo + CHUNK],
                                                             v_new.astype(BF16))
    for h in heads:
        state_ref[h] = st[h]
        o = jnp.concatenate(o_inter[h], axis=0) + _dot(attn[h], vnew_ref[h].astype(BF16))
        z = x_ref[0, :, 3 * gw + h * hdim:3 * gw + (h + 1) * hdim]
        o = _rms(o) * ng_ref[...] * _silu(z)
        o_ref[0, :, h * hdim:(h + 1) * hdim] = o.astype(o_ref.dtype)


def _gdn(gdn_in, conv_w, a_log, dt_bias, norm_g, sb):
    bsz, s, cols = gdn_in.shape
    hdim = GDN_HEAD_DIM
    nh = GDN_HEADS
    cw = conv_w.reshape(CONV_WIDTH, 3 * nh, hdim).transpose(1, 0, 2)
    alog = jnp.broadcast_to(a_log.reshape(nh, 1, 1), (nh, 1, hdim)).astype(F32)
    dtb = jnp.broadcast_to(dt_bias.reshape(nh, 1, 1), (nh, 1, hdim)).astype(F32)
    full = lambda shape: pl.BlockSpec(shape, lambda b, i: (0,) * len(shape))
    return pl.pallas_call(
        _gdn_kernel,
        grid=(bsz, s // sb),
        in_specs=[pl.BlockSpec((1, sb, cols), lambda b, i: (b, i, 0)),
                  full(cw.shape), full(alog.shape), full(dtb.shape), full((1, hdim))],
        out_specs=pl.BlockSpec((1, sb, nh * hdim), lambda b, i: (b, i, 0)),
        out_shape=jax.ShapeDtypeStruct((bsz, s, nh * hdim), BF16),
        scratch_shapes=[pltpu.VMEM((nh, hdim, hdim), F32),
                        pltpu.VMEM((3 * nh, SUBLANES, hdim), F32),
                        pltpu.VMEM((3 * nh, sb + SUBLANES, hdim), F32),
                        pltpu.VMEM((nh, sb, hdim), F32)],
        compiler_params=_params("parallel", "arbitrary"),
        name="gdn",
    )(gdn_in, cw, alog, dtb, norm_g.reshape(1, hdim))


MLA_QK_PAD = 256


def _mla_prep_kernel(m_ref, cos_ref, sin_ref, gq_ref, gkv_ref, wq_ref, wkv_ref,
                     q_ref, k_ref, v_ref):
    m = m_ref[0]
    cosr = cos_ref[0]
    sinr = sin_ref[0]
    cq = m[:, :Q_LORA_RANK]
    ckv = m[:, Q_LORA_RANK:Q_LORA_RANK + KV_LORA_RANK]
    o = Q_LORA_RANK + KV_LORA_RANK
    kr = m[:, o:o + LANES]
    krs = m[:, o + LANES:o + 2 * LANES]
    scale = (QK_NOPE_DIM + QK_ROPE_DIM) ** -0.5
    qa = _dot((_rms(cq) * gq_ref[...]).astype(BF16), wq_ref[...])
    kva = _dot((_rms(ckv) * gkv_ref[...]).astype(BF16), wkv_ref[...])
    k_rope = (kr * cosr + krs * sinr).astype(k_ref.dtype)
    sw0 = MLA_HEADS * MLA_QK_PAD
    for h in range(MLA_HEADS):
        b0 = h * MLA_QK_PAD
        rope = qa[:, b0 + LANES:b0 + 2 * LANES] * cosr + qa[:, sw0 + h * LANES:sw0 + (h + 1) * LANES] * sinr
        q_ref[0, h, :, 0:LANES] = (qa[:, b0:b0 + LANES] * scale).astype(q_ref.dtype)
        q_ref[0, h, :, LANES:2 * LANES] = (rope * scale).astype(q_ref.dtype)
        c0 = h * (QK_NOPE_DIM + V_HEAD_DIM)
        k_ref[0, h, :, 0:LANES] = kva[:, c0:c0 + QK_NOPE_DIM].astype(k_ref.dtype)
        k_ref[0, h, :, LANES:2 * LANES] = k_rope
        v_ref[0, h] = kva[:, c0 + QK_NOPE_DIM:c0 + QK_NOPE_DIM + V_HEAD_DIM].astype(v_ref.dtype)


def _mla_prep(mla_in, cosr, sinr, gq, gkv, wq, wkv, tm):
    bsz, s, mc = mla_in.shape
    nh = MLA_HEADS
    return pl.pallas_call(
        _mla_prep_kernel,
        grid=(bsz, s // tm),
        in_specs=[pl.BlockSpec((1, tm, mc), lambda b, i: (b, i, 0)),
                  pl.BlockSpec((1, tm, LANES), lambda b, i: (b, i, 0)),
                  pl.BlockSpec((1, tm, LANES), lambda b, i: (b, i, 0)),
                  pl.BlockSpec((1, Q_LORA_RANK), lambda b, i: (0, 0)),
                  pl.BlockSpec((1, KV_LORA_RANK), lambda b, i: (0, 0)),
                  pl.BlockSpec(wq.shape, lambda b, i: (0, 0)),
                  pl.BlockSpec(wkv.shape, lambda b, i: (0, 0))],
        out_specs=[pl.BlockSpec((1, nh, tm, MLA_QK_PAD), lambda b, i: (b, 0, i, 0)),
                   pl.BlockSpec((1, nh, tm, MLA_QK_PAD), lambda b, i: (b, 0, i, 0)),
                   pl.BlockSpec((1, nh, tm, V_HEAD_DIM), lambda b, i: (b, 0, i, 0))],
        out_shape=[jax.ShapeDtypeStruct((bsz, nh, s, MLA_QK_PAD), BF16),
                   jax.ShapeDtypeStruct((bsz, nh, s, MLA_QK_PAD), BF16),
                   jax.ShapeDtypeStruct((bsz, nh, s, V_HEAD_DIM), BF16)],
        compiler_params=_params("parallel", "parallel"),
        name="mla_prep",
    )(mla_in, cosr, sinr, gq, gkv, wq, wkv)


def _fold_lanes(x, op):
    parts = [x[:, c * LANES:(c + 1) * LANES] for c in range(x.shape[1] // LANES)]
    while len(parts) > 1:
        parts = [op(parts[i], parts[i + 1]) for i in range(0, len(parts), 2)]
    return parts[0]


def _attn_kernel(q_ref, k_ref, v_ref, o_ref, s_ref, m_ref, l_ref, acc_ref):
    i = pl.program_id(2)
    tq = q_ref.shape[2]
    m_ref[...] = jnp.full_like(m_ref, -jnp.inf)
    l_ref[...] = jnp.zeros_like(l_ref)
    acc_ref[...] = jnp.zeros_like(acc_ref)

    def scores(j, slot):
        start = pl.multiple_of(j * tq, tq)
        s_ref[slot] = _dot_nt(q_ref[0, 0], k_ref[0, 0, pl.ds(start, tq), :])

    def consume(j, slot, diagonal):
        start = pl.multiple_of(j * tq, tq)
        s = s_ref[slot]
        if diagonal:
            row = lax.broadcasted_iota(jnp.int32, s.shape, 0)
            col = lax.broadcasted_iota(jnp.int32, s.shape, 1)
            s = jnp.where(col <= row, s, -jnp.inf)
        m_prev = m_ref[...]
        m_new = jnp.maximum(m_prev, jnp.max(_fold_lanes(s, jnp.maximum), axis=-1, keepdims=True))
        alpha = jnp.exp(m_prev - m_new)
        p = jnp.exp(s - m_new)
        l_ref[...] = alpha * l_ref[...] + jnp.sum(_fold_lanes(p, jnp.add), axis=-1, keepdims=True)
        acc_ref[...] = alpha * acc_ref[...] + _dot(p.astype(v_ref.dtype),
                                                   v_ref[0, 0, pl.ds(start, tq), :])
        m_ref[...] = m_new

    scores(0, 0)

    def pair(t, carry):
        scores(2 * t + 1, 1)
        consume(2 * t, 0, False)
        scores(2 * t + 2, 0)
        consume(2 * t + 1, 1, False)
        return carry
    lax.fori_loop(0, i // 2, pair, 0)

    @pl.when(i % 2 == 1)
    def _():
        scores(i, 1)
        consume(i - 1, 0, False)
        consume(i, 1, True)

    @pl.when(i % 2 == 0)
    def _():
        consume(i, 0, True)

    o_ref[0] = (acc_ref[...] / l_ref[...]).astype(o_ref.dtype)


def _attn(q, k, v, tq):
    bsz, nh, s, dq = q.shape
    dv = v.shape[-1]
    return pl.pallas_call(
        _attn_kernel,
        grid=(bsz, nh, s // tq),
        in_specs=[pl.BlockSpec((1, 1, tq, dq), lambda b, h, i: (b, h, i, 0)),
                  pl.BlockSpec((1, 1, s, dq), lambda b, h, i: (b, h, 0, 0)),
                  pl.BlockSpec((1, 1, s, dv), lambda b, h, i: (b, h, 0, 0))],
        out_specs=pl.BlockSpec((1, tq, dv), lambda b, h, i: (b, i, h)),
        out_shape=jax.ShapeDtypeStruct((bsz, s, nh * dv), BF16),
        scratch_shapes=[pltpu.VMEM((2, tq, tq), F32),
                        pltpu.VMEM((tq, 1), F32), pltpu.VMEM((tq, 1), F32),
                        pltpu.VMEM((tq, dv), F32)],
        compiler_params=_params("parallel", "parallel", "arbitrary"),
        name="attn",
    )(q, k, v)


def _outproj_kernel(og_ref, om_ref, x_ref, mod_ref, wo_ref, g_ref, wpq_ref, keys_ref,
                    x1_ref, h2_ref, sc_ref):
    m = mod_ref[0]
    gw = og_ref.shape[-1]
    mixed = _dot(og_ref[0], wo_ref[0:gw, :]) + _dot(om_ref[0], wo_ref[gw:, :])
    x1 = x_ref[0] + m[2:3] * mixed
    h2 = _rms(x1) * g_ref[...] * (1.0 + m[4:5]) + m[3:4]
    x1_ref[0] = x1
    h2_ref[0] = h2.astype(h2_ref.dtype)
    qp = _dot(h2.astype(BF16), wpq_ref[...]).astype(BF16)
    for h in range(PEER_HEADS):
        for p in range(2):
            c0 = (2 * h + p) * N_KEYS
            sc_ref[0, :, c0:c0 + N_KEYS] = _dot_nt(qp[:, c0:c0 + N_KEYS], keys_ref[p, h])


def _out_proj(o_gdn, o_mla, x, mod, w_out, g, w_pq, keys, tm):
    bsz, s, d = x.shape
    gw = o_gdn.shape[-1]
    mw = o_mla.shape[-1]
    nq = w_pq.shape[1]
    return pl.pallas_call(
        _outproj_kernel,
        grid=(bsz, s // tm),
        in_specs=[pl.BlockSpec((1, tm, gw), lambda b, i: (b, i, 0)),
                  pl.BlockSpec((1, tm, mw), lambda b, i: (b, i, 0)),
                  pl.BlockSpec((1, tm, d), lambda b, i: (b, i, 0)),
                  pl.BlockSpec((1, N_MOD, d), lambda b, i: (b, 0, 0)),
                  pl.BlockSpec((gw + mw, d), lambda b, i: (0, 0)),
                  pl.BlockSpec((1, d), lambda b, i: (0, 0)),
                  pl.BlockSpec((d, nq), lambda b, i: (0, 0)),
                  pl.BlockSpec(keys.shape, lambda b, i: (0, 0, 0, 0))],
        out_specs=[pl.BlockSpec((1, tm, d), lambda b, i: (b, i, 0)),
                   pl.BlockSpec((1, tm, d), lambda b, i: (b, i, 0)),
                   pl.BlockSpec((1, tm, nq), lambda b, i: (b, i, 0))],
        out_shape=[jax.ShapeDtypeStruct((bsz, s, d), F32),
                   jax.ShapeDtypeStruct((bsz, s, d), BF16),
                   jax.ShapeDtypeStruct((bsz, s, nq), F32)],
        compiler_params=_params("parallel", "parallel"),
        name="out_proj",
    )(o_gdn, o_mla, x, mod, w_out, g, w_pq, keys)


PEER_SORT_TOKENS = 8


def _peer_key_topk(scores, part, nparts, after=None):
    t_all, width = scores.shape
    nk = N_KEYS
    ln = SC_LANES
    assert PEER_TOPK == ln
    nv = nk // ln
    rpt = width // nk
    grp = PEER_SORT_TOKENS
    toks = t_all // nparts
    tok_per_w = toks // SC_WORKERS
    n_groups = tok_per_w // grp
    assert n_groups * grp * SC_WORKERS * nparts == t_all and rpt % 2 == 0
    tok_base = part * toks
    rows = toks * rpt
    mesh = plsc.VectorSubcoreMesh(core_axis_name="c", subcore_axis_name="s",
                                  num_cores=SC_CORES, num_subcores=SC_SUBCORES)

    @functools.partial(
        pl.kernel, mesh=mesh,
        compiler_params=pltpu.CompilerParams(needs_layout_passes=False),
        out_type=[jax.ShapeDtypeStruct((rows * ln,), F32),
                  jax.ShapeDtypeStruct((rows * ln,), jnp.int32)],
        scratch_types=[pltpu.VMEM((grp, width), F32),
                       pltpu.VMEM((grp * rpt * ln,), F32),
                       pltpu.VMEM((grp * rpt * ln,), jnp.int32)],
    )
    def keys_kernel(s_hbm, *rest):
        ov_hbm, oi_hbm, buf, outv, outi = rest[-5:]
        wid = lax.axis_index("s") * SC_CORES + lax.axis_index("c")
        lane = lax.iota(jnp.int32, ln)

        def merge(a, b):
            ka, va = a
            kb, vb = b
            kbr = lax.rev(kb, (0,))
            vbr = lax.rev(vb, (0,))
            take_a = ka >= kbr
            kc = jnp.where(take_a, ka, kbr)
            vc = jnp.where(take_a, va, vbr)
            vi, ki = plsc.sort_key_val(vc, kc, descending=False)
            return plsc.sort_key_val(ki, vi, descending=True)

        def one_row(t, row):
            parts = []
            for j in range(nv):
                kj = buf[t, pl.ds(row * nk + j * ln, ln)]
                parts.append(plsc.sort_key_val(kj, lane + j * ln, descending=True))
            while len(parts) > 1:
                parts = [merge(parts[i], parts[i + 1]) for i in range(0, len(parts), 2)]
            out = (t * rpt + row) * ln
            outv[pl.ds(out, ln)] = parts[0][0]
            outi[pl.ds(out, ln)] = parts[0][1]

        def group_body(g, carry):
            tok0 = wid * tok_per_w + g * grp
            pltpu.sync_copy(s_hbm.at[pl.ds(tok_base + tok0, grp)], buf)

            def rows_body(i, c2):
                t = i // (rpt // 2)
                row = (i % (rpt // 2)) * 2
                one_row(t, row)
                one_row(t, row + 1)
                return c2
            lax.fori_loop(0, grp * rpt // 2, rows_body, 0)
            pltpu.sync_copy(outv, ov_hbm.at[pl.ds(tok0 * rpt * ln, grp * rpt * ln)])
            pltpu.sync_copy(outi, oi_hbm.at[pl.ds(tok0 * rpt * ln, grp * rpt * ln)])
            return carry
        lax.fori_loop(0, n_groups, group_body, 0)

    args = (scores,) if after is None else (scores, after)
    ov, oi = keys_kernel(*args)
    return ov.reshape(toks, rpt * ln), oi.reshape(toks, rpt * ln)


PEER_CAND_ROWS = 2 * SUBLANES + (SUBLANES - 1) * SUBLANES + SUBLANES


def _peer_select_kernel(stop_in_ref, itop_in_ref, *rest):
    idx_ref, gate_ref, stop_ref, itop_ref, cand_ref, cidx_ref, best_ref, idxt_ref, gatet_ref = rest[-9:]
    tb = stop_in_ref.shape[0]
    kk = PEER_TOPK
    assert kk == 2 * SUBLANES
    stop_ref[...] = stop_in_ref[...].T
    itop_ref[...] = itop_in_ref[...].astype(F32).T

    def extract(vals, row, n):
        m = jnp.max(vals, axis=0, keepdims=True)
        pos = jnp.min(jnp.where(vals == m, row, float(n)), axis=0, keepdims=True)
        return m, pos, jnp.where(row == pos, -jnp.inf, vals)

    row_c = lax.broadcasted_iota(jnp.int32, (PEER_CAND_ROWS, tb), 0).astype(F32)
    row_8 = lax.broadcasted_iota(jnp.int32, (SUBLANES, tb), 0)

    def build_candidates(h):
        r1 = 2 * h * kk
        r2 = r1 + kk
        cand_ref[h, 0:kk, :] = stop_ref[r1:r1 + 1, :] + stop_ref[r2:r2 + kk, :]
        cidx_ref[h, 0:kk, :] = itop_ref[r1:r1 + 1, :] * float(N_KEYS) + itop_ref[r2:r2 + kk, :]
        s2 = stop_ref[r2:r2 + SUBLANES, :]
        i2 = itop_ref[r2:r2 + SUBLANES, :]
        for a in range(1, SUBLANES):
            r0 = kk + (a - 1) * SUBLANES
            cand_ref[h, r0:r0 + SUBLANES, :] = jnp.where(
                row_8 < kk // (a + 1), stop_ref[r1 + a:r1 + a + 1, :] + s2, -jnp.inf)
            cidx_ref[h, r0:r0 + SUBLANES, :] = itop_ref[r1 + a:r1 + a + 1, :] * float(N_KEYS) + i2
        r0 = kk + (SUBLANES - 1) * SUBLANES
        cand_ref[h, r0:r0 + SUBLANES, :] = (stop_ref[r1 + SUBLANES:r1 + kk, :]
                                           + stop_ref[r2:r2 + 1, :])
        cidx_ref[h, r0:r0 + SUBLANES, :] = (itop_ref[r1 + SUBLANES:r1 + kk, :] * float(N_KEYS)
                                           + itop_ref[r2:r2 + 1, :])

    for h in range(PEER_HEADS):
        build_candidates(h)
    cvals = [cand_ref[h] for h in range(PEER_HEADS)]
    cidx = [cidx_ref[h] for h in range(PEER_HEADS)]
    for r in range(kk):
        for h in range(PEER_HEADS):
            m, pos, cvals[h] = extract(cvals[h], row_c, PEER_CAND_ROWS)
            best_ref[h, r:r + 1, :] = m
            idxt_ref[h * kk + r:h * kk + r + 1, :] = jnp.max(
                jnp.where(row_c == pos, cidx[h], -1.0), axis=0, keepdims=True)
    for h in range(PEER_HEADS):
        best = best_ref[h]
        e = jnp.exp(best - best[0:1, :])
        gatet_ref[h * kk:(h + 1) * kk, :] = e / jnp.sum(e, axis=0, keepdims=True)
    idx_ref[...] = idxt_ref[...].T.astype(jnp.int32)
    gate_ref[...] = gatet_ref[...].T


def _peer_select(stop, itop, tb, after=None):
    t, w = stop.shape
    hk = PEER_HEADS * PEER_TOPK
    kk = PEER_TOPK
    in_specs = [pl.BlockSpec((tb, w), lambda i: (i, 0)),
                pl.BlockSpec((tb, w), lambda i: (i, 0))]
    args = [stop, itop]
    if after is not None:
        in_specs.append(pl.BlockSpec(memory_space=pl.ANY))
        args.append(after)
    return pl.pallas_call(
        _peer_select_kernel,
        grid=(t // tb,),
        in_specs=in_specs,
        out_specs=[pl.BlockSpec((tb, hk), lambda i: (i, 0)),
                   pl.BlockSpec((tb, hk), lambda i: (i, 0))],
        out_shape=[jax.ShapeDtypeStruct((t, hk), jnp.int32),
                   jax.ShapeDtypeStruct((t, hk), F32)],
        scratch_shapes=[pltpu.VMEM((w, tb), F32), pltpu.VMEM((w, tb), F32),
                        pltpu.VMEM((PEER_HEADS, PEER_CAND_ROWS, tb), F32),
                        pltpu.VMEM((PEER_HEADS, PEER_CAND_ROWS, tb), F32),
                        pltpu.VMEM((PEER_HEADS, kk, tb), F32),
                        pltpu.VMEM((hk, tb), F32), pltpu.VMEM((hk, tb), F32)],
        compiler_params=_params("parallel"),
        name="peer_select",
    )(*args)


PEER_TOKEN_GROUP = 16
PEER_PARTS = 4


def _peer_gate_matrix(idx, gate, n_experts):
    t_total, hk = idx.shape
    ln = SC_LANES
    nu = hk // ln
    grp = PEER_TOKEN_GROUP
    tok_per_w = t_total // SC_WORKERS
    n_groups = tok_per_w // grp
    assert n_groups * grp * SC_WORKERS == t_total and grp % 2 == 0
    mesh = plsc.VectorSubcoreMesh(core_axis_name="c", subcore_axis_name="s",
                                  num_cores=SC_CORES, num_subcores=SC_SUBCORES)

    @functools.partial(
        pl.kernel, mesh=mesh,
        compiler_params=pltpu.CompilerParams(needs_layout_passes=False),
        out_type=jax.ShapeDtypeStruct((t_total, n_experts), F32),
        scratch_types=[
            pltpu.VMEM((grp * hk,), jnp.int32),
            pltpu.VMEM((grp * hk,), F32),
            pltpu.VMEM((n_experts,), F32),
            pltpu.VMEM((n_experts,), F32),
            pltpu.SemaphoreType.DMA((2,)),
        ],
    )
    def gate_kernel(idx_hbm, gate_hbm, g_hbm, idx_v, gate_v, row0_v, row1_v, sem):
        rows = (row0_v, row1_v)
        wid = lax.axis_index("s") * SC_CORES + lax.axis_index("c")
        base = wid * tok_per_w
        zero = jnp.zeros((ln,), F32)

        def zero_body(c, carry):
            row0_v[pl.ds(c * ln, ln)] = zero
            row1_v[pl.ds(c * ln, ln)] = zero
            return carry
        lax.fori_loop(0, n_experts // ln, zero_body, 0)

        def out_copy(tok, slot):
            return pltpu.make_async_copy(rows[slot], g_hbm.at[tok], sem.at[slot])

        def group_body(g, carry):
            tok0 = base + g * grp
            pltpu.sync_copy(idx_hbm.at[pl.ds(tok0 * hk, grp * hk)], idx_v)
            pltpu.sync_copy(gate_hbm.at[pl.ds(tok0 * hk, grp * hk)], gate_v)

            def pair_body(i, carry2):
                for slot in range(2):
                    t = i * 2 + slot
                    for u in range(nu):
                        sl = pl.ds(t * hk + u * ln, ln)
                        plsc.addupdate_scatter(rows[slot], [idx_v[sl]], gate_v[sl])
                    out_copy(tok0 + t, slot).start()
                for slot in range(2):
                    t = i * 2 + slot
                    out_copy(tok0 + t, slot).wait()
                    for u in range(nu):
                        plsc.store_scatter(rows[slot], [idx_v[pl.ds(t * hk + u * ln, ln)]], zero)
                return carry2
            lax.fori_loop(0, grp // 2, pair_body, 0)
            return carry
        lax.fori_loop(0, n_groups, group_body, 0)

    return gate_kernel(idx.reshape(-1), gate.reshape(-1))


def _peer_dense_kernel(h_ref, g_ref, u_ref, v_ref, x1_ref, mod_ref, ng_ref, *rest, normalize):
    o_ref = rest[-1]
    e = pl.program_id(1)
    s = _dot_nt(h_ref[...], u_ref[...])
    p = (jax.nn.gelu(s) * g_ref[...]).astype(v_ref.dtype)
    contrib = _dot(p, v_ref[...])

    @pl.when(e == 0)
    def _():
        o_ref[...] = contrib

    @pl.when(e > 0)
    def _():
        o_ref[...] += contrib

    @pl.when(e == pl.num_programs(1) - 1)
    def _():
        x2 = x1_ref[...] + mod_ref[0][5:6] * o_ref[...]
        o_ref[...] = _rms(x2) * ng_ref[...] if normalize else x2


def _peer_dense(h, gmat, u, v, x1, mod, ng, tb, eb, part, normalize, prev, after=None):
    t = gmat.shape[0]
    t_all, d = h.shape
    n_e = u.shape[0]
    blk0 = part * (t // tb)
    blk_per_batch = t_all // mod.shape[0] // tb
    tok_blk = pl.BlockSpec((tb, d), lambda i, e: (i + blk0, 0))
    in_specs = [tok_blk,
                pl.BlockSpec((tb, eb), lambda i, e: (i, e)),
                pl.BlockSpec((eb, d), lambda i, e: (e, 0)),
                pl.BlockSpec((eb, d), lambda i, e: (e, 0)),
                tok_blk,
                pl.BlockSpec((1, N_MOD, d), lambda i, e: ((i + blk0) // blk_per_batch, 0, 0)),
                pl.BlockSpec((1, d), lambda i, e: (0, 0))]
    args = [h, gmat, u, v, x1, mod, ng]
    aliases = {}
    if prev is not None:
        in_specs.append(pl.BlockSpec(memory_space=pl.ANY))
        args.append(prev)
        aliases = {len(args) - 1: 0}
    if after is not None:
        in_specs.append(pl.BlockSpec(memory_space=pl.ANY))
        args.append(after)
    return pl.pallas_call(
        functools.partial(_peer_dense_kernel, normalize=normalize),
        grid=(t // tb, n_e // eb),
        in_specs=in_specs,
        out_specs=tok_blk,
        out_shape=jax.ShapeDtypeStruct((t_all, d), F32),
        input_output_aliases=aliases,
        compiler_params=pltpu.CompilerParams(dimension_semantics=("parallel", "arbitrary"),
                                             vmem_limit_bytes=PEER_DENSE_VMEM),
        name="peer_dense",
    )(*args)


def _split_w_in(w):
    gw = GDN_HEADS * GDN_HEAD_DIM
    sizes = [gw] * 4 + [GDN_HEADS] * 2 + [Q_LORA_RANK, KV_LORA_RANK, QK_ROPE_DIM]
    offs = [0]
    for sz in sizes:
        offs.append(offs[-1] + sz)
    parts = [w[:, offs[i]:offs[i + 1]] for i in range(len(sizes))]
    gq, gk, gv, gz, ga, gb, cq, ckv, kr = parts
    d = w.shape[0]
    half = QK_ROPE_DIM // 2
    zeros = lambda n: jnp.zeros((d, n), w.dtype)
    gdn = jnp.concatenate([gq, gk, gv, gz, ga, gb, zeros(LANES - 2 * GDN_HEADS)], axis=1)
    kr_sw = jnp.concatenate([kr[:, half:], kr[:, :half]], axis=1)
    mla = jnp.concatenate([cq, ckv, kr, zeros(LANES - QK_ROPE_DIM),
                           kr_sw, zeros(LANES - QK_ROPE_DIM)], axis=1)
    return jnp.concatenate([gdn, mla], axis=1).astype(BF16), gdn.shape[1], mla.shape[1]


def _split_w_uq(w):
    r = w.shape[0]
    half = QK_ROPE_DIM // 2
    hd = QK_NOPE_DIM + QK_ROPE_DIM
    main, swapped = [], []
    for h in range(MLA_HEADS):
        nope = w[:, h * hd:h * hd + QK_NOPE_DIM]
        rope = w[:, h * hd + QK_NOPE_DIM:(h + 1) * hd]
        main += [nope, rope, jnp.zeros((r, MLA_QK_PAD - hd), w.dtype)]
        swapped += [rope[:, half:], rope[:, :half], jnp.zeros((r, LANES - QK_ROPE_DIM), w.dtype)]
    return jnp.concatenate(main + swapped, axis=1).astype(BF16)


def _rotary_tables(positions):
    half = QK_ROPE_DIM // 2
    inv_freq = ROPE_THETA ** (-jnp.arange(half, dtype=F32) / half)
    ang = positions.astype(F32)[..., None] * inv_freq
    cos, sin = jnp.cos(ang), jnp.sin(ang)
    z = jnp.zeros(cos.shape[:-1] + (LANES - QK_ROPE_DIM,), F32)
    return (jnp.concatenate([cos, cos, z], axis=-1),
            jnp.concatenate([-sin, sin, z], axis=-1))


def _block(n, pref):
    return pref if n % pref == 0 else n


def kernel(x, c, positions, ln_mix_g, w_in, conv_w, a_log, dt_bias, gdn_norm_g, q_norm_g, w_uq, kv_norm_g, w_ukv, w_out, ln_ffn_g, w_pq, sub_keys, expert_u, expert_v, w_ada, b_ada, final_norm_g):
    bsz, s, d = x.shape
    depth = w_in.shape[0]
    cosr, sinr = _rotary_tables(positions)
    tm = _block(s, 512)
    for layer in range(depth):
        mod = _mod(c, w_ada[layer], b_ada[layer]).reshape(bsz, N_MOD, d)
        w_cat, gdn_cols, mla_cols = _split_w_in(w_in[layer])
        gdn_in, mla_in = _in_proj(x, mod, ln_mix_g[layer].reshape(1, d), w_cat, gdn_cols, mla_cols, tm)
        o_gdn = _gdn(gdn_in, conv_w[layer], a_log[layer], dt_bias[layer], gdn_norm_g[layer],
                     _block(s, 256))
        q, k, v = _mla_prep(mla_in, cosr, sinr, q_norm_g[layer].reshape(1, -1),
                            kv_norm_g[layer].reshape(1, -1), _split_w_uq(w_uq[layer]),
                            w_ukv[layer].astype(BF16), _block(s, 512))
        o_mla = _attn(q, k, v, _block(s, 1024))
        x1, h2, scores = _out_proj(o_gdn, o_mla, x, mod, w_out[layer].astype(BF16),
                                   ln_ffn_g[layer].reshape(1, d), w_pq[layer].astype(BF16),
                                   sub_keys[layer].astype(BF16), tm)
        n_experts = expert_u.shape[1]
        u16 = expert_u[layer].astype(BF16)
        v16 = expert_v[layer].astype(BF16)
        scores2 = scores.reshape(bsz * s, -1)
        h22 = h2.reshape(bsz * s, d)
        x12 = x1.reshape(bsz * s, d)
        nparts = PEER_PARTS if bsz % PEER_PARTS == 0 else 1
        tpart = bsz * s // nparts
        last = layer + 1 == depth
        sel = {}
        gmats = {}

        def select(part, dense_out):
            stop, itop = _peer_key_topk(scores2, part, nparts, gmats.get(part - 2))
            sel[part] = _peer_select(stop, itop, _block(tpart, 256), dense_out)

        out = None
        select(0, None)
        for part in range(nparts):
            if part + 1 < nparts:
                select(part + 1, out)
            idx, gate = sel[part]
            gmats[part] = _peer_gate_matrix(idx, gate, n_experts)
            after = sel[part + 1][0] if part + 1 < nparts else None
            out = _peer_dense(h22, gmats[part], u16, v16, x12, mod, final_norm_g.reshape(1, d),
                              _block(tpart, 1024), _block(n_experts, 1024), part, last, out, after)
        x = out.reshape(bsz, s, d)
    return x
```

```python
import functools

import jax
import jax.numpy as jnp
from jax import lax
from jax.experimental import pallas as pl
from jax.experimental.pallas import tpu as pltpu
from jax.experimental.pallas import tpu_sc as plsc

GDN_HEADS = 4
GDN_HEAD_DIM = 128
CONV_WIDTH = 4
CHUNK = 64
MLA_HEADS = 4
QK_NOPE_DIM = 128
QK_ROPE_DIM = 64
V_HEAD_DIM = 128
Q_LORA_RANK = 384
KV_LORA_RANK = 256
ROPE_THETA = 10000.0
PEER_HEADS = 8
N_KEYS = 128
PEER_TOPK = 16
N_MOD = 6
EPS = 1e-6

LANES = 128
SUBLANES = 8
SC_CORES = 2
SC_SUBCORES = 16
SC_LANES = 16
SC_WORKERS = SC_CORES * SC_SUBCORES
VMEM_LIMIT = 48 * 1024 * 1024
PEER_DENSE_VMEM = 56 * 1024 * 1024

F32 = jnp.float32
BF16 = jnp.bfloat16
HI = lax.Precision.HIGHEST


def _dot(a, b, precision=None):
    return jnp.dot(a, b, preferred_element_type=F32, precision=precision)


def _dot_nt(a, b, precision=None):
    return lax.dot_general(a, b, (((1,), (1,)), ((), ())),
                           preferred_element_type=F32, precision=precision)


def _dot_tn(a, b, precision=None):
    return lax.dot_general(a, b, (((0,), (0,)), ((), ())),
                           preferred_element_type=F32, precision=precision)


def _rms(x):
    return x * lax.rsqrt(jnp.mean(x * x, axis=-1, keepdims=True) + EPS)


def _silu(x):
    return x * jax.nn.sigmoid(x)


def _params(*sem):
    return pltpu.CompilerParams(dimension_semantics=sem, vmem_limit_bytes=VMEM_LIMIT)


def _mod_kernel(c_ref, w_ref, b_ref, o_ref):
    o_ref[...] = _dot(_silu(c_ref[...]), w_ref[...], HI) + b_ref[...]


def _mod(c, w_ada, b_ada):
    bsz, d = c.shape
    n = w_ada.shape[1]
    return pl.pallas_call(
        _mod_kernel,
        grid=(n // d,),
        in_specs=[pl.BlockSpec((bsz, d), lambda j: (0, 0)),
                  pl.BlockSpec((d, d), lambda j: (0, j)),
                  pl.BlockSpec((1, d), lambda j: (0, j))],
        out_specs=pl.BlockSpec((bsz, d), lambda j: (0, j)),
        out_shape=jax.ShapeDtypeStruct((bsz, n), F32),
        compiler_params=_params("arbitrary"),
        name="mod",
    )(c, w_ada, b_ada.reshape(1, n))


def _inproj_kernel(x_ref, mod_ref, g_ref, w_ref, gdn_ref, mla_ref):
    m = mod_ref[0]
    h = _rms(x_ref[0]) * g_ref[...] * (1.0 + m[1:2]) + m[0:1]
    p = _dot(h.astype(BF16), w_ref[...])
    gw = gdn_ref.shape[-1]
    gdn_ref[0] = p[:, :gw]
    mla_ref[0] = p[:, gw:]


def _in_proj(x, mod, g, w_cat, gdn_cols, mla_cols, tm):
    bsz, s, d = x.shape
    return pl.pallas_call(
        _inproj_kernel,
        grid=(bsz, s // tm),
        in_specs=[pl.BlockSpec((1, tm, d), lambda b, i: (b, i, 0)),
                  pl.BlockSpec((1, N_MOD, d), lambda b, i: (b, 0, 0)),
                  pl.BlockSpec((1, d), lambda b, i: (0, 0)),
                  pl.BlockSpec((d, gdn_cols + mla_cols), lambda b, i: (0, 0))],
        out_specs=[pl.BlockSpec((1, tm, gdn_cols), lambda b, i: (b, i, 0)),
                   pl.BlockSpec((1, tm, mla_cols), lambda b, i: (b, i, 0))],
        out_shape=[jax.ShapeDtypeStruct((bsz, s, gdn_cols), F32),
                   jax.ShapeDtypeStruct((bsz, s, mla_cols), F32)],
        compiler_params=_params("parallel", "parallel"),
        name="in_proj",
    )(x, mod, g, w_cat)


def _gdn_kernel(x_ref, cw_ref, alog_ref, dtb_ref, ng_ref, o_ref,
                state_ref, tail_ref, buf_ref, vnew_ref):
    sb = x_ref.shape[1]
    nchunk = sb // CHUNK
    hdim = GDN_HEAD_DIM
    gw = GDN_HEADS * hdim

    @pl.when(pl.program_id(1) == 0)
    def _():
        state_ref[...] = jnp.zeros_like(state_ref)
        tail_ref[...] = jnp.zeros_like(tail_ref)

    def conv_silu(slot):
        x = x_ref[0, :, slot * hdim:(slot + 1) * hdim]
        buf_ref[slot, 0:SUBLANES, :] = tail_ref[slot]
        buf_ref[slot, SUBLANES:SUBLANES + sb, :] = x
        tail_ref[slot] = x[sb - SUBLANES:sb, :]
        cw = cw_ref[slot]
        y = cw[CONV_WIDTH - 1:CONV_WIDTH] * x
        for j in range(CONV_WIDTH - 1):
            off = SUBLANES - (CONV_WIDTH - 1) + j
            y = y + cw[j:j + 1] * buf_ref[slot, off:off + sb, :]
        return _silu(y)

    def l2n(x):
        return x * lax.rsqrt(jnp.sum(x * x, axis=-1, keepdims=True) + EPS)

    ri = lax.broadcasted_iota(jnp.int32, (sb, sb), 0)
    ci = lax.broadcasted_iota(jnp.int32, (sb, sb), 1)
    same = (ri // CHUNK) == (ci // CHUNK)
    incl = same & (ci <= ri)
    strict = same & (ci < ri)
    eye = (ri == ci).astype(F32)
    levels = []
    bs = 1
    while bs < CHUNK:
        levels.append(((ri // bs) % 2 == 1) & ((ci // bs) % 2 == 0)
                      & ((ri // (2 * bs)) == (ci // (2 * bs))))
        bs *= 2
    rowmod = lax.broadcasted_iota(jnp.int32, (sb, hdim), 0) % CHUNK
    ab = x_ref[0, :, 4 * gw:4 * gw + LANES]

    heads = range(GDN_HEADS)
    q = [l2n(conv_silu(h)) * (hdim ** -0.5) for h in heads]
    k = [l2n(conv_silu(GDN_HEADS + h)) for h in heads]
    v = [conv_silu(2 * GDN_HEADS + h) for h in heads]
    beta = [jax.nn.sigmoid(ab[:, GDN_HEADS + h:GDN_HEADS + h + 1]) for h in heads]

    gc = [-jnp.exp(alog_ref[h]) * jax.nn.softplus(ab[:, h:h + 1] + dtb_ref[h]) for h in heads]
    sh = 1
    while sh < CHUNK:
        gc = [g + jnp.where(rowmod >= sh, pltpu.roll(g, sh, axis=0), 0.0) for g in gc]
        sh *= 2
    gcl = [jnp.concatenate(
        [jnp.broadcast_to(g[(n + 1) * CHUNK - 1:(n + 1) * CHUNK, :], (CHUNK, hdim))
         for n in range(nchunk)], axis=0) for g in gc]
    decay = []
    for g in gc:
        gc_row = jnp.broadcast_to(g.T[0:1, :], (sb, sb))
        decay.append(jnp.where(incl, jnp.exp(jnp.where(incl, g[:, 0:1] - gc_row, 0.0)), 0.0))

    kb = [k[h] * beta[h] for h in heads]
    k16 = [k[h].astype(BF16) for h in heads]
    mmat = [jnp.where(strict, _dot_nt(kb[h].astype(BF16), k16[h]) * decay[h], 0.0) for h in heads]
    attn = [(_dot_nt(q[h].astype(BF16), k16[h]) * decay[h]).astype(BF16) for h in heads]
    tinv = [eye - jnp.where(levels[0], m, 0.0) for m in mmat]
    for off in levels[1:]:
        t16 = [t.astype(BF16) for t in tinv]
        ta = [_dot(t16[h], jnp.where(off, mmat[h], 0.0).astype(BF16)).astype(BF16) for h in heads]
        tinv = [tinv[h] - _dot(ta[h], t16[h]) for h in heads]

    egc = [jnp.exp(g) for g in gc]
    rhs = [jnp.concatenate([v[h] * beta[h], kb[h] * egc[h]], axis=1) for h in heads]
    sol = [rhs[h] + _dot((tinv[h] - eye).astype(BF16), rhs[h].astype(BF16)) for h in heads]
    u = [x[:, :hdim] for x in sol]
    w16 = [x[:, hdim:].astype(BF16) for x in sol]
    qd16 = [(q[h] * egc[h]).astype(BF16) for h in heads]
    kd16 = [(k[h] * jnp.exp(gcl[h] - gc[h])).astype(BF16) for h in heads]
    g_last = [jnp.exp(x) for x in gcl]

    st = [state_ref[h] for h in heads]
    o_inter = [[] for _ in heads]
    for n in range(nchunk):
        lo = n * CHUNK
        for h in heads:
            st16 = st[h].astype(BF16)
            v_new = u[h][lo:lo + CHUNK] - _dot(w16[h][lo:lo + CHUNK], st16)
            vnew_ref[h, lo:lo + CHUNK, :] = v_new
            o_inter[h].append(_dot(qd16[h][lo:lo + CHUNK], st16))
            st[h] = st[h] * g_last[h][lo:lo + 1, :] + _dot_tn(kd16[h][lo:lo + CHUNK],
                                                             v_new.astype(BF16))
    for h in heads:
        state_ref[h] = st[h]
        o = jnp.concatenate(o_inter[h], axis=0) + _dot(attn[h], vnew_ref[h].astype(BF16))
        z = x_ref[0, :, 3 * gw + h * hdim:3 * gw + (h + 1) * hdim]
        o = _rms(o) * ng_ref[...] * _silu(z)
        o_ref[0, :, h * hdim:(h + 1) * hdim] = o.astype(o_ref.dtype)


def _gdn(gdn_in, conv_w, a_log, dt_bias, norm_g, sb):
    bsz, s, cols = gdn_in.shape
    hdim = GDN_HEAD_DIM
    nh = GDN_HEADS
    cw = conv_w.reshape(CONV_WIDTH, 3 * nh, hdim).transpose(1, 0, 2)
    alog = jnp.broadcast_to(a_log.reshape(nh, 1, 1), (nh, 1, hdim)).astype(F32)
    dtb = jnp.broadcast_to(dt_bias.reshape(nh, 1, 1), (nh, 1, hdim)).astype(F32)
    full = lambda shape: pl.BlockSpec(shape, lambda b, i: (0,) * len(shape))
    return pl.pallas_call(
        _gdn_kernel,
        grid=(bsz, s // sb),
        in_specs=[pl.BlockSpec((1, sb, cols), lambda b, i: (b, i, 0)),
                  full(cw.shape), full(alog.shape), full(dtb.shape), full((1, hdim))],
        out_specs=pl.BlockSpec((1, sb, nh * hdim), lambda b, i: (b, i, 0)),
        out_shape=jax.ShapeDtypeStruct((bsz, s, nh * hdim), BF16),
        scratch_shapes=[pltpu.VMEM((nh, hdim, hdim), F32),
                        pltpu.VMEM((3 * nh, SUBLANES, hdim), F32),
                        pltpu.VMEM((3 * nh, sb + SUBLANES, hdim), F32),
                        pltpu.VMEM((nh, sb, hdim), F32)],
        compiler_params=_params("parallel", "arbitrary"),
        name="gdn",
    )(gdn_in, cw, alog, dtb, norm_g.reshape(1, hdim))


MLA_QK_PAD = 256


def _mla_prep_kernel(m_ref, cos_ref, sin_ref, gq_ref, gkv_ref, wq_ref, wkv_ref,
                     q_ref, k_ref, v_ref):
    m = m_ref[0]
    cosr = cos_ref[0]
    sinr = sin_ref[0]
    cq = m[:, :Q_LORA_RANK]
    ckv = m[:, Q_LORA_RANK:Q_LORA_RANK + KV_LORA_RANK]
    o = Q_LORA_RANK + KV_LORA_RANK
    kr = m[:, o:o + LANES]
    krs = m[:, o + LANES:o + 2 * LANES]
    scale = (QK_NOPE_DIM + QK_ROPE_DIM) ** -0.5
    qa = _dot((_rms(cq) * gq_ref[...]).astype(BF16), wq_ref[...])
    kva = _dot((_rms(ckv) * gkv_ref[...]).astype(BF16), wkv_ref[...])
    k_rope = (kr * cosr + krs * sinr).astype(k_ref.dtype)
    sw0 = MLA_HEADS * MLA_QK_PAD
    for h in range(MLA_HEADS):
        b0 = h * MLA_QK_PAD
        rope = qa[:, b0 + LANES:b0 + 2 * LANES] * cosr + qa[:, sw0 + h * LANES:sw0 + (h + 1) * LANES] * sinr
        q_ref[0, h, :, 0:LANES] = (qa[:, b0:b0 + LANES] * scale).astype(q_ref.dtype)
        q_ref[0, h, :, LANES:2 * LANES] = (rope * scale).astype(q_ref.dtype)
        c0 = h * (QK_NOPE_DIM + V_HEAD_DIM)
        k_ref[0, h, :, 0:LANES] = kva[:, c0:c0 + QK_NOPE_DIM].astype(k_ref.dtype)
        k_ref[0, h, :, LANES:2 * LANES] = k_rope
        v_ref[0, h] = kva[:, c0 + QK_NOPE_DIM:c0 + QK_NOPE_DIM + V_HEAD_DIM].astype(v_ref.dtype)


def _mla_prep(mla_in, cosr, sinr, gq, gkv, wq, wkv, tm):
    bsz, s, mc = mla_in.shape
    nh = MLA_HEADS
    return pl.pallas_call(
        _mla_prep_kernel,
        grid=(bsz, s // tm),
        in_specs=[pl.BlockSpec((1, tm, mc), lambda b, i: (b, i, 0)),
                  pl.BlockSpec((1, tm, LANES), lambda b, i: (b, i, 0)),
                  pl.BlockSpec((1, tm, LANES), lambda b, i: (b, i, 0)),
                  pl.BlockSpec((1, Q_LORA_RANK), lambda b, i: (0, 0)),
                  pl.BlockSpec((1, KV_LORA_RANK), lambda b, i: (0, 0)),
                  pl.BlockSpec(wq.shape, lambda b, i: (0, 0)),
                  pl.BlockSpec(wkv.shape, lambda b, i: (0, 0))],
        out_specs=[pl.BlockSpec((1, nh, tm, MLA_QK_PAD), lambda b, i: (b, 0, i, 0)),
                   pl.BlockSpec((1, nh, tm, MLA_QK_PAD), lambda b, i: (b, 0, i, 0)),
                   pl.BlockSpec((1, nh, tm, V_HEAD_DIM), lambda b, i: (b, 0, i, 0))],
        out_shape=[jax.ShapeDtypeStruct((bsz, nh, s, MLA_QK_PAD), BF16),
                   jax.ShapeDtypeStruct((bsz, nh, s, MLA_QK_PAD), BF16),
                   jax.ShapeDtypeStruct((bsz, nh, s, V_HEAD_DIM), BF16)],
        compiler_params=_params("parallel", "parallel"),
        name="mla_prep",
    )(mla_in, cosr, sinr, gq, gkv, wq, wkv)


def _fold_lanes(x, op):
    parts = [x[:, c * LANES:(c + 1) * LANES] for c in range(x.shape[1] // LANES)]
    while len(parts) > 1:
        parts = [op(parts[i], parts[i + 1]) for i in range(0, len(parts), 2)]
    return parts[0]


def _attn_kernel(q_ref, k_ref, v_ref, o_ref, s_ref, m_ref, l_ref, acc_ref):
    i = pl.program_id(2)
    tq = q_ref.shape[2]
    m_ref[...] = jnp.full_like(m_ref, -jnp.inf)
    l_ref[...] = jnp.zeros_like(l_ref)
    acc_ref[...] = jnp.zeros_like(acc_ref)

    def scores(j, slot):
        start = pl.multiple_of(j * tq, tq)
        s_ref[slot] = _dot_nt(q_ref[0, 0], k_ref[0, 0, pl.ds(start, tq), :])

    def consume(j, slot, diagonal):
        start = pl.multiple_of(j * tq, tq)
        s = s_ref[slot]
        if diagonal:
            row = lax.broadcasted_iota(jnp.int32, s.shape, 0)
            col = lax.broadcasted_iota(jnp.int32, s.shape, 1)
            s = jnp.where(col <= row, s, -jnp.inf)
        m_prev = m_ref[...]
        m_new = jnp.maximum(m_prev, jnp.max(_fold_lanes(s, jnp.maximum), axis=-1, keepdims=True))
        alpha = jnp.exp(m_prev - m_new)
        p = jnp.exp(s - m_new)
        l_ref[...] = alpha * l_ref[...] + jnp.sum(_fold_lanes(p, jnp.add), axis=-1, keepdims=True)
        acc_ref[...] = alpha * acc_ref[...] + _dot(p.astype(v_ref.dtype),
                                                   v_ref[0, 0, pl.ds(start, tq), :])
        m_ref[...] = m_new

    scores(0, 0)

    def pair(t, carry):
        scores(2 * t + 1, 1)
        consume(2 * t, 0, False)
        scores(2 * t + 2, 0)
        consume(2 * t + 1, 1, False)
        return carry
    lax.fori_loop(0, i // 2, pair, 0)

    @pl.when(i % 2 == 1)
    def _():
        scores(i, 1)
        consume(i - 1, 0, False)
        consume(i, 1, True)

    @pl.when(i % 2 == 0)
    def _():
        consume(i, 0, True)

    o_ref[0] = (acc_ref[...] / l_ref[...]).astype(o_ref.dtype)


def _attn(q, k, v, tq):
    bsz, nh, s, dq = q.shape
    dv = v.shape[-1]
    return pl.pallas_call(
        _attn_kernel,
        grid=(bsz, nh, s // tq),
        in_specs=[pl.BlockSpec((1, 1, tq, dq), lambda b, h, i: (b, h, i, 0)),
                  pl.BlockSpec((1, 1, s, dq), lambda b, h, i: (b, h, 0, 0)),
                  pl.BlockSpec((1, 1, s, dv), lambda b, h, i: (b, h, 0, 0))],
        out_specs=pl.BlockSpec((1, tq, dv), lambda b, h, i: (b, i, h)),
        out_shape=jax.ShapeDtypeStruct((bsz, s, nh * dv), BF16),
        scratch_shapes=[pltpu.VMEM((2, tq, tq), F32),
                        pltpu.VMEM((tq, 1), F32), pltpu.VMEM((tq, 1), F32),
                        pltpu.VMEM((tq, dv), F32)],
        compiler_params=_params("parallel", "parallel", "arbitrary"),
        name="attn",
    )(q, k, v)


def _outproj_kernel(og_ref, om_ref, x_ref, mod_ref, wo_ref, g_ref, wpq_ref, keys_ref,
                    x1_ref, h2_ref, sc_ref):
    m = mod_ref[0]
    gw = og_ref.shape[-1]
    mixed = _dot(og_ref[0], wo_ref[0:gw, :]) + _dot(om_ref[0], wo_ref[gw:, :])
    x1 = x_ref[0] + m[2:3] * mixed
    h2 = _rms(x1) * g_ref[...] * (1.0 + m[4:5]) + m[3:4]
    x1_ref[0] = x1
    h2_ref[0] = h2.astype(h2_ref.dtype)
    qp = _dot(h2.astype(BF16), wpq_ref[...]).astype(BF16)
    for h in range(PEER_HEADS):
        for p in range(2):
            c0 = (2 * h + p) * N_KEYS
            sc_ref[0, :, c0:c0 + N_KEYS] = _dot_nt(qp[:, c0:c0 + N_KEYS], keys_ref[p, h])


def _out_proj(o_gdn, o_mla, x, mod, w_out, g, w_pq, keys, tm):
    bsz, s, d = x.shape
    gw = o_gdn.shape[-1]
    mw = o_mla.shape[-1]
    nq = w_pq.shape[1]
    return pl.pallas_call(
        _outproj_kernel,
        grid=(bsz, s // tm),
        in_specs=[pl.BlockSpec((1, tm, gw), lambda b, i: (b, i, 0)),
                  pl.BlockSpec((1, tm, mw), lambda b, i: (b, i, 0)),
                  pl.BlockSpec((1, tm, d), lambda b, i: (b, i, 0)),
                  pl.BlockSpec((1, N_MOD, d), lambda b, i: (b, 0, 0)),
                  pl.BlockSpec((gw + mw, d), lambda b, i: (0, 0)),
                  pl.BlockSpec((1, d), lambda b, i: (0, 0)),
                  pl.BlockSpec((d, nq), lambda b, i: (0, 0)),
                  pl.BlockSpec(keys.shape, lambda b, i: (0, 0, 0, 0))],
        out_specs=[pl.BlockSpec((1, tm, d), lambda b, i: (b, i, 0)),
                   pl.BlockSpec((1, tm, d), lambda b, i: (b, i, 0)),
                   pl.BlockSpec((1, tm, nq), lambda b, i: (b, i, 0))],
        out_shape=[jax.ShapeDtypeStruct((bsz, s, d), F32),
                   jax.ShapeDtypeStruct((bsz, s, d), BF16),
                   jax.ShapeDtypeStruct((bsz, s, nq), F32)],
        compiler_params=_params("parallel", "parallel"),
        name="out_proj",
    )(o_gdn, o_mla, x, mod, w_out, g, w_pq, keys)


PEER_SORT_TOKENS = 8


def _peer_key_topk(scores, part, nparts, after=None):
    t_all, width = scores.shape
    nk = N_KEYS
    ln = SC_LANES
    assert PEER_TOPK == ln
    nv = nk // ln
    rpt = width // nk
    grp = PEER_SORT_TOKENS
    toks = t_all // nparts
    tok_per_w = toks // SC_WORKERS
    n_groups = tok_per_w // grp
    assert n_groups * grp * SC_WORKERS * nparts == t_all and rpt % 2 == 0
    tok_base = part * toks
    rows = toks * rpt
    mesh = plsc.VectorSubcoreMesh(core_axis_name="c", subcore_axis_name="s",
                                  num_cores=SC_CORES, num_subcores=SC_SUBCORES)

    @functools.partial(
        pl.kernel, mesh=mesh,
        compiler_params=pltpu.CompilerParams(needs_layout_passes=False),
        out_type=[jax.ShapeDtypeStruct((rows * ln,), F32),
                  jax.ShapeDtypeStruct((rows * ln,), jnp.int32)],
        scratch_types=[pltpu.VMEM((grp, width), F32),
                       pltpu.VMEM((grp * rpt * ln,), F32),
                       pltpu.VMEM((grp * rpt * ln,), jnp.int32)],
    )
    def keys_kernel(s_hbm, *rest):
        ov_hbm, oi_hbm, buf, outv, outi = rest[-5:]
        wid = lax.axis_index("s") * SC_CORES + lax.axis_index("c")
        lane = lax.iota(jnp.int32, ln)

        def merge(a, b):
            ka, va = a
            kb, vb = b
            kbr = lax.rev(kb, (0,))
            vbr = lax.rev(vb, (0,))
            take_a = ka >= kbr
            kc = jnp.where(take_a, ka, kbr)
            vc = jnp.where(take_a, va, vbr)
            vi, ki = plsc.sort_key_val(vc, kc, descending=False)
            return plsc.sort_key_val(ki, vi, descending=True)

        def one_row(t, row):
            parts = []
            for j in range(nv):
                kj = buf[t, pl.ds(row * nk + j * ln, ln)]
                parts.append(plsc.sort_key_val(kj, lane + j * ln, descending=True))
            while len(parts) > 1:
                parts = [merge(parts[i], parts[i + 1]) for i in range(0, len(parts), 2)]
            out = (t * rpt + row) * ln
            outv[pl.ds(out, ln)] = parts[0][0]
            outi[pl.ds(out, ln)] = parts[0][1]

        def group_body(g, carry):
            tok0 = wid * tok_per_w + g * grp
            pltpu.sync_copy(s_hbm.at[pl.ds(tok_base + tok0, grp)], buf)

            def rows_body(i, c2):
                t = i // (rpt // 2)
                row = (i % (rpt // 2)) * 2
                one_row(t, row)
                one_row(t, row + 1)
                return c2
            lax.fori_loop(0, grp * rpt // 2, rows_body, 0)
            pltpu.sync_copy(outv, ov_hbm.at[pl.ds(tok0 * rpt * ln, grp * rpt * ln)])
            pltpu.sync_copy(outi, oi_hbm.at[pl.ds(tok0 * rpt * ln, grp * rpt * ln)])
            return carry
        lax.fori_loop(0, n_groups, group_body, 0)

    args = (scores,) if after is None else (scores, after)
    ov, oi = keys_kernel(*args)
    return ov.reshape(toks, rpt * ln), oi.reshape(toks, rpt * ln)


PEER_CAND_ROWS = 2 * SUBLANES + (SUBLANES - 1) * SUBLANES + SUBLANES


def _peer_select_kernel(stop_in_ref, itop_in_ref, *rest):
    idx_ref, gate_ref, stop_ref, itop_ref, cand_ref, cidx_ref, best_ref, idxt_ref, gatet_ref = rest[-9:]
    tb = stop_in_ref.shape[0]
    kk = PEER_TOPK
    assert kk == 2 * SUBLANES
    stop_ref[...] = stop_in_ref[...].T
    itop_ref[...] = itop_in_ref[...].astype(F32).T

    def extract(vals, row, n):
        m = jnp.max(vals, axis=0, keepdims=True)
        pos = jnp.min(jnp.where(vals == m, row, float(n)), axis=0, keepdims=True)
        return m, pos, jnp.where(row == pos, -jnp.inf, vals)

    row_c = lax.broadcasted_iota(jnp.int32, (PEER_CAND_ROWS, tb), 0).astype(F32)
    row_8 = lax.broadcasted_iota(jnp.int32, (SUBLANES, tb), 0)

    def build_candidates(h):
        r1 = 2 * h * kk
        r2 = r1 + kk
        cand_ref[h, 0:kk, :] = stop_ref[r1:r1 + 1, :] + stop_ref[r2:r2 + kk, :]
        cidx_ref[h, 0:kk, :] = itop_ref[r1:r1 + 1, :] * float(N_KEYS) + itop_ref[r2:r2 + kk, :]
        s2 = stop_ref[r2:r2 + SUBLANES, :]
        i2 = itop_ref[r2:r2 + SUBLANES, :]
        for a in range(1, SUBLANES):
            r0 = kk + (a - 1) * SUBLANES
            cand_ref[h, r0:r0 + SUBLANES, :] = jnp.where(
                row_8 < kk // (a + 1), stop_ref[r1 + a:r1 + a + 1, :] + s2, -jnp.inf)
            cidx_ref[h, r0:r0 + SUBLANES, :] = itop_ref[r1 + a:r1 + a + 1, :] * float(N_KEYS) + i2
        r0 = kk + (SUBLANES - 1) * SUBLANES
        cand_ref[h, r0:r0 + SUBLANES, :] = (stop_ref[r1 + SUBLANES:r1 + kk, :]
                                           + stop_ref[r2:r2 + 1, :])
        cidx_ref[h, r0:r0 + SUBLANES, :] = (itop_ref[r1 + SUBLANES:r1 + kk, :] * float(N_KEYS)
                                           + itop_ref[r2:r2 + 1, :])

    for h in range(PEER_HEADS):
        build_candidates(h)
    cvals = [cand_ref[h] for h in range(PEER_HEADS)]
    cidx = [cidx_ref[h] for h in range(PEER_HEADS)]
    for r in range(kk):
        for h in range(PEER_HEADS):
            m, pos, cvals[h] = extract(cvals[h], row_c, PEER_CAND_ROWS)
            best_ref[h, r:r + 1, :] = m
            idxt_ref[h * kk + r:h * kk + r + 1, :] = jnp.max(
                jnp.where(row_c == pos, cidx[h], -1.0), axis=0, keepdims=True)
    for h in range(PEER_HEADS):
        best = best_ref[h]
        e = jnp.exp(best - best[0:1, :])
        gatet_ref[h * kk:(h + 1) * kk, :] = e / jnp.sum(e, axis=0, keepdims=True)
    idx_ref[...] = idxt_ref[...].T.astype(jnp.int32)
    gate_ref[...] = gatet_ref[...].T


def _peer_select(stop, itop, tb, after=None):
    t, w = stop.shape
    hk = PEER_HEADS * PEER_TOPK
    kk = PEER_TOPK
    in_specs = [pl.BlockSpec((tb, w), lambda i: (i, 0)),
                pl.BlockSpec((tb, w), lambda i: (i, 0))]
    args = [stop, itop]
    if after is not None:
        in_specs.append(pl.BlockSpec(memory_space=pl.ANY))
        args.append(after)
    return pl.pallas_call(
        _peer_select_kernel,
        grid=(t // tb,),
        in_specs=in_specs,
        out_specs=[pl.BlockSpec((tb, hk), lambda i: (i, 0)),
                   pl.BlockSpec((tb, hk), lambda i: (i, 0))],
        out_shape=[jax.ShapeDtypeStruct((t, hk), jnp.int32),
                   jax.ShapeDtypeStruct((t, hk), F32)],
        scratch_shapes=[pltpu.VMEM((w, tb), F32), pltpu.VMEM((w, tb), F32),
                        pltpu.VMEM((PEER_HEADS, PEER_CAND_ROWS, tb), F32),
                        pltpu.VMEM((PEER_HEADS, PEER_CAND_ROWS, tb), F32),
                        pltpu.VMEM((PEER_HEADS, kk, tb), F32),
                        pltpu.VMEM((hk, tb), F32), pltpu.VMEM((hk, tb), F32)],
        compiler_params=_params("parallel"),
        name="peer_select",
    )(*args)


PEER_TOKEN_GROUP = 16
PEER_PARTS = 8


def _peer_gate_matrix(idx, gate, n_experts):
    t_total, hk = idx.shape
    ln = SC_LANES
    nu = hk // ln
    grp = PEER_TOKEN_GROUP
    tok_per_w = t_total // SC_WORKERS
    n_groups = tok_per_w // grp
    assert n_groups * grp * SC_WORKERS == t_total and grp % 2 == 0
    mesh = plsc.VectorSubcoreMesh(core_axis_name="c", subcore_axis_name="s",
                                  num_cores=SC_CORES, num_subcores=SC_SUBCORES)

    @functools.partial(
        pl.kernel, mesh=mesh,
        compiler_params=pltpu.CompilerParams(needs_layout_passes=False),
        out_type=jax.ShapeDtypeStruct((t_total, n_experts), F32),
        scratch_types=[
            pltpu.VMEM((grp * hk,), jnp.int32),
            pltpu.VMEM((grp * hk,), F32),
            pltpu.VMEM((n_experts,), F32),
            pltpu.VMEM((n_experts,), F32),
            pltpu.SemaphoreType.DMA((2,)),
        ],
    )
    def gate_kernel(idx_hbm, gate_hbm, g_hbm, idx_v, gate_v, row0_v, row1_v, sem):
        rows = (row0_v, row1_v)
        wid = lax.axis_index("s") * SC_CORES + lax.axis_index("c")
        base = wid * tok_per_w
        zero = jnp.zeros((ln,), F32)

        def zero_body(c, carry):
            row0_v[pl.ds(c * ln, ln)] = zero
            row1_v[pl.ds(c * ln, ln)] = zero
            return carry
        lax.fori_loop(0, n_experts // ln, zero_body, 0)

        def out_copy(tok, slot):
            return pltpu.make_async_copy(rows[slot], g_hbm.at[tok], sem.at[slot])

        def group_body(g, carry):
            tok0 = base + g * grp
            pltpu.sync_copy(idx_hbm.at[pl.ds(tok0 * hk, grp * hk)], idx_v)
            pltpu.sync_copy(gate_hbm.at[pl.ds(tok0 * hk, grp * hk)], gate_v)

            def pair_body(i, carry2):
                for slot in range(2):
                    t = i * 2 + slot
                    for u in range(nu):
                        sl = pl.ds(t * hk + u * ln, ln)
                        plsc.addupdate_scatter(rows[slot], [idx_v[sl]], gate_v[sl])
                    out_copy(tok0 + t, slot).start()
                for slot in range(2):
                    t = i * 2 + slot
                    out_copy(tok0 + t, slot).wait()
                    for u in range(nu):
                        plsc.store_scatter(rows[slot], [idx_v[pl.ds(t * hk + u * ln, ln)]], zero)
                return carry2
            lax.fori_loop(0, grp // 2, pair_body, 0)
            return carry
        lax.fori_loop(0, n_groups, group_body, 0)

    return gate_kernel(idx.reshape(-1), gate.reshape(-1))


def _peer_dense_kernel(h_ref, g_ref, u_ref, v_ref, x1_ref, mod_ref, ng_ref, *rest, normalize):
    o_ref = rest[-1]
    e = pl.program_id(1)
    s = _dot_nt(h_ref[...], u_ref[...])
    p = (jax.nn.gelu(s) * g_ref[...]).astype(v_ref.dtype)
    contrib = _dot(p, v_ref[...])

    @pl.when(e == 0)
    def _():
        o_ref[...] = contrib

    @pl.when(e > 0)
    def _():
        o_ref[...] += contrib

    @pl.when(e == pl.num_programs(1) - 1)
    def _():
        x2 = x1_ref[...] + mod_ref[0][5:6] * o_ref[...]
        o_ref[...] = _rms(x2) * ng_ref[...] if normalize else x2


def _peer_dense(h, gmat, u, v, x1, mod, ng, tb, eb, part, normalize, prev, after=None):
    t = gmat.shape[0]
    t_all, d = h.shape
    n_e = u.shape[0]
    blk0 = part * (t // tb)
    blk_per_batch = t_all // mod.shape[0] // tb
    tok_blk = pl.BlockSpec((tb, d), lambda i, e: (i + blk0, 0))
    in_specs = [tok_blk,
                pl.BlockSpec((tb, eb), lambda i, e: (i, e)),
                pl.BlockSpec((eb, d), lambda i, e: (e, 0)),
                pl.BlockSpec((eb, d), lambda i, e: (e, 0)),
                tok_blk,
                pl.BlockSpec((1, N_MOD, d), lambda i, e: ((i + blk0) // blk_per_batch, 0, 0)),
                pl.BlockSpec((1, d), lambda i, e: (0, 0))]
    args = [h, gmat, u, v, x1, mod, ng]
    aliases = {}
    if prev is not None:
        in_specs.append(pl.BlockSpec(memory_space=pl.ANY))
        args.append(prev)
        aliases = {len(args) - 1: 0}
    if after is not None:
        in_specs.append(pl.BlockSpec(memory_space=pl.ANY))
        args.append(after)
    return pl.pallas_call(
        functools.partial(_peer_dense_kernel, normalize=normalize),
        grid=(t // tb, n_e // eb),
        in_specs=in_specs,
        out_specs=tok_blk,
        out_shape=jax.ShapeDtypeStruct((t_all, d), F32),
        input_output_aliases=aliases,
        compiler_params=pltpu.CompilerParams(dimension_semantics=("parallel", "arbitrary"),
                                             vmem_limit_bytes=PEER_DENSE_VMEM),
        name="peer_dense",
    )(*args)


def _split_w_in(w):
    gw = GDN_HEADS * GDN_HEAD_DIM
    sizes = [gw] * 4 + [GDN_HEADS] * 2 + [Q_LORA_RANK, KV_LORA_RANK, QK_ROPE_DIM]
    offs = [0]
    for sz in sizes:
        offs.append(offs[-1] + sz)
    parts = [w[:, offs[i]:offs[i + 1]] for i in range(len(sizes))]
    gq, gk, gv, gz, ga, gb, cq, ckv, kr = parts
    d = w.shape[0]
    half = QK_ROPE_DIM // 2
    zeros = lambda n: jnp.zeros((d, n), w.dtype)
    gdn = jnp.concatenate([gq, gk, gv, gz, ga, gb, zeros(LANES - 2 * GDN_HEADS)], axis=1)
    kr_sw = jnp.concatenate([kr[:, half:], kr[:, :half]], axis=1)
    mla = jnp.concatenate([cq, ckv, kr, zeros(LANES - QK_ROPE_DIM),
                           kr_sw, zeros(LANES - QK_ROPE_DIM)], axis=1)
    return jnp.concatenate([gdn, mla], axis=1).astype(BF16), gdn.shape[1], mla.shape[1]


def _split_w_uq(w):
    r = w.shape[0]
    half = QK_ROPE_DIM // 2
    hd = QK_NOPE_DIM + QK_ROPE_DIM
    main, swapped = [], []
    for h in range(MLA_HEADS):
        nope = w[:, h * hd:h * hd + QK_NOPE_DIM]
        rope = w[:, h * hd + QK_NOPE_DIM:(h + 1) * hd]
        main += [nope, rope, jnp.zeros((r, MLA_QK_PAD - hd), w.dtype)]
        swapped += [rope[:, half:], rope[:, :half], jnp.zeros((r, LANES - QK_ROPE_DIM), w.dtype)]
    return jnp.concatenate(main + swapped, axis=1).astype(BF16)


def _rotary_tables(positions):
    half = QK_ROPE_DIM // 2
    inv_freq = ROPE_THETA ** (-jnp.arange(half, dtype=F32) / half)
    ang = positions.astype(F32)[..., None] * inv_freq
    cos, sin = jnp.cos(ang), jnp.sin(ang)
    z = jnp.zeros(cos.shape[:-1] + (LANES - QK_ROPE_DIM,), F32)
    return (jnp.concatenate([cos, cos, z], axis=-1),
            jnp.concatenate([-sin, sin, z], axis=-1))


def _block(n, pref):
    return pref if n % pref == 0 else n


def kernel(x, c, positions, ln_mix_g, w_in, conv_w, a_log, dt_bias, gdn_norm_g, q_norm_g, w_uq, kv_norm_g, w_ukv, w_out, ln_ffn_g, w_pq, sub_keys, expert_u, expert_v, w_ada, b_ada, final_norm_g):
    bsz, s, d = x.shape
    depth = w_in.shape[0]
    cosr, sinr = _rotary_tables(positions)
    tm = _block(s, 512)
    for layer in range(depth):
        mod = _mod(c, w_ada[layer], b_ada[layer]).reshape(bsz, N_MOD, d)
        w_cat, gdn_cols, mla_cols = _split_w_in(w_in[layer])
        gdn_in, mla_in = _in_proj(x, mod, ln_mix_g[layer].reshape(1, d), w_cat, gdn_cols, mla_cols, tm)
        o_gdn = _gdn(gdn_in, conv_w[layer], a_log[layer], dt_bias[layer], gdn_norm_g[layer],
                     _block(s, 256))
        q, k, v = _mla_prep(mla_in, cosr, sinr, q_norm_g[layer].reshape(1, -1),
                            kv_norm_g[layer].reshape(1, -1), _split_w_uq(w_uq[layer]),
                            w_ukv[layer].astype(BF16), _block(s, 512))
        o_mla = _attn(q, k, v, _block(s, 1024))
        x1, h2, scores = _out_proj(o_gdn, o_mla, x, mod, w_out[layer].astype(BF16),
                                   ln_ffn_g[layer].reshape(1, d), w_pq[layer].astype(BF16),
                                   sub_keys[layer].astype(BF16), tm)
        n_experts = expert_u.shape[1]
        u16 = expert_u[layer].astype(BF16)
        v16 = expert_v[layer].astype(BF16)
        scores2 = scores.reshape(bsz * s, -1)
        h22 = h2.reshape(bsz * s, d)
        x12 = x1.reshape(bsz * s, d)
        nparts = PEER_PARTS if bsz % PEER_PARTS == 0 else 1
        tpart = bsz * s // nparts
        last = layer + 1 == depth
        sel = {}
        gmats = {}

        def select(part, dense_out):
            stop, itop = _peer_key_topk(scores2, part, nparts, gmats.get(part - 2))
            sel[part] = _peer_select(stop, itop, _block(tpart, 256), dense_out)

        out = None
        select(0, None)
        for part in range(nparts):
            if part + 1 < nparts:
                select(part + 1, out)
            idx, gate = sel[part]
            gmats[part] = _peer_gate_matrix(idx, gate, n_experts)
            after = sel[part + 1][0] if part + 1 < nparts else None
            out = _peer_dense(h22, gmats[part], u16, v16, x12, mod, final_norm_g.reshape(1, d),
                              _block(tpart, 1024), _block(n_experts, 1024), part, last, out, after)
        x = out.reshape(bsz, s, d)
    return x
```

```python
import functools

import jax
import jax.numpy as jnp
from jax import lax
from jax.experimental import pallas as pl
from jax.experimental.pallas import tpu as pltpu
from jax.experimental.pallas import tpu_sc as plsc

GDN_HEADS = 4
GDN_HEAD_DIM = 128
CONV_WIDTH = 4
CHUNK = 64
MLA_HEADS = 4
QK_NOPE_DIM = 128
QK_ROPE_DIM = 64
V_HEAD_DIM = 128
Q_LORA_RANK = 384
KV_LORA_RANK = 256
ROPE_THETA = 10000.0
PEER_HEADS = 8
N_KEYS = 128
PEER_TOPK = 16
N_MOD = 6
EPS = 1e-6

LANES = 128
SUBLANES = 8
SC_CORES = 2
SC_SUBCORES = 16
SC_LANES = 16
SC_WORKERS = SC_CORES * SC_SUBCORES
VMEM_LIMIT = 48 * 1024 * 1024
PEER_DENSE_VMEM = 56 * 1024 * 1024

F32 = jnp.float32
BF16 = jnp.bfloat16
HI = lax.Precision.HIGHEST


def _dot(a, b, precision=None):
    return jnp.dot(a, b, preferred_element_type=F32, precision=precision)


def _dot_nt(a, b, precision=None):
    return lax.dot_general(a, b, (((1,), (1,)), ((), ())),
                           preferred_element_type=F32, precision=precision)


def _dot_tn(a, b, precision=None):
    return lax.dot_general(a, b, (((0,), (0,)), ((), ())),
                           preferred_element_type=F32, precision=precision)


def _rms(x):
    return x * lax.rsqrt(jnp.mean(x * x, axis=-1, keepdims=True) + EPS)


def _silu(x):
    return x * jax.nn.sigmoid(x)


def _params(*sem):
    return pltpu.CompilerParams(dimension_semantics=sem, vmem_limit_bytes=VMEM_LIMIT)


def _mod_kernel(c_ref, w_ref, b_ref, o_ref):
    o_ref[...] = _dot(_silu(c_ref[...]), w_ref[...], HI) + b_ref[...]


def _mod(c, w_ada, b_ada):
    bsz, d = c.shape
    n = w_ada.shape[1]
    return pl.pallas_call(
        _mod_kernel,
        grid=(n // d,),
        in_specs=[pl.BlockSpec((bsz, d), lambda j: (0, 0)),
                  pl.BlockSpec((d, d), lambda j: (0, j)),
                  pl.BlockSpec((1, d), lambda j: (0, j))],
        out_specs=pl.BlockSpec((bsz, d), lambda j: (0, j)),
        out_shape=jax.ShapeDtypeStruct((bsz, n), F32),
        compiler_params=_params("arbitrary"),
        name="mod",
    )(c, w_ada, b_ada.reshape(1, n))


def _inproj_kernel(x_ref, mod_ref, g_ref, w_ref, gdn_ref, mla_ref):
    m = mod_ref[0]
    h = _rms(x_ref[0]) * g_ref[...] * (1.0 + m[1:2]) + m[0:1]
    p = _dot(h.astype(BF16), w_ref[...])
    gw = gdn_ref.shape[-1]
    gdn_ref[0] = p[:, :gw]
    mla_ref[0] = p[:, gw:]


def _in_proj(x, mod, g, w_cat, gdn_cols, mla_cols, tm):
    bsz, s, d = x.shape
    return pl.pallas_call(
        _inproj_kernel,
        grid=(bsz, s // tm),
        in_specs=[pl.BlockSpec((1, tm, d), lambda b, i: (b, i, 0)),
                  pl.BlockSpec((1, N_MOD, d), lambda b, i: (b, 0, 0)),
                  pl.BlockSpec((1, d), lambda b, i: (0, 0)),
                  pl.BlockSpec((d, gdn_cols + mla_cols), lambda b, i: (0, 0))],
        out_specs=[pl.BlockSpec((1, tm, gdn_cols), lambda b, i: (b, i, 0)),
                   pl.BlockSpec((1, tm, mla_cols), lambda b, i: (b, i, 0))],
        out_shape=[jax.ShapeDtypeStruct((bsz, s, gdn_cols), F32),
                   jax.ShapeDtypeStruct((bsz, s, mla_cols), F32)],
        compiler_params=_params("parallel", "parallel"),
        name="in_proj",
    )(x, mod, g, w_cat)


def _gdn_kernel(x_ref, cw_ref, alog_ref, dtb_ref, ng_ref, o_ref,
                state_ref, tail_ref, buf_ref, vnew_ref):
    sb = x_ref.shape[1]
    nchunk = sb // CHUNK
    hdim = GDN_HEAD_DIM
    gw = GDN_HEADS * hdim

    @pl.when(pl.program_id(1) == 0)
    def _():
        state_ref[...] = jnp.zeros_like(state_ref)
        tail_ref[...] = jnp.zeros_like(tail_ref)

    def conv_silu(slot):
        x = x_ref[0, :, slot * hdim:(slot + 1) * hdim]
        buf_ref[slot, 0:SUBLANES, :] = tail_ref[slot]
        buf_ref[slot, SUBLANES:SUBLANES + sb, :] = x
        tail_ref[slot] = x[sb - SUBLANES:sb, :]
        cw = cw_ref[slot]
        y = cw[CONV_WIDTH - 1:CONV_WIDTH] * x
        for j in range(CONV_WIDTH - 1):
            off = SUBLANES - (CONV_WIDTH - 1) + j
            y = y + cw[j:j + 1] * buf_ref[slot, off:off + sb, :]
        return _silu(y)

    def l2n(x):
        return x * lax.rsqrt(jnp.sum(x * x, axis=-1, keepdims=True) + EPS)

    ri = lax.broadcasted_iota(jnp.int32, (sb, sb), 0)
    ci = lax.broadcasted_iota(jnp.int32, (sb, sb), 1)
    same = (ri // CHUNK) == (ci // CHUNK)
    incl = same & (ci <= ri)
    strict = same & (ci < ri)
    eye = (ri == ci).astype(F32)
    levels = []
    bs = 1
    while bs < CHUNK:
        levels.append(((ri // bs) % 2 == 1) & ((ci // bs) % 2 == 0)
                      & ((ri // (2 * bs)) == (ci // (2 * bs))))
        bs *= 2
    rowmod = lax.broadcasted_iota(jnp.int32, (sb, hdim), 0) % CHUNK
    ab = x_ref[0, :, 4 * gw:4 * gw + LANES]

    heads = range(GDN_HEADS)
    q = [l2n(conv_silu(h)) * (hdim ** -0.5) for h in heads]
    k = [l2n(conv_silu(GDN_HEADS + h)) for h in heads]
    v = [conv_silu(2 * GDN_HEADS + h) for h in heads]
    beta = [jax.nn.sigmoid(ab[:, GDN_HEADS + h:GDN_HEADS + h + 1]) for h in heads]

    gc = [-jnp.exp(alog_ref[h]) * jax.nn.softplus(ab[:, h:h + 1] + dtb_ref[h]) for h in heads]
    sh = 1
    while sh < CHUNK:
        gc = [g + jnp.where(rowmod >= sh, pltpu.roll(g, sh, axis=0), 0.0) for g in gc]
        sh *= 2
    gcl = [jnp.concatenate(
        [jnp.broadcast_to(g[(n + 1) * CHUNK - 1:(n + 1) * CHUNK, :], (CHUNK, hdim))
         for n in range(nchunk)], axis=0) for g in gc]
    decay = []
    for g in gc:
        gc_row = jnp.broadcast_to(g.T[0:1, :], (sb, sb))
        decay.append(jnp.where(incl, jnp.exp(jnp.where(incl, g[:, 0:1] - gc_row, 0.0)), 0.0))

    kb = [k[h] * beta[h] for h in heads]
    k16 = [k[h].astype(BF16) for h in heads]
    mmat = [jnp.where(strict, _dot_nt(kb[h].astype(BF16), k16[h]) * decay[h], 0.0) for h in heads]
    attn = [(_dot_nt(q[h].astype(BF16), k16[h]) * decay[h]).astype(BF16) for h in heads]
    tinv = [eye - jnp.where(levels[0], m, 0.0) for m in mmat]
    for off in levels[1:]:
        t16 = [t.astype(BF16) for t in tinv]
        ta = [_dot(t16[h], jnp.where(off, mmat[h], 0.0).astype(BF16)).astype(BF16) for h in heads]
        tinv = [tinv[h] - _dot(ta[h], t16[h]) for h in heads]

    egc = [jnp.exp(g) for g in gc]
    rhs = [jnp.concatenate([v[h] * beta[h], kb[h] * egc[h]], axis=1) for h in heads]
    sol = [rhs[h] + _dot((tinv[h] - eye).astype(BF16), rhs[h].astype(BF16)) for h in heads]
    u = [x[:, :hdim] for x in sol]
    w16 = [x[:, hdim:].astype(BF16) for x in sol]
    qd16 = [(q[h] * egc[h]).astype(BF16) for h in heads]
    kd16 = [(k[h] * jnp.exp(gcl[h] - gc[h])).astype(BF16) for h in heads]
    g_last = [jnp.exp(x) for x in gcl]

    st = [state_ref[h] for h in heads]
    o_inter = [[] for _ in heads]
    for n in range(nchunk):
        lo = n * CHUNK
        for h in heads:
            st16 = st[h].astype(BF16)
            v_new = u[h][lo:lo + CHUNK] - _dot(w16[h][lo:lo + CHUNK], st16)
            vnew_ref[h, lo:lo + CHUNK, :] = v_new
            o_inter[h].append(_dot(qd16[h][lo:lo + CHUNK], st16))
            st[h] = st[h] * g_last[h][lo:lo + 1, :] + _dot_tn(kd16[h][lo:lo + CHUNK],
                                                             v_new.astype(BF16))
    for h in heads:
        state_ref[h] = st[h]
        o = jnp.concatenate(o_inter[h], axis=0) + _dot(attn[h], vnew_ref[h].astype(BF16))
        z = x_ref[0, :, 3 * gw + h * hdim:3 * gw + (h + 1) * hdim]
        o = _rms(o) * ng_ref[...] * _silu(z)
        o_ref[0, :, h * hdim:(h + 1) * hdim] = o.astype(o_ref.dtype)


def _gdn(gdn_in, conv_w, a_log, dt_bias, norm_g, sb):
    bsz, s, cols = gdn_in.shape
    hdim = GDN_HEAD_DIM
    nh = GDN_HEADS
    cw = conv_w.reshape(CONV_WIDTH, 3 * nh, hdim).transpose(1, 0, 2)
    alog = jnp.broadcast_to(a_log.reshape(nh, 1, 1), (nh, 1, hdim)).astype(F32)
    dtb = jnp.broadcast_to(dt_bias.reshape(nh, 1, 1), (nh, 1, hdim)).astype(F32)
    full = lambda shape: pl.BlockSpec(shape, lambda b, i: (0,) * len(shape))
    return pl.pallas_call(
        _gdn_kernel,
        grid=(bsz, s // sb),
        in_specs=[pl.BlockSpec((1, sb, cols), lambda b, i: (b, i, 0)),
                  full(cw.shape), full(alog.shape), full(dtb.shape), full((1, hdim))],
        out_specs=pl.BlockSpec((1, sb, nh * hdim), lambda b, i: (b, i, 0)),
        out_shape=jax.ShapeDtypeStruct((bsz, s, nh * hdim), BF16),
        scratch_shapes=[pltpu.VMEM((nh, hdim, hdim), F32),
                        pltpu.VMEM((3 * nh, SUBLANES, hdim), F32),
                        pltpu.VMEM((3 * nh, sb + SUBLANES, hdim), F32),
                        pltpu.VMEM((nh, sb, hdim), F32)],
        compiler_params=_params("parallel", "arbitrary"),
        name="gdn",
    )(gdn_in, cw, alog, dtb, norm_g.reshape(1, hdim))


MLA_QK_PAD = 256


def _mla_prep_kernel(m_ref, cos_ref, sin_ref, gq_ref, gkv_ref, wq_ref, wkv_ref,
                     q_ref, k_ref, v_ref):
    m = m_ref[0]
    cosr = cos_ref[0]
    sinr = sin_ref[0]
    cq = m[:, :Q_LORA_RANK]
    ckv = m[:, Q_LORA_RANK:Q_LORA_RANK + KV_LORA_RANK]
    o = Q_LORA_RANK + KV_LORA_RANK
    kr = m[:, o:o + LANES]
    krs = m[:, o + LANES:o + 2 * LANES]
    scale = (QK_NOPE_DIM + QK_ROPE_DIM) ** -0.5
    qa = _dot((_rms(cq) * gq_ref[...]).astype(BF16), wq_ref[...])
    kva = _dot((_rms(ckv) * gkv_ref[...]).astype(BF16), wkv_ref[...])
    k_rope = (kr * cosr + krs * sinr).astype(k_ref.dtype)
    sw0 = MLA_HEADS * MLA_QK_PAD
    for h in range(MLA_HEADS):
        b0 = h * MLA_QK_PAD
        rope = qa[:, b0 + LANES:b0 + 2 * LANES] * cosr + qa[:, sw0 + h * LANES:sw0 + (h + 1) * LANES] * sinr
        q_ref[0, h, :, 0:LANES] = (qa[:, b0:b0 + LANES] * scale).astype(q_ref.dtype)
        q_ref[0, h, :, LANES:2 * LANES] = (rope * scale).astype(q_ref.dtype)
        c0 = h * (QK_NOPE_DIM + V_HEAD_DIM)
        k_ref[0, h, :, 0:LANES] = kva[:, c0:c0 + QK_NOPE_DIM].astype(k_ref.dtype)
        k_ref[0, h, :, LANES:2 * LANES] = k_rope
        v_ref[0, h] = kva[:, c0 + QK_NOPE_DIM:c0 + QK_NOPE_DIM + V_HEAD_DIM].astype(v_ref.dtype)


def _mla_prep(mla_in, cosr, sinr, gq, gkv, wq, wkv, tm):
    bsz, s, mc = mla_in.shape
    nh = MLA_HEADS
    return pl.pallas_call(
        _mla_prep_kernel,
        grid=(bsz, s // tm),
        in_specs=[pl.BlockSpec((1, tm, mc), lambda b, i: (b, i, 0)),
                  pl.BlockSpec((1, tm, LANES), lambda b, i: (b, i, 0)),
                  pl.BlockSpec((1, tm, LANES), lambda b, i: (b, i, 0)),
                  pl.BlockSpec((1, Q_LORA_RANK), lambda b, i: (0, 0)),
                  pl.BlockSpec((1, KV_LORA_RANK), lambda b, i: (0, 0)),
                  pl.BlockSpec(wq.shape, lambda b, i: (0, 0)),
                  pl.BlockSpec(wkv.shape, lambda b, i: (0, 0))],
        out_specs=[pl.BlockSpec((1, nh, tm, MLA_QK_PAD), lambda b, i: (b, 0, i, 0)),
                   pl.BlockSpec((1, nh, tm, MLA_QK_PAD), lambda b, i: (b, 0, i, 0)),
                   pl.BlockSpec((1, nh, tm, V_HEAD_DIM), lambda b, i: (b, 0, i, 0))],
        out_shape=[jax.ShapeDtypeStruct((bsz, nh, s, MLA_QK_PAD), BF16),
                   jax.ShapeDtypeStruct((bsz, nh, s, MLA_QK_PAD), BF16),
                   jax.ShapeDtypeStruct((bsz, nh, s, V_HEAD_DIM), BF16)],
        compiler_params=_params("parallel", "parallel"),
        name="mla_prep",
    )(mla_in, cosr, sinr, gq, gkv, wq, wkv)


def _fold_lanes(x, op):
    parts = [x[:, c * LANES:(c + 1) * LANES] for c in range(x.shape[1] // LANES)]
    while len(parts) > 1:
        parts = [op(parts[i], parts[i + 1]) for i in range(0, len(parts), 2)]
    return parts[0]


def _attn_kernel(q_ref, k_ref, v_ref, o_ref, s_ref, m_ref, l_ref, acc_ref):
    i = pl.program_id(2)
    tq = q_ref.shape[2]
    m_ref[...] = jnp.full_like(m_ref, -jnp.inf)
    l_ref[...] = jnp.zeros_like(l_ref)
    acc_ref[...] = jnp.zeros_like(acc_ref)

    def scores(j, slot):
        start = pl.multiple_of(j * tq, tq)
        s_ref[slot] = _dot_nt(q_ref[0, 0], k_ref[0, 0, pl.ds(start, tq), :])

    def consume(j, slot, diagonal):
        start = pl.multiple_of(j * tq, tq)
        s = s_ref[slot]
        if diagonal:
            row = lax.broadcasted_iota(jnp.int32, s.shape, 0)
            col = lax.broadcasted_iota(jnp.int32, s.shape, 1)
            s = jnp.where(col <= row, s, -jnp.inf)
        m_prev = m_ref[...]
        m_new = jnp.maximum(m_prev, jnp.max(_fold_lanes(s, jnp.maximum), axis=-1, keepdims=True))
        alpha = jnp.exp(m_prev - m_new)
        p = jnp.exp(s - m_new)
        l_ref[...] = alpha * l_ref[...] + jnp.sum(_fold_lanes(p, jnp.add), axis=-1, keepdims=True)
        acc_ref[...] = alpha * acc_ref[...] + _dot(p.astype(v_ref.dtype),
                                                   v_ref[0, 0, pl.ds(start, tq), :])
        m_ref[...] = m_new

    scores(0, 0)

    def pair(t, carry):
        scores(2 * t + 1, 1)
        consume(2 * t, 0, False)
        scores(2 * t + 2, 0)
        consume(2 * t + 1, 1, False)
        return carry
    lax.fori_loop(0, i // 2, pair, 0)

    @pl.when(i % 2 == 1)
    def _():
        scores(i, 1)
        consume(i - 1, 0, False)
        consume(i, 1, True)

    @pl.when(i % 2 == 0)
    def _():
        consume(i, 0, True)

    o_ref[0] = (acc_ref[...] / l_ref[...]).astype(o_ref.dtype)


def _attn(q, k, v, tq):
    bsz, nh, s, dq = q.shape
    dv = v.shape[-1]
    return pl.pallas_call(
        _attn_kernel,
        grid=(bsz, nh, s // tq),
        in_specs=[pl.BlockSpec((1, 1, tq, dq), lambda b, h, i: (b, h, i, 0)),
                  pl.BlockSpec((1, 1, s, dq), lambda b, h, i: (b, h, 0, 0)),
                  pl.BlockSpec((1, 1, s, dv), lambda b, h, i: (b, h, 0, 0))],
        out_specs=pl.BlockSpec((1, tq, dv), lambda b, h, i: (b, i, h)),
        out_shape=jax.ShapeDtypeStruct((bsz, s, nh * dv), BF16),
        scratch_shapes=[pltpu.VMEM((2, tq, tq), F32),
                        pltpu.VMEM((tq, 1), F32), pltpu.VMEM((tq, 1), F32),
                        pltpu.VMEM((tq, dv), F32)],
        compiler_params=_params("parallel", "parallel", "arbitrary"),
        name="attn",
    )(q, k, v)


def _outproj_kernel(og_ref, om_ref, x_ref, mod_ref, wo_ref, g_ref, wpq_ref, keys_ref,
                    x1_ref, h2_ref, sc_ref):
    m = mod_ref[0]
    gw = og_ref.shape[-1]
    mixed = _dot(og_ref[0], wo_ref[0:gw, :]) + _dot(om_ref[0], wo_ref[gw:, :])
    x1 = x_ref[0] + m[2:3] * mixed
    h2 = _rms(x1) * g_ref[...] * (1.0 + m[4:5]) + m[3:4]
    x1_ref[0] = x1
    h2_ref[0] = h2.astype(h2_ref.dtype)
    qp = _dot(h2.astype(BF16), wpq_ref[...]).astype(BF16)
    for h in range(PEER_HEADS):
        for p in range(2):
            c0 = (2 * h + p) * N_KEYS
            sc_ref[0, :, c0:c0 + N_KEYS] = _dot_nt(qp[:, c0:c0 + N_KEYS], keys_ref[p, h])


def _out_proj(o_gdn, o_mla, x, mod, w_out, g, w_pq, keys, tm):
    bsz, s, d = x.shape
    gw = o_gdn.shape[-1]
    mw = o_mla.shape[-1]
    nq = w_pq.shape[1]
    return pl.pallas_call(
        _outproj_kernel,
        grid=(bsz, s // tm),
        in_specs=[pl.BlockSpec((1, tm, gw), lambda b, i: (b, i, 0)),
                  pl.BlockSpec((1, tm, mw), lambda b, i: (b, i, 0)),
                  pl.BlockSpec((1, tm, d), lambda b, i: (b, i, 0)),
                  pl.BlockSpec((1, N_MOD, d), lambda b, i: (b, 0, 0)),
                  pl.BlockSpec((gw + mw, d), lambda b, i: (0, 0)),
                  pl.BlockSpec((1, d), lambda b, i: (0, 0)),
                  pl.BlockSpec((d, nq), lambda b, i: (0, 0)),
                  pl.BlockSpec(keys.shape, lambda b, i: (0, 0, 0, 0))],
        out_specs=[pl.BlockSpec((1, tm, d), lambda b, i: (b, i, 0)),
                   pl.BlockSpec((1, tm, d), lambda b, i: (b, i, 0)),
                   pl.BlockSpec((1, tm, nq), lambda b, i: (b, i, 0))],
        out_shape=[jax.ShapeDtypeStruct((bsz, s, d), F32),
                   jax.ShapeDtypeStruct((bsz, s, d), BF16),
                   jax.ShapeDtypeStruct((bsz, s, nq), F32)],
        compiler_params=_params("parallel", "parallel"),
        name="out_proj",
    )(o_gdn, o_mla, x, mod, w_out, g, w_pq, keys)


PEER_SORT_TOKENS = 8


def _peer_key_topk(scores, part, nparts, after=None):
    t_all, width = scores.shape
    nk = N_KEYS
    ln = SC_LANES
    assert PEER_TOPK == ln
    nv = nk // ln
    rpt = width // nk
    grp = PEER_SORT_TOKENS
    toks = t_all // nparts
    tok_per_w = toks // SC_WORKERS
    n_groups = tok_per_w // grp
    assert n_groups * grp * SC_WORKERS * nparts == t_all and rpt % 2 == 0
    tok_base = part * toks
    rows = toks * rpt
    mesh = plsc.VectorSubcoreMesh(core_axis_name="c", subcore_axis_name="s",
                                  num_cores=SC_CORES, num_subcores=SC_SUBCORES)

    @functools.partial(
        pl.kernel, mesh=mesh,
        compiler_params=pltpu.CompilerParams(needs_layout_passes=False),
        out_type=[jax.ShapeDtypeStruct((rows * ln,), F32),
                  jax.ShapeDtypeStruct((rows * ln,), jnp.int32)],
        scratch_types=[pltpu.VMEM((grp, width), F32),
                       pltpu.VMEM((grp * rpt * ln,), F32),
                       pltpu.VMEM((grp * rpt * ln,), jnp.int32)],
    )
    def keys_kernel(s_hbm, *rest):
        ov_hbm, oi_hbm, buf, outv, outi = rest[-5:]
        wid = lax.axis_index("s") * SC_CORES + lax.axis_index("c")
        lane = lax.iota(jnp.int32, ln)

        def merge(a, b):
            ka, va = a
            kb, vb = b
            kbr = lax.rev(kb, (0,))
            vbr = lax.rev(vb, (0,))
            take_a = ka >= kbr
            kc = jnp.where(take_a, ka, kbr)
            vc = jnp.where(take_a, va, vbr)
            vi, ki = plsc.sort_key_val(vc, kc, descending=False)
            return plsc.sort_key_val(ki, vi, descending=True)

        def one_row(t, row):
            parts = []
            for j in range(nv):
                kj = buf[t, pl.ds(row * nk + j * ln, ln)]
                parts.append(plsc.sort_key_val(kj, lane + j * ln, descending=True))
            while len(parts) > 1:
                parts = [merge(parts[i], parts[i + 1]) for i in range(0, len(parts), 2)]
            out = (t * rpt + row) * ln
            outv[pl.ds(out, ln)] = parts[0][0]
            outi[pl.ds(out, ln)] = parts[0][1]

        def group_body(g, carry):
            tok0 = wid * tok_per_w + g * grp
            pltpu.sync_copy(s_hbm.at[pl.ds(tok_base + tok0, grp)], buf)

            def rows_body(i, c2):
                t = i // (rpt // 2)
                row = (i % (rpt // 2)) * 2
                one_row(t, row)
                one_row(t, row + 1)
                return c2
            lax.fori_loop(0, grp * rpt // 2, rows_body, 0)
            pltpu.sync_copy(outv, ov_hbm.at[pl.ds(tok0 * rpt * ln, grp * rpt * ln)])
            pltpu.sync_copy(outi, oi_hbm.at[pl.ds(tok0 * rpt * ln, grp * rpt * ln)])
            return carry
        lax.fori_loop(0, n_groups, group_body, 0)

    args = (scores,) if after is None else (scores, after)
    ov, oi = keys_kernel(*args)
    return ov.reshape(toks, rpt * ln), oi.reshape(toks, rpt * ln)


PEER_CAND_ROWS = 2 * SUBLANES + (SUBLANES - 1) * SUBLANES + SUBLANES


def _peer_select_kernel(stop_in_ref, itop_in_ref, *rest):
    idx_ref, gate_ref, stop_ref, itop_ref, cand_ref, cidx_ref, best_ref, idxt_ref, gatet_ref = rest[-9:]
    tb = stop_in_ref.shape[0]
    kk = PEER_TOPK
    assert kk == 2 * SUBLANES
    stop_ref[...] = stop_in_ref[...].T
    itop_ref[...] = itop_in_ref[...].astype(F32).T

    def extract(vals, row, n):
        m = jnp.max(vals, axis=0, keepdims=True)
        pos = jnp.min(jnp.where(vals == m, row, float(n)), axis=0, keepdims=True)
        return m, pos, jnp.where(row == pos, -jnp.inf, vals)

    row_c = lax.broadcasted_iota(jnp.int32, (PEER_CAND_ROWS, tb), 0).astype(F32)
    row_8 = lax.broadcasted_iota(jnp.int32, (SUBLANES, tb), 0)

    def build_candidates(h):
        r1 = 2 * h * kk
        r2 = r1 + kk
        cand_ref[h, 0:kk, :] = stop_ref[r1:r1 + 1, :] + stop_ref[r2:r2 + kk, :]
        cidx_ref[h, 0:kk, :] = itop_ref[r1:r1 + 1, :] * float(N_KEYS) + itop_ref[r2:r2 + kk, :]
        s2 = stop_ref[r2:r2 + SUBLANES, :]
        i2 = itop_ref[r2:r2 + SUBLANES, :]
        for a in range(1, SUBLANES):
            r0 = kk + (a - 1) * SUBLANES
            cand_ref[h, r0:r0 + SUBLANES, :] = jnp.where(
                row_8 < kk // (a + 1), stop_ref[r1 + a:r1 + a + 1, :] + s2, -jnp.inf)
            cidx_ref[h, r0:r0 + SUBLANES, :] = itop_ref[r1 + a:r1 + a + 1, :] * float(N_KEYS) + i2
        r0 = kk + (SUBLANES - 1) * SUBLANES
        cand_ref[h, r0:r0 + SUBLANES, :] = (stop_ref[r1 + SUBLANES:r1 + kk, :]
                                           + stop_ref[r2:r2 + 1, :])
        cidx_ref[h, r0:r0 + SUBLANES, :] = (itop_ref[r1 + SUBLANES:r1 + kk, :] * float(N_KEYS)
                                           + itop_ref[r2:r2 + 1, :])

    for h in range(PEER_HEADS):
        build_candidates(h)
    cvals = [cand_ref[h] for h in range(PEER_HEADS)]
    cidx = [cidx_ref[h] for h in range(PEER_HEADS)]
    for r in range(kk):
        for h in range(PEER_HEADS):
            m, pos, cvals[h] = extract(cvals[h], row_c, PEER_CAND_ROWS)
            best_ref[h, r:r + 1, :] = m
            idxt_ref[h * kk + r:h * kk + r + 1, :] = jnp.max(
                jnp.where(row_c == pos, cidx[h], -1.0), axis=0, keepdims=True)
    for h in range(PEER_HEADS):
        best = best_ref[h]
        e = jnp.exp(best - best[0:1, :])
        gatet_ref[h * kk:(h + 1) * kk, :] = e / jnp.sum(e, axis=0, keepdims=True)
    idx_ref[...] = idxt_ref[...].T.astype(jnp.int32)
    gate_ref[...] = gatet_ref[...].T


def _peer_select(stop, itop, tb, after=None):
    t, w = stop.shape
    hk = PEER_HEADS * PEER_TOPK
    kk = PEER_TOPK
    in_specs = [pl.BlockSpec((tb, w), lambda i: (i, 0)),
                pl.BlockSpec((tb, w), lambda i: (i, 0))]
    args = [stop, itop]
    if after is not None:
        in_specs.append(pl.BlockSpec(memory_space=pl.ANY))
        args.append(after)
    return pl.pallas_call(
        _peer_select_kernel,
        grid=(t // tb,),
        in_specs=in_specs,
        out_specs=[pl.BlockSpec((tb, hk), lambda i: (i, 0)),
                   pl.BlockSpec((tb, hk), lambda i: (i, 0))],
        out_shape=[jax.ShapeDtypeStruct((t, hk), jnp.int32),
                   jax.ShapeDtypeStruct((t, hk), F32)],
        scratch_shapes=[pltpu.VMEM((w, tb), F32), pltpu.VMEM((w, tb), F32),
                        pltpu.VMEM((PEER_HEADS, PEER_CAND_ROWS, tb), F32),
                        pltpu.VMEM((PEER_HEADS, PEER_CAND_ROWS, tb), F32),
                        pltpu.VMEM((PEER_HEADS, kk, tb), F32),
                        pltpu.VMEM((hk, tb), F32), pltpu.VMEM((hk, tb), F32)],
        compiler_params=_params("parallel"),
        name="peer_select",
    )(*args)


PEER_TOKEN_GROUP = 16
PEER_PARTS = 16


def _peer_gate_matrix(idx, gate, n_experts):
    t_total, hk = idx.shape
    ln = SC_LANES
    nu = hk // ln
    grp = PEER_TOKEN_GROUP
    tok_per_w = t_total // SC_WORKERS
    n_groups = tok_per_w // grp
    assert n_groups * grp * SC_WORKERS == t_total and grp % 2 == 0
    mesh = plsc.VectorSubcoreMesh(core_axis_name="c", subcore_axis_name="s",
                                  num_cores=SC_CORES, num_subcores=SC_SUBCORES)

    @functools.partial(
        pl.kernel, mesh=mesh,
        compiler_params=pltpu.CompilerParams(needs_layout_passes=False),
        out_type=jax.ShapeDtypeStruct((t_total, n_experts), F32),
        scratch_types=[
            pltpu.VMEM((grp * hk,), jnp.int32),
            pltpu.VMEM((grp * hk,), F32),
            pltpu.VMEM((n_experts,), F32),
            pltpu.VMEM((n_experts,), F32),
            pltpu.SemaphoreType.DMA((2,)),
        ],
    )
    def gate_kernel(idx_hbm, gate_hbm, g_hbm, idx_v, gate_v, row0_v, row1_v, sem):
        rows = (row0_v, row1_v)
        wid = lax.axis_index("s") * SC_CORES + lax.axis_index("c")
        base = wid * tok_per_w
        zero = jnp.zeros((ln,), F32)

        def zero_body(c, carry):
            row0_v[pl.ds(c * ln, ln)] = zero
            row1_v[pl.ds(c * ln, ln)] = zero
            return carry
        lax.fori_loop(0, n_experts // ln, zero_body, 0)

        def out_copy(tok, slot):
            return pltpu.make_async_copy(rows[slot], g_hbm.at[tok], sem.at[slot])

        def group_body(g, carry):
            tok0 = base + g * grp
            pltpu.sync_copy(idx_hbm.at[pl.ds(tok0 * hk, grp * hk)], idx_v)
            pltpu.sync_copy(gate_hbm.at[pl.ds(tok0 * hk, grp * hk)], gate_v)

            def pair_body(i, carry2):
                for slot in range(2):
                    t = i * 2 + slot
                    for u in range(nu):
                        sl = pl.ds(t * hk + u * ln, ln)
                        plsc.addupdate_scatter(rows[slot], [idx_v[sl]], gate_v[sl])
                    out_copy(tok0 + t, slot).start()
                for slot in range(2):
                    t = i * 2 + slot
                    out_copy(tok0 + t, slot).wait()
                    for u in range(nu):
                        plsc.store_scatter(rows[slot], [idx_v[pl.ds(t * hk + u * ln, ln)]], zero)
                return carry2
            lax.fori_loop(0, grp // 2, pair_body, 0)
            return carry
        lax.fori_loop(0, n_groups, group_body, 0)

    return gate_kernel(idx.reshape(-1), gate.reshape(-1))


def _peer_dense_kernel(h_ref, g_ref, u_ref, v_ref, x1_ref, mod_ref, ng_ref, *rest, normalize):
    o_ref = rest[-1]
    e = pl.program_id(1)
    s = _dot_nt(h_ref[...], u_ref[...])
    p = (jax.nn.gelu(s) * g_ref[...]).astype(v_ref.dtype)
    contrib = _dot(p, v_ref[...])

    @pl.when(e == 0)
    def _():
        o_ref[...] = contrib

    @pl.when(e > 0)
    def _():
        o_ref[...] += contrib

    @pl.when(e == pl.num_programs(1) - 1)
    def _():
        x2 = x1_ref[...] + mod_ref[0][5:6] * o_ref[...]
        o_ref[...] = _rms(x2) * ng_ref[...] if normalize else x2


def _peer_dense(h, gmat, u, v, x1, mod, ng, tb, eb, part, normalize, prev, after=None):
    t = gmat.shape[0]
    t_all, d = h.shape
    n_e = u.shape[0]
    blk0 = part * (t // tb)
    assert (t_all // mod.shape[0]) % tb == 0
    blk_per_batch = t_all // mod.shape[0] // tb
    tok_blk = pl.BlockSpec((tb, d), lambda i, e: (i + blk0, 0))
    in_specs = [tok_blk,
                pl.BlockSpec((tb, eb), lambda i, e: (i, e)),
                pl.BlockSpec((eb, d), lambda i, e: (e, 0)),
                pl.BlockSpec((eb, d), lambda i, e: (e, 0)),
                tok_blk,
                pl.BlockSpec((1, N_MOD, d), lambda i, e: ((i + blk0) // blk_per_batch, 0, 0)),
                pl.BlockSpec((1, d), lambda i, e: (0, 0))]
    args = [h, gmat, u, v, x1, mod, ng]
    aliases = {}
    if prev is not None:
        in_specs.append(pl.BlockSpec(memory_space=pl.ANY))
        args.append(prev)
        aliases = {len(args) - 1: 0}
    if after is not None:
        in_specs.append(pl.BlockSpec(memory_space=pl.ANY))
        args.append(after)
    return pl.pallas_call(
        functools.partial(_peer_dense_kernel, normalize=normalize),
        grid=(t // tb, n_e // eb),
        in_specs=in_specs,
        out_specs=tok_blk,
        out_shape=jax.ShapeDtypeStruct((t_all, d), F32),
        input_output_aliases=aliases,
        compiler_params=pltpu.CompilerParams(dimension_semantics=("parallel", "arbitrary"),
                                             vmem_limit_bytes=PEER_DENSE_VMEM),
        name="peer_dense",
    )(*args)


def _split_w_in(w):
    gw = GDN_HEADS * GDN_HEAD_DIM
    sizes = [gw] * 4 + [GDN_HEADS] * 2 + [Q_LORA_RANK, KV_LORA_RANK, QK_ROPE_DIM]
    offs = [0]
    for sz in sizes:
        offs.append(offs[-1] + sz)
    parts = [w[:, offs[i]:offs[i + 1]] for i in range(len(sizes))]
    gq, gk, gv, gz, ga, gb, cq, ckv, kr = parts
    d = w.shape[0]
    half = QK_ROPE_DIM // 2
    zeros = lambda n: jnp.zeros((d, n), w.dtype)
    gdn = jnp.concatenate([gq, gk, gv, gz, ga, gb, zeros(LANES - 2 * GDN_HEADS)], axis=1)
    kr_sw = jnp.concatenate([kr[:, half:], kr[:, :half]], axis=1)
    mla = jnp.concatenate([cq, ckv, kr, zeros(LANES - QK_ROPE_DIM),
                           kr_sw, zeros(LANES - QK_ROPE_DIM)], axis=1)
    return jnp.concatenate([gdn, mla], axis=1).astype(BF16), gdn.shape[1], mla.shape[1]


def _split_w_uq(w):
    r = w.shape[0]
    half = QK_ROPE_DIM // 2
    hd = QK_NOPE_DIM + QK_ROPE_DIM
    main, swapped = [], []
    for h in range(MLA_HEADS):
        nope = w[:, h * hd:h * hd + QK_NOPE_DIM]
        rope = w[:, h * hd + QK_NOPE_DIM:(h + 1) * hd]
        main += [nope, rope, jnp.zeros((r, MLA_QK_PAD - hd), w.dtype)]
        swapped += [rope[:, half:], rope[:, :half], jnp.zeros((r, LANES - QK_ROPE_DIM), w.dtype)]
    return jnp.concatenate(main + swapped, axis=1).astype(BF16)


def _rotary_tables(positions):
    half = QK_ROPE_DIM // 2
    inv_freq = ROPE_THETA ** (-jnp.arange(half, dtype=F32) / half)
    ang = positions.astype(F32)[..., None] * inv_freq
    cos, sin = jnp.cos(ang), jnp.sin(ang)
    z = jnp.zeros(cos.shape[:-1] + (LANES - QK_ROPE_DIM,), F32)
    return (jnp.concatenate([cos, cos, z], axis=-1),
            jnp.concatenate([-sin, sin, z], axis=-1))


def _block(n, pref):
    return pref if n % pref == 0 else n


def kernel(x, c, positions, ln_mix_g, w_in, conv_w, a_log, dt_bias, gdn_norm_g, q_norm_g, w_uq, kv_norm_g, w_ukv, w_out, ln_ffn_g, w_pq, sub_keys, expert_u, expert_v, w_ada, b_ada, final_norm_g):
    bsz, s, d = x.shape
    depth = w_in.shape[0]
    cosr, sinr = _rotary_tables(positions)
    tm = _block(s, 512)
    for layer in range(depth):
        mod = _mod(c, w_ada[layer], b_ada[layer]).reshape(bsz, N_MOD, d)
        w_cat, gdn_cols, mla_cols = _split_w_in(w_in[layer])
        gdn_in, mla_in = _in_proj(x, mod, ln_mix_g[layer].reshape(1, d), w_cat, gdn_cols, mla_cols, tm)
        o_gdn = _gdn(gdn_in, conv_w[layer], a_log[layer], dt_bias[layer], gdn_norm_g[layer],
                     _block(s, 256))
        q, k, v = _mla_prep(mla_in, cosr, sinr, q_norm_g[layer].reshape(1, -1),
                            kv_norm_g[layer].reshape(1, -1), _split_w_uq(w_uq[layer]),
                            w_ukv[layer].astype(BF16), _block(s, 512))
        o_mla = _attn(q, k, v, _block(s, 1024))
        x1, h2, scores = _out_proj(o_gdn, o_mla, x, mod, w_out[layer].astype(BF16),
                                   ln_ffn_g[layer].reshape(1, d), w_pq[layer].astype(BF16),
                                   sub_keys[layer].astype(BF16), tm)
        n_experts = expert_u.shape[1]
        u16 = expert_u[layer].astype(BF16)
        v16 = expert_v[layer].astype(BF16)
        scores2 = scores.reshape(bsz * s, -1)
        h22 = h2.reshape(bsz * s, d)
        x12 = x1.reshape(bsz * s, d)
        nparts = PEER_PARTS if (bsz * s) % (PEER_PARTS * 1024) == 0 and s % 1024 == 0 else 1
        tpart = bsz * s // nparts
        last = layer + 1 == depth
        sel = {}
        gmats = {}

        def select(part, dense_out):
            stop, itop = _peer_key_topk(scores2, part, nparts, gmats.get(part - 2))
            sel[part] = _peer_select(stop, itop, _block(tpart, 256), dense_out)

        out = None
        select(0, None)
        for part in range(nparts):
            if part + 1 < nparts:
                select(part + 1, out)
            idx, gate = sel[part]
            gmats[part] = _peer_gate_matrix(idx, gate, n_experts)
            after = sel[part + 1][0] if part + 1 < nparts else None
            out = _peer_dense(h22, gmats[part], u16, v16, x12, mod, final_norm_g.reshape(1, d),
                              _block(tpart, 1024), _block(n_experts, 1024), part, last, out, after)
        x = out.reshape(bsz, s, d)
    return x
```
